```python
import math
import jax, jax.numpy as jnp
from jax import lax
import numpy as np

D_MODEL = 2048
BATCH = 1
SEQ = 8192
DEPTH = 2
DEC_BATCH = 16
DEC_SEQ = 64
PAST_LEN = 4096

CHUNK = 64
N_MIXERS = 2
N_HGRN_LAYERS = (DEPTH + 1) // 2
N_ATTN_LAYERS = DEPTH // 2
HGRN_HEADS = 16
HGRN_DK = D_MODEL // HGRN_HEADS
HGRN_DV = D_MODEL // HGRN_HEADS
HGRN_D_KEY = HGRN_HEADS * HGRN_DK
HGRN_BLOCK = 16
DIFF_HEADS = 8
DIFF_DH = D_MODEL // DIFF_HEADS // 2
Q_BLOCK = 128
D_FF = 5632
CONV_W = 3
EPS = 1e-6

kernel_name = "hgrn2_diffattn_convffn_stream_step"


def rms_norm(x, g):
    xf = x.astype(jnp.float32)
    y = xf * lax.rsqrt(jnp.mean(xf * xf, axis=-1, keepdims=True) + EPS)
    return (y * g.astype(jnp.float32)).astype(x.dtype)


def hgrn_block_step(S, blk):
    q, k, v, g = blk
    C = q.shape[1]
    b = jnp.cumsum(g, axis=1)
    causal = jnp.tril(jnp.ones((C, C), bool))[None, :, :, None, None]
    decay = jnp.exp(jnp.where(causal, b[:, :, None] - b[:, None, :], -jnp.inf))
    scores = jnp.einsum('bthd,btshd,bshd->bhts', q, decay, k)
    o = (jnp.einsum('bthd,bhde->bthe', q * jnp.exp(b), S)
         + jnp.einsum('bhts,bshe->bthe', scores, v))
    b_last = b[:, -1]
    S = (jnp.exp(b_last)[..., None] * S
         + jnp.einsum('bshd,bshe->bhde', k * jnp.exp(b_last[:, None] - b), v))
    return S, o


def hgrn_scan(q, k, v, logf, S0):
    B, L, H, _ = q.shape
    pad = (-L) % HGRN_BLOCK
    padding = ((0, 0), (0, pad), (0, 0), (0, 0))
    q, k, v, logf = [jnp.pad(t, padding) for t in (q, k, v, logf)]
    n = (L + pad) // HGRN_BLOCK

    def blocks(t):
        return t.reshape(B, n, HGRN_BLOCK, H, t.shape[-1]).transpose(1, 0, 2, 3, 4)

    S, o = lax.scan(hgrn_block_step, S0, (blocks(q), blocks(k), blocks(v), blocks(logf)))
    o = o.transpose(1, 0, 2, 3, 4).reshape(B, n * HGRN_BLOCK, H, HGRN_DV)[:, :L]
    return o, S


def hgrn_mixer(h, S0, lb, w_in, w_out, out_norm):
    B, L, _ = h.shape
    f32 = jnp.float32
    q, fz, i_in, gate = jnp.split(h @ w_in, 4, axis=-1)
    f = lb + (1.0 - lb) * jax.nn.sigmoid(fz.astype(f32))
    kshape = (B, L, HGRN_HEADS, HGRN_DK)
    o, S = hgrn_scan(q.astype(f32).reshape(kshape), (1.0 - f).reshape(kshape),
                     i_in.astype(f32).reshape(B, L, HGRN_HEADS, HGRN_DV),
                     jnp.log(f).reshape(kshape), S0.astype(f32))
    o = rms_norm(o, out_norm).reshape(B, L, HGRN_HEADS * HGRN_DV) * jax.nn.silu(gate.astype(f32))
    return o.astype(h.dtype) @ w_out, S.astype(S0.dtype)


def diff_project(h, w_in, q_g, k_g):
    B, L, _ = h.shape
    q, k, v = jnp.split(h @ w_in, 3, axis=-1)
    q = rms_norm(q.reshape(B, L, DIFF_HEADS, 2, DIFF_DH), q_g)
    k = rms_norm(k.reshape(B, L, DIFF_HEADS, 2, DIFF_DH), k_g)
    v = v.reshape(B, L, DIFF_HEADS, 2 * DIFF_DH)
    return q, k, v


def diff_attend(q, k, v, mask, lam):
    s = jnp.einsum('bqhcd,bkhcd->bhcqk', q, k).astype(jnp.float32) * (DIFF_DH ** -0.5)
    p = jax.nn.softmax(jnp.where(mask, s, -jnp.inf), axis=-1)
    a = p[:, :, 0] - lam * p[:, :, 1]
    return jnp.einsum('bhqk,bkhe->bqhe', a, v.astype(jnp.float32))


def diff_attn_prompt(q, k, v, lam):
    B, L = q.shape[:2]
    n = L // Q_BLOCK
    key_pos = jnp.arange(L)

    def one_block(i):
        qb = lax.dynamic_slice_in_dim(q, i * Q_BLOCK, Q_BLOCK, axis=1)
        q_pos = i * Q_BLOCK + jnp.arange(Q_BLOCK)
        mask = key_pos[None, :] < ((q_pos // CHUNK + 1) * CHUNK)[:, None]
        return diff_attend(qb, k, v, mask, lam)

    o = lax.map(one_block, jnp.arange(n))
    return o.transpose(1, 0, 2, 3, 4).reshape(B, L, DIFF_HEADS, 2 * DIFF_DH)


def diff_output(o, subln, lam_init, w_out, dtype):
    B, L = o.shape[:2]
    o = rms_norm(o, subln) * (1.0 - lam_init)
    return o.reshape(B, L, D_MODEL).astype(dtype) @ w_out


def conv_ffn(h, prev, w_up, conv_w, conv_b, w_down):
    L = h.shape[1]
    gate, up = jnp.split(h @ w_up, 2, axis=-1)
    gp = jnp.concatenate([prev.astype(gate.dtype), gate], axis=1)
    conv = conv_b + conv_w[CONV_W - 1] * gp[:, CONV_W - 1:]
    for j in range(CONV_W - 1):
        conv = conv + conv_w[j] * gp[:, j:j + L]
    y = (jax.nn.silu(conv) * up) @ w_down
    return y, gp[:, -(CONV_W - 1):]


def setup_inputs(seed: int = 0) -> dict:
    key = jax.random.key(seed)
    ks = jax.random.split(key, 26)
    f32 = jnp.float32

    def nrm(k, shape, scale):
        return jax.random.normal(k, shape, f32) * scale

    d = D_MODEL
    return {
        "x_prompt": nrm(ks[0], (BATCH, SEQ, d), 1.0),
        "x_sample": nrm(ks[1], (DEC_BATCH, DEC_SEQ, d), 1.0),
        "state_hgrn": nrm(ks[2], (N_HGRN_LAYERS, DEC_BATCH, HGRN_HEADS, HGRN_DK, HGRN_DV), 0.5),
        "cache_k": nrm(ks[3], (N_ATTN_LAYERS, DEC_BATCH, PAST_LEN, DIFF_HEADS, 2, DIFF_DH), 1.0),
        "cache_v": nrm(ks[4], (N_ATTN_LAYERS, DEC_BATCH, PAST_LEN, DIFF_HEADS, 2 * DIFF_DH), 1.0),
        "state_ffn_conv": nrm(ks[5], (DEPTH, DEC_BATCH, CONV_W - 1, D_FF), 1.0),
        "norm_mix": 1.0 + nrm(ks[6], (DEPTH, d), 0.02),
        "norm_ffn": 1.0 + nrm(ks[7], (DEPTH, d), 0.02),
        "hgrn_lower_bounds": nrm(ks[8], (DEPTH + 1, HGRN_D_KEY), 0.5),
        "w_hgrn_in": nrm(ks[9], (N_HGRN_LAYERS, d, 4 * HGRN_D_KEY), d ** -0.5),
        "w_hgrn_out": nrm(ks[10], (N_HGRN_LAYERS, HGRN_HEADS * HGRN_DV, d), (HGRN_HEADS * HGRN_DV) ** -0.5),
        "hgrn_out_norm": 1.0 + nrm(ks[11], (N_HGRN_LAYERS, HGRN_DV), 0.02),
        "w_diff_in": nrm(ks[12], (N_ATTN_LAYERS, d, 3 * d), d ** -0.5),
        "w_diff_out": nrm(ks[13], (N_ATTN_LAYERS, d, d), d ** -0.5),
        "diff_q_norm": 1.0 + nrm(ks[14], (N_ATTN_LAYERS, DIFF_DH), 0.02),
        "diff_k_norm": 1.0 + nrm(ks[15], (N_ATTN_LAYERS, DIFF_DH), 0.02),
        "diff_lambda_q1": nrm(ks[16], (N_ATTN_LAYERS, DIFF_DH), 0.1),
        "diff_lambda_k1": nrm(ks[17], (N_ATTN_LAYERS, DIFF_DH), 0.1),
        "diff_lambda_q2": nrm(ks[18], (N_ATTN_LAYERS, DIFF_DH), 0.1),
        "diff_lambda_k2": nrm(ks[19], (N_ATTN_LAYERS, DIFF_DH), 0.1),
        "diff_subln": 1.0 + nrm(ks[20], (N_ATTN_LAYERS, 2 * DIFF_DH), 0.02),
        "w_ffn_up": nrm(ks[21], (DEPTH, d, 2 * D_FF), d ** -0.5),
        "ffn_conv_w": nrm(ks[22], (DEPTH, CONV_W, D_FF), CONV_W ** -0.5),
        "ffn_conv_b": nrm(ks[23], (DEPTH, D_FF), 0.01),
        "w_ffn_down": nrm(ks[24], (DEPTH, D_FF, d), D_FF ** -0.5),
    }


def reference(x_prompt, x_sample, state_hgrn, cache_k, cache_v, state_ffn_conv,
              norm_mix, norm_ffn, hgrn_lower_bounds, w_hgrn_in, w_hgrn_out, hgrn_out_norm,
              w_diff_in, w_diff_out, diff_q_norm, diff_k_norm,
              diff_lambda_q1, diff_lambda_k1, diff_lambda_q2, diff_lambda_k2, diff_subln,
              w_ffn_up, ffn_conv_w, ffn_conv_b, w_ffn_down):
    f32 = jnp.float32
    lb_table = jnp.cumsum(jax.nn.softmax(hgrn_lower_bounds.astype(f32), axis=0), axis=0)
    xp, xs = x_prompt, x_sample
    bp = xp.shape[0]
    hgrn_p, hgrn_s = [], []
    kp_rows, vp_rows, ks_rows, vs_rows = [], [], [], []
    conv_p, conv_s = [], []
    for i in range(DEPTH):
        j = i // N_MIXERS
        hp = rms_norm(xp, norm_mix[i])
        hs = rms_norm(xs, norm_mix[i])
        if i % N_MIXERS == 0:
            lb = lb_table[i]
            s0 = jnp.zeros((bp, HGRN_HEADS, HGRN_DK, HGRN_DV), state_hgrn.dtype)
            yp, sp = hgrn_mixer(hp, s0, lb, w_hgrn_in[j], w_hgrn_out[j], hgrn_out_norm[j])
            ys, ss = hgrn_mixer(hs, state_hgrn[j], lb, w_hgrn_in[j], w_hgrn_out[j], hgrn_out_norm[j])
            hgrn_p.append(sp)
            hgrn_s.append(ss)
        else:
            lam_init = 0.8 - 0.6 * math.exp(-0.3 * i)
            lam = (jnp.exp(jnp.sum(diff_lambda_q1[j].astype(f32) * diff_lambda_k1[j].astype(f32)))
                   - jnp.exp(jnp.sum(diff_lambda_q2[j].astype(f32) * diff_lambda_k2[j].astype(f32)))
                   + lam_init)
            qp, kp, vp = diff_project(hp, w_diff_in[j], diff_q_norm[j], diff_k_norm[j])
            op = diff_attn_prompt(qp, kp, vp, lam)
            qs, kss, vss = diff_project(hs, w_diff_in[j], diff_q_norm[j], diff_k_norm[j])
            k_all = jnp.concatenate([cache_k[j].astype(kss.dtype), kss], axis=1)
            v_all = jnp.concatenate([cache_v[j].astype(vss.dtype), vss], axis=1)
            mask = jnp.ones((qs.shape[1], k_all.shape[1]), bool)
            os_ = diff_attend(qs, k_all, v_all, mask, lam)
            yp = diff_output(op, diff_subln[j], lam_init, w_diff_out[j], hp.dtype)
            ys = diff_output(os_, diff_subln[j], lam_init, w_diff_out[j], hs.dtype)
            kp_rows.append(kp)
            vp_rows.append(vp)
            ks_rows.append(kss)
            vs_rows.append(vss)
        xp = xp + yp
        xs = xs + ys
        fp, cp = conv_ffn(rms_norm(xp, norm_ffn[i]), jnp.zeros((bp, CONV_W - 1, D_FF), xp.dtype),
                          w_ffn_up[i], ffn_conv_w[i], ffn_conv_b[i], w_ffn_down[i])
        fs, cs = conv_ffn(rms_norm(xs, norm_ffn[i]), state_ffn_conv[i],
                          w_ffn_up[i], ffn_conv_w[i], ffn_conv_b[i], w_ffn_down[i])
        xp = xp + fp
        xs = xs + fs
        conv_p.append(cp)
        conv_s.append(cs)
    return (xp, xs, jnp.stack(hgrn_p), jnp.stack(hgrn_s),
            jnp.stack(kp_rows), jnp.stack(vp_rows), jnp.stack(ks_rows), jnp.stack(vs_rows),
            jnp.stack(conv_p), jnp.stack(conv_s))
```

```python
import functools
import math

import jax
import jax.numpy as jnp
from jax import lax
from jax.experimental import pallas as pl
from jax.experimental.pallas import tpu as pltpu

EPS = 1e-6
CHUNK = 64
HGRN_BLOCK = 16
CONV_W = 3
LANES = 128
V7X_VMEM_BYTES = 64 * 1024 * 1024
VMEM_LIMIT = V7X_VMEM_BYTES - 8 * 1024 * 1024

F32 = jnp.float32
BF16 = jnp.bfloat16


def _params(*sem):
    return pltpu.CompilerParams(dimension_semantics=sem, vmem_limit_bytes=VMEM_LIMIT)


def _rows3(a):
    return a.reshape(a.shape[0], 1, a.shape[1])


def _tile(n, pref):
    t = min(n, pref)
    while n % t:
        t //= 2
    return t


def _norm_kernel(x_ref, g_ref, o_ref):
    x = x_ref[...]
    y = x * lax.rsqrt(jnp.mean(x * x, axis=-1, keepdims=True) + EPS)
    o_ref[...] = (y * g_ref[...]).astype(o_ref.dtype)


def _rms_norm_bf16(x, gains, layer):
    m, d = x.shape
    bm = _tile(m, 512)
    return pl.pallas_call(
        _norm_kernel,
        grid=(m // bm,),
        in_specs=[pl.BlockSpec((bm, d), lambda i: (i, 0)),
                  pl.BlockSpec((None, 1, d), lambda i: (layer, 0, 0))],
        out_specs=pl.BlockSpec((bm, d), lambda i: (i, 0)),
        out_shape=jax.ShapeDtypeStruct((m, d), BF16),
        compiler_params=_params("parallel"),
        name="rms_norm",
    )(x, _rows3(gains))


def _mm_kernel(h_ref, w_ref, o_ref, wb_ref):
    @pl.when(pl.program_id(1) == 0)
    def _():
        wb_ref[...] = w_ref[...].astype(BF16)

    o_ref[...] = jnp.dot(h_ref[...], wb_ref[...], preferred_element_type=F32)


def _mm_res_kernel(h_ref, w_ref, r_ref, o_ref, wb_ref):
    @pl.when(pl.program_id(1) == 0)
    def _():
        wb_ref[...] = w_ref[...].astype(BF16)

    o_ref[...] = r_ref[...] + jnp.dot(h_ref[...], wb_ref[...], preferred_element_type=F32)


def _matmul(h, w, layer, *, col0=0, ncols=None, res=None, bm=1024, bn=512, name="matmul"):
    m, k = h.shape
    ncols = w.shape[2] - col0 if ncols is None else ncols
    bm, bn = _tile(m, bm), _tile(ncols, bn)
    assert col0 % bn == 0
    c0 = col0 // bn
    in_specs = [pl.BlockSpec((bm, k), lambda j, i: (i, 0)),
                pl.BlockSpec((None, k, bn), lambda j, i: (layer, 0, c0 + j))]
    args = [h, w]
    kern = _mm_kernel
    if res is not None:
        in_specs.append(pl.BlockSpec((bm, bn), lambda j, i: (i, j)))
        args.append(res)
        kern = _mm_res_kernel
    return pl.pallas_call(
        kern,
        grid=(ncols // bn, m // bm),
        in_specs=in_specs,
        out_specs=pl.BlockSpec((bm, bn), lambda j, i: (i, j)),
        out_shape=jax.ShapeDtypeStruct((m, ncols), F32),
        scratch_shapes=[pltpu.VMEM((k, bn), BF16)],
        compiler_params=_params("parallel", "arbitrary"),
        name=name,
    )(*args)


def _headnorm(acc, g, scale):
    outs = []
    for c in range(acc.shape[1] // LANES):
        blk = acc[:, c * LANES:(c + 1) * LANES]
        y = blk * lax.rsqrt(jnp.mean(blk * blk, axis=-1, keepdims=True) + EPS)
        outs.append(y * g * scale if scale != 1.0 else y * g)
    return jnp.concatenate(outs, axis=1) if len(outs) > 1 else outs[0]


def _proj_q_kernel(h_ref, w_ref, g_ref, qb_ref, wb_ref, *, scale):
    @pl.when(pl.program_id(1) == 0)
    def _():
        wb_ref[...] = w_ref[...].astype(BF16)

    acc = jnp.dot(h_ref[...], wb_ref[...], preferred_element_type=F32)
    qb_ref[...] = _headnorm(acc, g_ref[...], scale).astype(BF16)


def _proj_k_kernel(h_ref, w_ref, g_ref, kf_ref, kb_ref, wb_ref):
    @pl.when(pl.program_id(1) == 0)
    def _():
        wb_ref[...] = w_ref[...].astype(BF16)

    acc = jnp.dot(h_ref[...], wb_ref[...], preferred_element_type=F32)
    kn = _headnorm(acc, g_ref[...], 1.0)
    kf_ref[...] = kn
    kb_ref[...] = kn.astype(BF16)


def _proj_v_kernel(h_ref, w_ref, vf_ref, vb_ref, wb_ref):
    @pl.when(pl.program_id(1) == 0)
    def _():
        wb_ref[...] = w_ref[...].astype(BF16)

    acc = jnp.dot(h_ref[...], wb_ref[...], preferred_element_type=F32)
    vf_ref[...] = acc
    vb_ref[...] = acc.astype(BF16)


def _diff_project(h, w, layer, q_g, k_g, dh):
    m, d = h.shape
    bm, bn = _tile(m, 1024), _tile(d, 512)
    nj = d // bn
    grid = (nj, m // bm)
    h_spec = pl.BlockSpec((bm, d), lambda j, i: (i, 0))
    g_spec = pl.BlockSpec((None, 1, dh), lambda j, i: (layer, 0, 0))
    q_g, k_g = _rows3(q_g), _rows3(k_g)
    o_spec = pl.BlockSpec((bm, bn), lambda j, i: (i, j))

    def w_spec(seg):
        return pl.BlockSpec((None, d, bn), lambda j, i: (layer, 0, seg * nj + j))

    scratch = [pltpu.VMEM((d, bn), BF16)]
    cp = _params("parallel", "arbitrary")
    qb = pl.pallas_call(
        functools.partial(_proj_q_kernel, scale=dh ** -0.5),
        grid=grid, in_specs=[h_spec, w_spec(0), g_spec], out_specs=o_spec,
        out_shape=jax.ShapeDtypeStruct((m, d), BF16),
        scratch_shapes=scratch, compiler_params=cp, name="diff_proj_q",
    )(h, w, q_g)
    kf, kb = pl.pallas_call(
        _proj_k_kernel,
        grid=grid, in_specs=[h_spec, w_spec(1), g_spec], out_specs=[o_spec, o_spec],
        out_shape=[jax.ShapeDtypeStruct((m, d), F32), jax.ShapeDtypeStruct((m, d), BF16)],
        scratch_shapes=scratch, compiler_params=cp, name="diff_proj_k",
    )(h, w, k_g)
    vf, vb = pl.pallas_call(
        _proj_v_kernel,
        grid=grid, in_specs=[h_spec, w_spec(2)], out_specs=[o_spec, o_spec],
        out_shape=[jax.ShapeDtypeStruct((m, d), F32), jax.ShapeDtypeStruct((m, d), BF16)],
        scratch_shapes=scratch, compiler_params=cp, name="diff_proj_v",
    )(h, w)
    return qb, kf, kb, vf, vb


def _ffn_up_kernel(h_ref, wg_ref, wu_ref, cw_ref, cb_ref, st_ref, a_ref, tail_ref,
                   wgb_ref, wub_ref, g_ref, *, n_prompt_tiles, seq):
    i = pl.program_id(1)
    bm, bn = a_ref.shape
    nseg = bm // seq

    @pl.when(i == 0)
    def _():
        wgb_ref[...] = wg_ref[...].astype(BF16)
        wub_ref[...] = wu_ref[...].astype(BF16)
        g_ref[0:8, :] = jnp.zeros((8, bn), F32)

    h = h_ref[...]
    g = jnp.dot(h, wgb_ref[...], preferred_element_type=F32)
    u = jnp.dot(h, wub_ref[...], preferred_element_type=F32)
    g_ref[8:8 + bm, :] = g
    g1 = g_ref[7:7 + bm, :]
    g2 = g_ref[6:6 + bm, :]
    cw = cw_ref[...]
    base = cb_ref[...] + cw[2:3, :] * g

    def finish(g1, g2):
        conv = base + cw[1:2, :] * g1 + cw[0:1, :] * g2
        a_ref[...] = (conv * jax.nn.sigmoid(conv) * u).astype(BF16)

    @pl.when(i < n_prompt_tiles)
    def _():
        finish(g1, g2)

    @pl.when(i >= n_prompt_tiles)
    def _():
        st = st_ref[...]
        p2 = jnp.broadcast_to(st[:, 0:1, :], (nseg, seq, bn)).reshape(bm, bn)
        p1 = jnp.broadcast_to(st[:, 1:2, :], (nseg, seq, bn)).reshape(bm, bn)
        pos = lax.broadcasted_iota(jnp.int32, (bm, bn), 0) % seq
        finish(jnp.where(pos == 0, p1, g1),
               jnp.where(pos == 0, p2, jnp.where(pos == 1, p1, g2)))

    for n in range(nseg):
        end = 8 + (n + 1) * seq
        tail_ref[n] = g_ref[end - (CONV_W - 1):end, :]
    g_ref[0:8, :] = g_ref[bm:bm + 8, :]


def _ffn_up(h, w_up, conv_w, conv_b, conv_state, layer, n_prompt_rows, seq):
    m, d = h.shape
    dff = conv_w.shape[-1]
    nb = conv_state.shape[1]
    bm = _tile(math.gcd(n_prompt_rows, nb * seq), 1024)
    bn = _tile(dff, 512)
    assert bm % seq == 0 and (m - n_prompt_rows) == nb * seq
    nseg = bm // seq
    npt = n_prompt_rows // bm
    nj = dff // bn
    return pl.pallas_call(
        functools.partial(_ffn_up_kernel, n_prompt_tiles=npt, seq=seq),
        grid=(nj, m // bm),
        in_specs=[pl.BlockSpec((bm, d), lambda j, i: (i, 0)),
                  pl.BlockSpec((None, d, bn), lambda j, i: (layer, 0, j)),
                  pl.BlockSpec((None, d, bn), lambda j, i: (layer, 0, nj + j)),
                  pl.BlockSpec((None, CONV_W, bn), lambda j, i: (layer, 0, j)),
                  pl.BlockSpec((None, 1, bn), lambda j, i: (layer, 0, j)),
                  pl.BlockSpec((None, nseg, CONV_W - 1, bn),
                               lambda j, i: (layer, jnp.maximum(i - npt, 0), 0, j))],
        out_specs=[pl.BlockSpec((bm, bn), lambda j, i: (i, j)),
                   pl.BlockSpec((nseg, CONV_W - 1, bn), lambda j, i: (i, 0, j))],
        out_shape=[jax.ShapeDtypeStruct((m, dff), BF16),
                   jax.ShapeDtypeStruct((m // seq, CONV_W - 1, dff), F32)],
        scratch_shapes=[pltpu.VMEM((d, bn), BF16), pltpu.VMEM((d, bn), BF16),
                        pltpu.VMEM((bm + 8, bn), F32)],
        compiler_params=_params("parallel", "arbitrary"),
        name="ffn_up",
    )(h, w_up, w_up, conv_w, _rows3(conv_b), conv_state)


def _hgrn_kernel(q_ref, fz_ref, v_ref, gate_ref, lbp_ref, gout_ref, s0_ref, o_ref, sout_ref,
                 st_ref, b_ref, k_ref, qe_ref, kd_ref, oacc_ref, *, layer, carry):
    c = pl.program_id(1)
    t_rows = q_ref.shape[0]
    nblk = t_rows // HGRN_BLOCK

    if carry:
        @pl.when(c == 0)
        def _():
            st_ref[...] = jnp.zeros_like(st_ref)
    else:
        st_ref[...] = s0_ref[...].T

    lbp = lbp_ref[...]
    e = jnp.exp(lbp - jnp.max(lbp, axis=0, keepdims=True))
    lb = jnp.sum(e[0:layer + 1], axis=0, keepdims=True) / jnp.sum(e, axis=0, keepdims=True)

    f = lb + (1.0 - lb) * jax.nn.sigmoid(fz_ref[...])
    logf = jnp.log(f)
    kk = 1.0 - f
    pos = lax.broadcasted_iota(jnp.int32, (t_rows, LANES), 0) % HGRN_BLOCK
    b = logf
    suf = logf
    sh = 1
    while sh < HGRN_BLOCK:
        b = b + jnp.where(pos >= sh, pltpu.roll(b, sh, 0), 0.0)
        suf = suf + jnp.where(pos < HGRN_BLOCK - sh, pltpu.roll(suf, t_rows - sh, 0), 0.0)
        sh *= 2
    b_ref[...] = b
    k_ref[...] = kk
    qe_ref[...] = (q_ref[...] * jnp.exp(b)).astype(BF16)
    kd_ref[...] = (kk * jnp.exp(suf - logf)).astype(BF16)

    row = lax.broadcasted_iota(jnp.int32, (HGRN_BLOCK, LANES), 0)
    lane = lax.broadcasted_iota(jnp.int32, (HGRN_BLOCK, LANES), 1)

    def block(j, carry_):
        r0 = pl.multiple_of(j * HGRN_BLOCK, HGRN_BLOCK)
        rows = pl.ds(r0, HGRN_BLOCK)
        bb = b_ref[rows, :]
        qq = q_ref[rows, :]
        kb = k_ref[rows, :]
        vb = v_ref[rows, :].astype(BF16)
        sc = jnp.zeros((HGRN_BLOCK, LANES), F32)
        for s in range(HGRN_BLOCK):
            w = jnp.exp(bb - bb[s:s + 1, :]) * (qq * kb[s:s + 1, :])
            col = jnp.sum(w, axis=-1, keepdims=True)
            sc = jnp.where((lane == s) & (row >= s), col, sc)
        st = st_ref[...]
        o = lax.dot_general(qe_ref[rows, :], st.astype(BF16), (((1,), (1,)), ((), ())),
                            preferred_element_type=F32)
        o = o + jnp.dot(sc[:, 0:HGRN_BLOCK].astype(BF16), vb, preferred_element_type=F32)
        oacc_ref[rows, :] = o
        upd = lax.dot_general(vb, kd_ref[rows, :], (((0,), (0,)), ((), ())),
                              preferred_element_type=F32)
        st_ref[...] = st * jnp.exp(bb[HGRN_BLOCK - 1:HGRN_BLOCK, :]) + upd
        return carry_

    lax.fori_loop(0, nblk, block, 0)

    o = oacc_ref[...]
    y = o * lax.rsqrt(jnp.mean(o * o, axis=-1, keepdims=True) + EPS) * gout_ref[...]
    gate = gate_ref[...]
    o_ref[...] = (y * (gate * jax.nn.sigmoid(gate))).astype(BF16)

    if carry:
        @pl.when(c == pl.num_programs(1) - 1)
        def _():
            sout_ref[...] = st_ref[...].T
    else:
        sout_ref[...] = st_ref[...].T


def _hgrn_scan(proj, lower_bounds, out_norm, state, layer, jl, n_prompt_rows, seq):
    m = proj.shape[0]
    nb, nh, dk, dv = state.shape[1:]
    assert dk == LANES and dv == LANES
    scratch = lambda t: [pltpu.VMEM((dv, dk), F32), pltpu.VMEM((t, dk), F32), pltpu.VMEM((t, dk), F32),
                         pltpu.VMEM((t, dk), BF16), pltpu.VMEM((t, dk), BF16), pltpu.VMEM((t, dv), F32)]

    def specs(t, row_of):
        seg = lambda k: pl.BlockSpec((t, LANES), lambda h, c: (row_of(c), k * nh + h))
        return [seg(0), seg(1), seg(2), seg(3),
                pl.BlockSpec((lower_bounds.shape[0], LANES), lambda h, c: (0, h)),
                pl.BlockSpec((None, 1, dv), lambda h, c: (jl, 0, 0))]

    out_norm = _rows3(out_norm)
    tp = _tile(n_prompt_rows, 512)
    o_p, s_p = pl.pallas_call(
        functools.partial(_hgrn_kernel, layer=layer, carry=True),
        grid=(nh, n_prompt_rows // tp),
        in_specs=specs(tp, lambda c: c) + [pl.BlockSpec((None, None, None, dk, dv),
                                                        lambda h, c: (jl, 0, h, 0, 0))],
        out_specs=[pl.BlockSpec((tp, dv), lambda h, c: (c, h)),
                   pl.BlockSpec((None, dk, dv), lambda h, c: (h, 0, 0))],
        out_shape=[jax.ShapeDtypeStruct((n_prompt_rows, nh * dv), BF16),
                   jax.ShapeDtypeStruct((nh, dk, dv), F32)],
        scratch_shapes=scratch(tp),
        compiler_params=_params("parallel", "arbitrary"),
        name="hgrn_prompt",
    )(proj, proj, proj, proj, lower_bounds, out_norm, state)
    r0 = n_prompt_rows // seq
    o_s, s_s = pl.pallas_call(
        functools.partial(_hgrn_kernel, layer=layer, carry=False),
        grid=(nh, nb),
        in_specs=specs(seq, lambda c: r0 + c) + [pl.BlockSpec((None, None, None, dk, dv),
                                                               lambda h, c: (jl, c, h, 0, 0))],
        out_specs=[pl.BlockSpec((seq, dv), lambda h, c: (c, h)),
                   pl.BlockSpec((None, None, dk, dv), lambda h, c: (c, h, 0, 0))],
        out_shape=[jax.ShapeDtypeStruct((nb * seq, nh * dv), BF16),
                   jax.ShapeDtypeStruct((nb, nh, dk, dv), F32)],
        scratch_shapes=scratch(seq),
        compiler_params=_params("parallel", "arbitrary"),
        name="hgrn_sample",
    )(proj, proj, proj, proj, lower_bounds, out_norm, state)
    return jnp.concatenate([o_p, o_s], axis=0), s_p, s_s


def _softmax_step(c, q, k, v, m_ref, l_ref, acc_ref, mask):
    s = lax.dot_general(q, k, (((1,), (1,)), ((), ())), preferred_element_type=F32)
    if mask is not None:
        s = jnp.where(mask, s, -jnp.inf)
    m_prev = m_ref[c]
    m_new = jnp.maximum(m_prev, jnp.max(s, axis=-1, keepdims=True))
    alpha = jnp.exp(m_prev - m_new)
    p = jnp.exp(s - m_new)
    l_ref[c] = alpha * l_ref[c] + jnp.sum(p, axis=-1, keepdims=True)
    acc_ref[c] = alpha * acc_ref[c] + jnp.dot(p.astype(BF16), v, preferred_element_type=F32)
    m_ref[c] = m_new


def _diff_finish(a0, l0, a1, l1, lam_refs, subln, lam_init):
    lq1, lk1, lq2, lk2 = [r[...] for r in lam_refs]
    lam = (jnp.exp(jnp.sum(lq1 * lk1, axis=-1, keepdims=True))
           - jnp.exp(jnp.sum(lq2 * lk2, axis=-1, keepdims=True)) + lam_init)
    o = a0 / l0 - lam * (a1 / l1)
    y = o * lax.rsqrt(jnp.mean(o * o, axis=-1, keepdims=True) + EPS)
    return (y * subln * (1.0 - lam_init)).astype(BF16)


def _attn_prompt_kernel(q_ref, k_ref, v_ref, lq1, lk1, lq2, lk2, sub_ref, o_ref,
                        m_ref, l_ref, acc_ref, *, lam_init):
    i, j = pl.program_id(1), pl.program_id(2)
    bq, bk = q_ref.shape[0], k_ref.shape[0]
    dh = q_ref.shape[1] // 2

    @pl.when(j == 0)
    def _():
        m_ref[...] = jnp.full(m_ref.shape, -jnp.inf, F32)
        l_ref[...] = jnp.zeros(l_ref.shape, F32)
        acc_ref[...] = jnp.zeros(acc_ref.shape, F32)

    def step(mask):
        v = v_ref[...]
        for c in range(2):
            _softmax_step(c, q_ref[:, c * dh:(c + 1) * dh], k_ref[:, c * dh:(c + 1) * dh], v,
                          m_ref, l_ref, acc_ref, mask)

    @pl.when(j < i)
    def _():
        step(None)

    @pl.when(j == i)
    def _():
        qpos = lax.broadcasted_iota(jnp.int32, (bq, bk), 0)
        kpos = lax.broadcasted_iota(jnp.int32, (bq, bk), 1)
        step(kpos < (qpos // CHUNK + 1) * CHUNK)
        o_ref[...] = _diff_finish(acc_ref[0], l_ref[0], acc_ref[1], l_ref[1],
                                  (lq1, lk1, lq2, lk2), sub_ref[...], lam_init)


def _attn_prompt(qb, kb, vb, lams, subln, jl, lam_init, n_rows, nh):
    dv = qb.shape[1] // nh
    bq = _tile(n_rows, 512)
    assert bq % CHUNK == 0
    nq = n_rows // bq
    lam_spec = pl.BlockSpec((None, 1, dv // 2), lambda h, i, j: (jl, 0, 0))
    return pl.pallas_call(
        functools.partial(_attn_prompt_kernel, lam_init=lam_init),
        grid=(nh, nq, nq),
        in_specs=[pl.BlockSpec((bq, dv), lambda h, i, j: (i, h)),
                  pl.BlockSpec((bq, dv), lambda h, i, j: (jnp.minimum(j, i), h)),
                  pl.BlockSpec((bq, dv), lambda h, i, j: (jnp.minimum(j, i), h)),
                  lam_spec, lam_spec, lam_spec, lam_spec,
                  pl.BlockSpec((None, 1, dv), lambda h, i, j: (jl, 0, 0))],
        out_specs=pl.BlockSpec((bq, dv), lambda h, i, j: (i, h)),
        out_shape=jax.ShapeDtypeStruct((n_rows, nh * dv), BF16),
        scratch_shapes=[pltpu.VMEM((2, bq, 1), F32), pltpu.VMEM((2, bq, 1), F32),
                        pltpu.VMEM((2, bq, dv), F32)],
        compiler_params=_params("parallel", "parallel", "arbitrary"),
        name="attn_prompt",
    )(qb, kb, vb, *[_rows3(a) for a in lams], _rows3(subln))


def _attn_sample_kernel(q_ref, ck_ref, cv_ref, kn_ref, vn_ref, lq1, lk1, lq2, lk2, sub_ref, o_ref,
                        m_ref, l_ref, acc_ref, *, lam_init, nh):
    j = pl.program_id(1)
    last = pl.num_programs(1) - 1
    dv = q_ref.shape[1] // nh
    dh = dv // 2

    @pl.when(j == 0)
    def _():
        m_ref[...] = jnp.full(m_ref.shape, -jnp.inf, F32)
        l_ref[...] = jnp.zeros(l_ref.shape, F32)
        acc_ref[...] = jnp.zeros(acc_ref.shape, F32)

    def step(k_ref, v_ref):
        for h in range(nh):
            v = v_ref[:, h * dv:(h + 1) * dv].astype(BF16)
            for c in range(2):
                lo = h * dv + c * dh
                _softmax_step(2 * h + c, q_ref[:, lo:lo + dh], k_ref[:, lo:lo + dh].astype(BF16), v,
                              m_ref, l_ref, acc_ref, None)

    @pl.when(j < last)
    def _():
        step(ck_ref, cv_ref)

    @pl.when(j == last)
    def _():
        step(kn_ref, vn_ref)
        for h in range(nh):
            o_ref[:, h * dv:(h + 1) * dv] = _diff_finish(
                acc_ref[2 * h], l_ref[2 * h], acc_ref[2 * h + 1], l_ref[2 * h + 1],
                (lq1, lk1, lq2, lk2), sub_ref[...], lam_init)


def _attn_sample(qb, kb, vb, cache_k, cache_v, lams, subln, jl, lam_init, n_prompt_rows, seq, nh):
    d = qb.shape[1]
    dv = d // nh
    nb, past = cache_k.shape[1], cache_k.shape[2]
    ck = cache_k.reshape(cache_k.shape[0], nb, past, d)
    cv = cache_v.reshape(cache_v.shape[0], nb, past, d)
    bk = _tile(past, 512)
    nkc = past // bk
    r0 = n_prompt_rows // seq
    new_spec = pl.BlockSpec((seq, d), lambda b, j: (r0 + b, 0))
    cache_spec = pl.BlockSpec((None, None, bk, d), lambda b, j: (jl, b, jnp.minimum(j, nkc - 1), 0))
    lam_spec = pl.BlockSpec((None, 1, dv // 2), lambda b, j: (jl, 0, 0))
    return pl.pallas_call(
        functools.partial(_attn_sample_kernel, lam_init=lam_init, nh=nh),
        grid=(nb, nkc + 1),
        in_specs=[new_spec, cache_spec, cache_spec, new_spec, new_spec,
                  lam_spec, lam_spec, lam_spec, lam_spec,
                  pl.BlockSpec((None, 1, dv), lambda b, j: (jl, 0, 0))],
        out_specs=pl.BlockSpec((seq, d), lambda b, j: (b, 0)),
        out_shape=jax.ShapeDtypeStruct((nb * seq, d), BF16),
        scratch_shapes=[pltpu.VMEM((2 * nh, seq, 1), F32), pltpu.VMEM((2 * nh, seq, 1), F32),
                        pltpu.VMEM((2 * nh, seq, dv), F32)],
        compiler_params=_params("parallel", "arbitrary"),
        name="attn_sample",
    )(qb, ck, cv, kb, vb, *[_rows3(a) for a in lams], _rows3(subln))


def kernel(x_prompt, x_sample, state_hgrn, cache_k, cache_v, state_ffn_conv, norm_mix, norm_ffn, hgrn_lower_bounds, w_hgrn_in, w_hgrn_out, hgrn_out_norm, w_diff_in, w_diff_out, diff_q_norm, diff_k_norm, diff_lambda_q1, diff_lambda_k1, diff_lambda_q2, diff_lambda_k2, diff_subln, w_ffn_up, ffn_conv_w, ffn_conv_b, w_ffn_down):
    bp, seq_p, d = x_prompt.shape
    nb, seq_s, _ = x_sample.shape
    assert bp == 1 and seq_s == CHUNK
    depth = norm_mix.shape[0]
    n_mixers = 2
    n_p = bp * seq_p
    dff = ffn_conv_w.shape[-1]
    diff_heads, dh = cache_k.shape[3], cache_k.shape[5]
    x = jnp.concatenate([x_prompt.reshape(n_p, d), x_sample.reshape(nb * seq_s, d)], axis=0)

    hgrn_p, hgrn_s, kfs, vfs, tails = [], [], [], [], []
    for i in range(depth):
        jl = i // n_mixers
        h = _rms_norm_bf16(x, norm_mix, i)
        if i % n_mixers == 0:
            proj = _matmul(h, w_hgrn_in, jl, name="hgrn_in")
            o, s_p, s_s = _hgrn_scan(proj, hgrn_lower_bounds, hgrn_out_norm, state_hgrn, i, jl, n_p, seq_s)
            hgrn_p.append(s_p[None])
            hgrn_s.append(s_s)
            x = _matmul(o, w_hgrn_out, jl, res=x, name="hgrn_out")
        else:
            lam_init = 0.8 - 0.6 * math.exp(-0.3 * i)
            lams = (diff_lambda_q1, diff_lambda_k1, diff_lambda_q2, diff_lambda_k2)
            qb, kf, kb, vf, vb = _diff_project(h, w_diff_in, jl, diff_q_norm, diff_k_norm, dh)
            o_p = _attn_prompt(qb, kb, vb, lams, diff_subln, jl, lam_init, n_p, diff_heads)
            o_s = _attn_sample(qb, kb, vb, cache_k, cache_v, lams, diff_subln, jl, lam_init,
                               n_p, seq_s, diff_heads)
            kfs.append(kf)
            vfs.append(vf)
            x = _matmul(jnp.concatenate([o_p, o_s], axis=0), w_diff_out, jl, res=x, name="diff_out")
        h = _rms_norm_bf16(x, norm_ffn, i)
        act, tail = _ffn_up(h, w_ffn_up, ffn_conv_w, ffn_conv_b, state_ffn_conv, i, n_p, seq_s)
        tails.append(tail)
        x = _matmul(act, w_ffn_down, i, res=x, bm=512, bn=512, name="ffn_down")

    kf, vf, tail = jnp.stack(kfs), jnp.stack(vfs), jnp.stack(tails)
    seg_p = n_p // seq_s
    return (x[:n_p].reshape(bp, seq_p, d),
            x[n_p:].reshape(nb, seq_s, d),
            jnp.stack(hgrn_p),
            jnp.stack(hgrn_s),
            kf[:, :n_p].reshape(-1, bp, seq_p, diff_heads, 2, dh),
            vf[:, :n_p].reshape(-1, bp, seq_p, diff_heads, 2 * dh),
            kf[:, n_p:].reshape(-1, nb, seq_s, diff_heads, 2, dh),
            vf[:, n_p:].reshape(-1, nb, seq_s, diff_heads, 2 * dh),
            tail[:, seg_p - 1][:, None],
            tail[:, seg_p:])
```

```python
import functools
import math

import jax
import jax.numpy as jnp
from jax import lax
from jax.experimental import pallas as pl
from jax.experimental.pallas import tpu as pltpu

EPS = 1e-6
LOG2E = math.log2(math.e)
CHUNK = 64
HGRN_BLOCK = 16
HGRN_GROUP = 16
CONV_W = 3
LANES = 128
V7X_VMEM_BYTES = 64 * 1024 * 1024
VMEM_LIMIT = V7X_VMEM_BYTES - 8 * 1024 * 1024

F32 = jnp.float32
BF16 = jnp.bfloat16


def _params(*sem):
    return pltpu.CompilerParams(dimension_semantics=sem, vmem_limit_bytes=VMEM_LIMIT)


def _rows3(a):
    return a.reshape(a.shape[0], 1, a.shape[1])


def _tile(n, pref):
    t = min(n, pref)
    while n % t:
        t //= 2
    return t


def _norm_kernel(x_ref, g_ref, o_ref):
    x = x_ref[...]
    y = x * lax.rsqrt(jnp.mean(x * x, axis=-1, keepdims=True) + EPS)
    o_ref[...] = (y * g_ref[...]).astype(o_ref.dtype)


def _rms_norm_bf16(x, gains, layer):
    m, d = x.shape
    bm = _tile(m, 512)
    return pl.pallas_call(
        _norm_kernel,
        grid=(m // bm,),
        in_specs=[pl.BlockSpec((bm, d), lambda i: (i, 0)),
                  pl.BlockSpec((None, 1, d), lambda i: (layer, 0, 0))],
        out_specs=pl.BlockSpec((bm, d), lambda i: (i, 0)),
        out_shape=jax.ShapeDtypeStruct((m, d), BF16),
        compiler_params=_params("parallel"),
        name="rms_norm",
    )(x, _rows3(gains))


def _mm_kernel(h_ref, w_ref, o_ref, wb_ref):
    @pl.when(pl.program_id(1) == 0)
    def _():
        wb_ref[...] = w_ref[...].astype(BF16)

    o_ref[...] = jnp.dot(h_ref[...], wb_ref[...], preferred_element_type=F32)


def _mm_res_kernel(h_ref, w_ref, r_ref, o_ref, wb_ref):
    @pl.when(pl.program_id(1) == 0)
    def _():
        wb_ref[...] = w_ref[...].astype(BF16)

    o_ref[...] = r_ref[...] + jnp.dot(h_ref[...], wb_ref[...], preferred_element_type=F32)


def _matmul(h, w, layer, *, col0=0, ncols=None, res=None, bm=1024, bn=512, name="matmul"):
    m, k = h.shape
    ncols = w.shape[2] - col0 if ncols is None else ncols
    bm, bn = _tile(m, bm), _tile(ncols, bn)
    assert col0 % bn == 0
    c0 = col0 // bn
    in_specs = [pl.BlockSpec((bm, k), lambda j, i: (i, 0)),
                pl.BlockSpec((None, k, bn), lambda j, i: (layer, 0, c0 + j))]
    args = [h, w]
    kern = _mm_kernel
    if res is not None:
        in_specs.append(pl.BlockSpec((bm, bn), lambda j, i: (i, j)))
        args.append(res)
        kern = _mm_res_kernel
    return pl.pallas_call(
        kern,
        grid=(ncols // bn, m // bm),
        in_specs=in_specs,
        out_specs=pl.BlockSpec((bm, bn), lambda j, i: (i, j)),
        out_shape=jax.ShapeDtypeStruct((m, ncols), F32),
        scratch_shapes=[pltpu.VMEM((k, bn), BF16)],
        compiler_params=_params("parallel", "arbitrary"),
        name=name,
    )(*args)


def _headnorm(acc, g, scale):
    outs = []
    for c in range(acc.shape[1] // LANES):
        blk = acc[:, c * LANES:(c + 1) * LANES]
        y = blk * lax.rsqrt(jnp.mean(blk * blk, axis=-1, keepdims=True) + EPS)
        outs.append(y * g * scale if scale != 1.0 else y * g)
    return jnp.concatenate(outs, axis=1) if len(outs) > 1 else outs[0]


def _proj_q_kernel(h_ref, w_ref, g_ref, qb_ref, wb_ref, *, scale):
    @pl.when(pl.program_id(1) == 0)
    def _():
        wb_ref[...] = w_ref[...].astype(BF16)

    acc = jnp.dot(h_ref[...], wb_ref[...], preferred_element_type=F32)
    qb_ref[...] = _headnorm(acc, g_ref[...], scale).astype(BF16)


def _proj_k_kernel(h_ref, w_ref, g_ref, kf_ref, kb_ref, wb_ref):
    @pl.when(pl.program_id(1) == 0)
    def _():
        wb_ref[...] = w_ref[...].astype(BF16)

    acc = jnp.dot(h_ref[...], wb_ref[...], preferred_element_type=F32)
    kn = _headnorm(acc, g_ref[...], 1.0)
    kf_ref[...] = kn
    kb_ref[...] = kn.astype(BF16)


def _proj_v_kernel(h_ref, w_ref, vf_ref, vb_ref, wb_ref):
    @pl.when(pl.program_id(1) == 0)
    def _():
        wb_ref[...] = w_ref[...].astype(BF16)

    acc = jnp.dot(h_ref[...], wb_ref[...], preferred_element_type=F32)
    vf_ref[...] = acc
    vb_ref[...] = acc.astype(BF16)


def _diff_project(h, w, layer, q_g, k_g, dh):
    m, d = h.shape
    bm, bn = _tile(m, 1024), _tile(d, 512)
    nj = d // bn
    grid = (nj, m // bm)
    h_spec = pl.BlockSpec((bm, d), lambda j, i: (i, 0))
    g_spec = pl.BlockSpec((None, 1, dh), lambda j, i: (layer, 0, 0))
    q_g, k_g = _rows3(q_g), _rows3(k_g)
    o_spec = pl.BlockSpec((bm, bn), lambda j, i: (i, j))

    def w_spec(seg):
        return pl.BlockSpec((None, d, bn), lambda j, i: (layer, 0, seg * nj + j))

    scratch = [pltpu.VMEM((d, bn), BF16)]
    cp = _params("parallel", "arbitrary")
    qb = pl.pallas_call(
        functools.partial(_proj_q_kernel, scale=dh ** -0.5 * LOG2E),
        grid=grid, in_specs=[h_spec, w_spec(0), g_spec], out_specs=o_spec,
        out_shape=jax.ShapeDtypeStruct((m, d), BF16),
        scratch_shapes=scratch, compiler_params=cp, name="diff_proj_q",
    )(h, w, q_g)
    kf, kb = pl.pallas_call(
        _proj_k_kernel,
        grid=grid, in_specs=[h_spec, w_spec(1), g_spec], out_specs=[o_spec, o_spec],
        out_shape=[jax.ShapeDtypeStruct((m, d), F32), jax.ShapeDtypeStruct((m, d), BF16)],
        scratch_shapes=scratch, compiler_params=cp, name="diff_proj_k",
    )(h, w, k_g)
    vf, vb = pl.pallas_call(
        _proj_v_kernel,
        grid=grid, in_specs=[h_spec, w_spec(2)], out_specs=[o_spec, o_spec],
        out_shape=[jax.ShapeDtypeStruct((m, d), F32), jax.ShapeDtypeStruct((m, d), BF16)],
        scratch_shapes=scratch, compiler_params=cp, name="diff_proj_v",
    )(h, w)
    return qb, kf, kb, vf, vb


def _ffn_up_kernel(h_ref, wg_ref, wu_ref, cw_ref, cb_ref, st_ref, a_ref, tail_ref,
                   wgb_ref, wub_ref, g_ref, *, n_prompt_tiles, seq):
    i = pl.program_id(1)
    bm, bn = a_ref.shape
    nseg = bm // seq

    @pl.when(i == 0)
    def _():
        wgb_ref[...] = wg_ref[...].astype(BF16)
        wub_ref[...] = wu_ref[...].astype(BF16)
        g_ref[0:8, :] = jnp.zeros((8, bn), F32)

    h = h_ref[...]
    g = jnp.dot(h, wgb_ref[...], preferred_element_type=F32)
    u = jnp.dot(h, wub_ref[...], preferred_element_type=F32)
    g_ref[8:8 + bm, :] = g
    g1 = g_ref[7:7 + bm, :]
    g2 = g_ref[6:6 + bm, :]
    cw = cw_ref[...]
    base = cb_ref[...] + cw[2:3, :] * g

    def finish(g1, g2):
        conv = base + cw[1:2, :] * g1 + cw[0:1, :] * g2
        a_ref[...] = (conv * jax.nn.sigmoid(conv) * u).astype(BF16)

    @pl.when(i < n_prompt_tiles)
    def _():
        finish(g1, g2)

    @pl.when(i >= n_prompt_tiles)
    def _():
        st = st_ref[...]
        p2 = jnp.broadcast_to(st[:, 0:1, :], (nseg, seq, bn)).reshape(bm, bn)
        p1 = jnp.broadcast_to(st[:, 1:2, :], (nseg, seq, bn)).reshape(bm, bn)
        pos = lax.broadcasted_iota(jnp.int32, (bm, bn), 0) % seq
        finish(jnp.where(pos == 0, p1, g1),
               jnp.where(pos == 0, p2, jnp.where(pos == 1, p1, g2)))

    for n in range(nseg):
        end = 8 + (n + 1) * seq
        tail_ref[n] = g_ref[end - (CONV_W - 1):end, :]
    g_ref[0:8, :] = g_ref[bm:bm + 8, :]


def _ffn_up(h, w_up, conv_w, conv_b, conv_state, layer, n_prompt_rows, seq):
    m, d = h.shape
    dff = conv_w.shape[-1]
    nb = conv_state.shape[1]
    bm = _tile(math.gcd(n_prompt_rows, nb * seq), 1024)
    bn = _tile(dff, 512)
    assert bm % seq == 0 and (m - n_prompt_rows) == nb * seq
    nseg = bm // seq
    npt = n_prompt_rows // bm
    nj = dff // bn
    return pl.pallas_call(
        functools.partial(_ffn_up_kernel, n_prompt_tiles=npt, seq=seq),
        grid=(nj, m // bm),
        in_specs=[pl.BlockSpec((bm, d), lambda j, i: (i, 0)),
                  pl.BlockSpec((None, d, bn), lambda j, i: (layer, 0, j)),
                  pl.BlockSpec((None, d, bn), lambda j, i: (layer, 0, nj + j)),
                  pl.BlockSpec((None, CONV_W, bn), lambda j, i: (layer, 0, j)),
                  pl.BlockSpec((None, 1, bn), lambda j, i: (layer, 0, j)),
                  pl.BlockSpec((None, nseg, CONV_W - 1, bn),
                               lambda j, i: (layer, jnp.maximum(i - npt, 0), 0, j))],
        out_specs=[pl.BlockSpec((bm, bn), lambda j, i: (i, j)),
                   pl.BlockSpec((nseg, CONV_W - 1, bn), lambda j, i: (i, 0, j))],
        out_shape=[jax.ShapeDtypeStruct((m, dff), BF16),
                   jax.ShapeDtypeStruct((m // seq, CONV_W - 1, dff), F32)],
        scratch_shapes=[pltpu.VMEM((d, bn), BF16), pltpu.VMEM((d, bn), BF16),
                        pltpu.VMEM((bm + 8, bn), F32)],
        compiler_params=_params("parallel", "arbitrary"),
        name="ffn_up",
    )(h, w_up, w_up, conv_w, _rows3(conv_b), conv_state)


def _hgrn_kernel(q_ref, fz_ref, v_ref, gate_ref, lbp_ref, gout_ref, s0_ref, o_ref, sout_ref,
                 st_ref, b_ref, k_ref, qe_ref, kd_ref, oacc_ref, *, layer, seq_blocks):
    c = pl.program_id(1)
    t_rows = q_ref.shape[0]
    nblk = t_rows // HGRN_BLOCK
    carry = seq_blocks is None

    if carry:
        @pl.when(c == 0)
        def _():
            st_ref[...] = jnp.zeros_like(st_ref)

    lbp = lbp_ref[...]
    e = jnp.exp(lbp - jnp.max(lbp, axis=0, keepdims=True))
    lb = jnp.sum(e[0:layer + 1], axis=0, keepdims=True) / jnp.sum(e, axis=0, keepdims=True)

    f = lb + (1.0 - lb) * jax.nn.sigmoid(fz_ref[...])
    logf = jnp.log(f)
    kk = 1.0 - f
    pos = lax.broadcasted_iota(jnp.int32, (t_rows, LANES), 0) % HGRN_BLOCK
    b = logf
    suf = logf
    sh = 1
    while sh < HGRN_BLOCK:
        b = b + jnp.where(pos >= sh, pltpu.roll(b, sh, 0), 0.0)
        suf = suf + jnp.where(pos < HGRN_BLOCK - sh, pltpu.roll(suf, t_rows - sh, 0), 0.0)
        sh *= 2
    b_ref[...] = b * LOG2E
    k_ref[...] = kk
    qe_ref[...] = (q_ref[...] * jnp.exp(b)).astype(BF16)
    kd_ref[...] = (kk * jnp.exp(suf - logf)).astype(BF16)

    half = HGRN_BLOCK // 2
    row = lax.broadcasted_iota(jnp.int32, (half, LANES), 0)
    lane = lax.broadcasted_iota(jnp.int32, (half, LANES), 1)

    def scores(r0):
        lo, hi = pl.ds(r0, half), pl.ds(r0 + half, half)
        b_lo, b_hi, q_lo, q_hi = b_ref[lo, :], b_ref[hi, :], q_ref[lo, :], q_ref[hi, :]
        sc_lo = jnp.zeros((half, LANES), F32)
        sc_hi = jnp.zeros((half, LANES), F32)
        for s in range(HGRN_BLOCK):
            bs, ks = b_ref[pl.ds(r0 + s, 1), :], k_ref[pl.ds(r0 + s, 1), :]
            col_hi = jnp.sum(jnp.exp2(b_hi - bs) * (q_hi * ks), axis=-1, keepdims=True)
            sc_hi = jnp.where(lane == s, col_hi, sc_hi)
            if s < half:
                col_lo = jnp.sum(jnp.exp2(b_lo - bs) * (q_lo * ks), axis=-1, keepdims=True)
                sc_lo = jnp.where(lane == s, col_lo, sc_lo)
        sc = jnp.concatenate([jnp.where(row >= lane, sc_lo, 0.0),
                              jnp.where(row + half >= lane, sc_hi, 0.0)], axis=0)
        return sc[:, 0:HGRN_BLOCK].astype(BF16)

    group = min(nblk, HGRN_GROUP) if carry else nblk

    def blocks(jg, st):
        rows, sc, vb, decay, upd = [], [], [], [], []
        for g in range(group):
            r0 = pl.multiple_of((jg * group + g) * HGRN_BLOCK, HGRN_BLOCK)
            rows.append(pl.ds(r0, HGRN_BLOCK))
            vb.append(v_ref[rows[g], :].astype(BF16))
            sc.append(scores(r0))
            decay.append(jnp.exp2(b_ref[pl.ds(r0 + HGRN_BLOCK - 1, 1), :]))
            upd.append(lax.dot_general(vb[g], kd_ref[rows[g], :], (((0,), (0,)), ((), ())),
                                       preferred_element_type=F32))
        states = []
        for g in range(group):
            if not carry and g % seq_blocks == 0:
                st = s0_ref[g // seq_blocks].T
            states.append(st.astype(BF16))
            st = st * decay[g] + upd[g]
            if not carry and (g + 1) % seq_blocks == 0:
                sout_ref[g // seq_blocks] = st.T
        for g in range(group):
            o = lax.dot_general(qe_ref[rows[g], :], states[g], (((1,), (1,)), ((), ())),
                                preferred_element_type=F32)
            oacc_ref[rows[g], :] = o + jnp.dot(sc[g], vb[g], preferred_element_type=F32)
        return st

    if carry:
        st_ref[...] = lax.fori_loop(0, nblk // group, blocks, st_ref[...])
    else:
        blocks(0, None)

    o = oacc_ref[...]
    y = o * lax.rsqrt(jnp.mean(o * o, axis=-1, keepdims=True) + EPS) * gout_ref[...]
    gate = gate_ref[...]
    o_ref[...] = (y * (gate * jax.nn.sigmoid(gate))).astype(BF16)

    if carry:
        @pl.when(c == pl.num_programs(1) - 1)
        def _():
            sout_ref[...] = st_ref[...].T


def _hgrn_scan(proj, lower_bounds, out_norm, state, layer, jl, n_prompt_rows, seq):
    nb, nh, dk, dv = state.shape[1:]
    assert dk == LANES and dv == LANES
    scratch = lambda t: [pltpu.VMEM((dv, dk), F32), pltpu.VMEM((t, dk), F32), pltpu.VMEM((t, dk), F32),
                         pltpu.VMEM((t, dk), BF16), pltpu.VMEM((t, dk), BF16), pltpu.VMEM((t, dv), F32)]

    def specs(t, row_of):
        seg = lambda k: pl.BlockSpec((t, LANES), lambda h, c: (row_of(c), k * nh + h))
        return [seg(0), seg(1), seg(2), seg(3),
                pl.BlockSpec((lower_bounds.shape[0], LANES), lambda h, c: (0, h)),
                pl.BlockSpec((None, 1, dv), lambda h, c: (jl, 0, 0))]

    out_norm = _rows3(out_norm)
    tp = _tile(n_prompt_rows, 512)
    o_p, s_p = pl.pallas_call(
        functools.partial(_hgrn_kernel, layer=layer, seq_blocks=None),
        grid=(nh, n_prompt_rows // tp),
        in_specs=specs(tp, lambda c: c) + [pl.BlockSpec((None, None, None, dk, dv),
                                                        lambda h, c: (jl, 0, h, 0, 0))],
        out_specs=[pl.BlockSpec((tp, dv), lambda h, c: (c, h)),
                   pl.BlockSpec((None, dk, dv), lambda h, c: (h, 0, 0))],
        out_shape=[jax.ShapeDtypeStruct((n_prompt_rows, nh * dv), BF16),
                   jax.ShapeDtypeStruct((nh, dk, dv), F32)],
        scratch_shapes=scratch(tp),
        compiler_params=_params("parallel", "arbitrary"),
        name="hgrn_prompt",
    )(proj, proj, proj, proj, lower_bounds, out_norm, state)
    ns = _tile(nb, max(1, HGRN_GROUP * HGRN_BLOCK // seq))
    ts = ns * seq
    assert n_prompt_rows % ts == 0 and seq % HGRN_BLOCK == 0
    r0 = n_prompt_rows // ts
    o_s, s_s = pl.pallas_call(
        functools.partial(_hgrn_kernel, layer=layer, seq_blocks=seq // HGRN_BLOCK),
        grid=(nh, nb // ns),
        in_specs=specs(ts, lambda c: r0 + c) + [pl.BlockSpec((None, ns, None, dk, dv),
                                                              lambda h, c: (jl, c, h, 0, 0))],
        out_specs=[pl.BlockSpec((ts, dv), lambda h, c: (c, h)),
                   pl.BlockSpec((ns, None, dk, dv), lambda h, c: (c, h, 0, 0))],
        out_shape=[jax.ShapeDtypeStruct((nb * seq, nh * dv), BF16),
                   jax.ShapeDtypeStruct((nb, nh, dk, dv), F32)],
        scratch_shapes=scratch(ts),
        compiler_params=_params("parallel", "arbitrary"),
        name="hgrn_sample",
    )(proj, proj, proj, proj, lower_bounds, out_norm, state)
    return jnp.concatenate([o_p, o_s], axis=0), s_p, s_s


def _lanes(x, n):
    return x[:, :n] if n <= LANES else jnp.concatenate([x] * (n // LANES), axis=1)


def _softmax_step(c, q, k, v, m_ref, l_ref, acc_ref, mask):
    s = lax.dot_general(q, k, (((1,), (1,)), ((), ())), preferred_element_type=F32)
    if mask is not None:
        s = jnp.where(mask, s, -jnp.inf)
    m_prev = m_ref[c]
    m_new = jnp.maximum(m_prev, jnp.max(s, axis=-1, keepdims=True))
    alpha = jnp.exp2(m_prev - m_new)
    p = jnp.exp2(s - _lanes(m_new, s.shape[1]))
    l_ref[c] = alpha * l_ref[c] + jnp.sum(p, axis=-1, keepdims=True)
    acc_ref[c] = (_lanes(alpha, v.shape[1]) * acc_ref[c]
                  + jnp.dot(p.astype(BF16), v, preferred_element_type=F32))
    m_ref[c] = m_new


def _diff_finish(a0, l0, a1, l1, lam_refs, subln, lam_init):
    lq1, lk1, lq2, lk2 = [r[...] for r in lam_refs]
    lam = (jnp.exp(jnp.sum(lq1 * lk1, axis=-1, keepdims=True))
           - jnp.exp(jnp.sum(lq2 * lk2, axis=-1, keepdims=True)) + lam_init)
    o = a0 * _lanes(1.0 / l0, a0.shape[1]) - lam * (a1 * _lanes(1.0 / l1, a1.shape[1]))
    y = o * lax.rsqrt(jnp.mean(o * o, axis=-1, keepdims=True) + EPS)
    return (y * subln * (1.0 - lam_init)).astype(BF16)


def _attn_prompt_kernel(it_ref, jt_ref, q_ref, k_ref, v_ref, lq1, lk1, lq2, lk2, sub_ref, o_ref,
                        m_ref, l_ref, acc_ref, *, lam_init):
    p = pl.program_id(1)
    i, j = it_ref[p], jt_ref[p]
    bq, bk = q_ref.shape[0], k_ref.shape[0]
    dh = q_ref.shape[1] // 2

    @pl.when(j == 0)
    def _():
        m_ref[...] = jnp.full(m_ref.shape, -jnp.inf, F32)
        l_ref[...] = jnp.zeros(l_ref.shape, F32)
        acc_ref[...] = jnp.zeros(acc_ref.shape, F32)

    def step(mask):
        v = v_ref[...]
        for c in range(2):
            _softmax_step(c, q_ref[:, c * dh:(c + 1) * dh], k_ref[:, c * dh:(c + 1) * dh], v,
                          m_ref, l_ref, acc_ref, mask)

    @pl.when(j < i)
    def _():
        step(None)

    @pl.when(j == i)
    def _():
        qpos = lax.broadcasted_iota(jnp.int32, (bq, bk), 0)
        kpos = lax.broadcasted_iota(jnp.int32, (bq, bk), 1)
        step(kpos < (qpos // CHUNK + 1) * CHUNK)
        o_ref[...] = _diff_finish(acc_ref[0], l_ref[0], acc_ref[1], l_ref[1],
                                  (lq1, lk1, lq2, lk2), sub_ref[...], lam_init)


def _attn_prompt(qb, kb, vb, lams, subln, jl, lam_init, n_rows, nh):
    dv = qb.shape[1] // nh
    bq = _tile(n_rows, 512)
    assert bq % CHUNK == 0
    nq = n_rows // bq
    pairs = [(i, j) for i in range(nq) for j in range(i + 1)]
    it = jnp.asarray([p[0] for p in pairs], jnp.int32)
    jt = jnp.asarray([p[1] for p in pairs], jnp.int32)
    lam_spec = pl.BlockSpec((None, 1, dv // 2), lambda h, p, it, jt: (jl, 0, 0))
    return pl.pallas_call(
        functools.partial(_attn_prompt_kernel, lam_init=lam_init),
        grid_spec=pltpu.PrefetchScalarGridSpec(
            num_scalar_prefetch=2,
            grid=(nh, len(pairs)),
            in_specs=[pl.BlockSpec((bq, dv), lambda h, p, it, jt: (it[p], h)),
                      pl.BlockSpec((bq, dv), lambda h, p, it, jt: (jt[p], h)),
                      pl.BlockSpec((bq, dv), lambda h, p, it, jt: (jt[p], h)),
                      lam_spec, lam_spec, lam_spec, lam_spec,
                      pl.BlockSpec((None, 1, dv), lambda h, p, it, jt: (jl, 0, 0))],
            out_specs=pl.BlockSpec((bq, dv), lambda h, p, it, jt: (it[p], h)),
            scratch_shapes=[pltpu.VMEM((2, bq, LANES), F32), pltpu.VMEM((2, bq, LANES), F32),
                            pltpu.VMEM((2, bq, dv), F32)]),
        out_shape=jax.ShapeDtypeStruct((n_rows, nh * dv), BF16),
        compiler_params=_params("parallel", "arbitrary"),
        name="attn_prompt",
    )(it, jt, qb, kb, vb, *[_rows3(a) for a in lams], _rows3(subln))


def _attn_sample_kernel(q_ref, ck_ref, cv_ref, kn_ref, vn_ref, lq1, lk1, lq2, lk2, sub_ref, o_ref,
                        m_ref, l_ref, acc_ref, *, lam_init, nh):
    j = pl.program_id(1)
    last = pl.num_programs(1) - 1
    dv = q_ref.shape[1] // nh
    dh = dv // 2

    @pl.when(j == 0)
    def _():
        m_ref[...] = jnp.full(m_ref.shape, -jnp.inf, F32)
        l_ref[...] = jnp.zeros(l_ref.shape, F32)
        acc_ref[...] = jnp.zeros(acc_ref.shape, F32)

    def step(k_of, v_of):
        for h in range(nh):
            v = v_of(h)
            for c in range(2):
                lo = h * dv + c * dh
                _softmax_step(2 * h + c, q_ref[:, lo:lo + dh], k_of(h, c), v, m_ref, l_ref, acc_ref, None)

    @pl.when(j < last)
    def _():
        bk = ck_ref.shape[0] // (2 * nh)

        def rows(ref, first):
            return ref[pl.ds(first, bk, stride=2 * nh), :].astype(BF16)

        step(lambda h, c: rows(ck_ref, 2 * h + c),
             lambda h: jnp.concatenate([rows(cv_ref, h), rows(cv_ref, nh + h)], axis=1))

    @pl.when(j == last)
    def _():
        step(lambda h, c: kn_ref[:, h * dv + c * dh:h * dv + (c + 1) * dh],
             lambda h: vn_ref[:, h * dv:(h + 1) * dv])
        for h in range(nh):
            o_ref[:, h * dv:(h + 1) * dv] = _diff_finish(
                acc_ref[2 * h], l_ref[2 * h], acc_ref[2 * h + 1], l_ref[2 * h + 1],
                (lq1, lk1, lq2, lk2), sub_ref[...], lam_init)


def _attn_sample(qb, kb, vb, cache_k, cache_v, lams, subln, jl, lam_init, n_prompt_rows, seq, nh):
    d = qb.shape[1]
    dv = d // nh
    nb, past = cache_k.shape[1], cache_k.shape[2]
    bk = _tile(past, 512)
    nkc = past // bk
    r0 = n_prompt_rows // seq
    new_spec = pl.BlockSpec((seq, d), lambda b, j: (r0 + b, 0))
    nl = cache_k.shape[0]
    ck = cache_k.reshape(nl, nb, past * nh * 2, dv // 2)
    cv = cache_v.reshape(nl, nb, past, nh, 2, dv // 2).transpose(0, 1, 2, 4, 3, 5).reshape(ck.shape)
    cache_spec = pl.BlockSpec((None, None, bk * nh * 2, dv // 2),
                              lambda b, j: (jl, b, jnp.minimum(j, nkc - 1), 0))
    lam_spec = pl.BlockSpec((None, 1, dv // 2), lambda b, j: (jl, 0, 0))
    return pl.pallas_call(
        functools.partial(_attn_sample_kernel, lam_init=lam_init, nh=nh),
        grid=(nb, nkc + 1),
        in_specs=[new_spec, cache_spec, cache_spec, new_spec, new_spec,
                  lam_spec, lam_spec, lam_spec, lam_spec,
                  pl.BlockSpec((None, 1, dv), lambda b, j: (jl, 0, 0))],
        out_specs=pl.BlockSpec((seq, d), lambda b, j: (b, 0)),
        out_shape=jax.ShapeDtypeStruct((nb * seq, d), BF16),
        scratch_shapes=[pltpu.VMEM((2 * nh, seq, LANES), F32), pltpu.VMEM((2 * nh, seq, LANES), F32),
                        pltpu.VMEM((2 * nh, seq, dv), F32)],
        compiler_params=_params("parallel", "arbitrary"),
        name="attn_sample",
    )(qb, ck, cv, kb, vb, *[_rows3(a) for a in lams], _rows3(subln))


def kernel(x_prompt, x_sample, state_hgrn, cache_k, cache_v, state_ffn_conv, norm_mix, norm_ffn, hgrn_lower_bounds, w_hgrn_in, w_hgrn_out, hgrn_out_norm, w_diff_in, w_diff_out, diff_q_norm, diff_k_norm, diff_lambda_q1, diff_lambda_k1, diff_lambda_q2, diff_lambda_k2, diff_subln, w_ffn_up, ffn_conv_w, ffn_conv_b, w_ffn_down):
    bp, seq_p, d = x_prompt.shape
    nb, seq_s, _ = x_sample.shape
    assert bp == 1 and seq_s == CHUNK
    depth = norm_mix.shape[0]
    n_mixers = 2
    n_p = bp * seq_p
    diff_heads, dh = cache_k.shape[3], cache_k.shape[5]
    x = jnp.concatenate([x_prompt.reshape(n_p, d), x_sample.reshape(nb * seq_s, d)], axis=0)

    hgrn_p, hgrn_s, kfs, vfs, tails = [], [], [], [], []
    for i in range(depth):
        jl = i // n_mixers
        h = _rms_norm_bf16(x, norm_mix, i)
        if i % n_mixers == 0:
            proj = _matmul(h, w_hgrn_in, jl, name="hgrn_in")
            o, s_p, s_s = _hgrn_scan(proj, hgrn_lower_bounds, hgrn_out_norm, state_hgrn, i, jl, n_p, seq_s)
            hgrn_p.append(s_p[None])
            hgrn_s.append(s_s)
            x = _matmul(o, w_hgrn_out, jl, res=x, name="hgrn_out")
        else:
            lam_init = 0.8 - 0.6 * math.exp(-0.3 * i)
            lams = (diff_lambda_q1, diff_lambda_k1, diff_lambda_q2, diff_lambda_k2)
            qb, kf, kb, vf, vb = _diff_project(h, w_diff_in, jl, diff_q_norm, diff_k_norm, dh)
            o_p = _attn_prompt(qb, kb, vb, lams, diff_subln, jl, lam_init, n_p, diff_heads)
            o_s = _attn_sample(qb, kb, vb, cache_k, cache_v, lams, diff_subln, jl, lam_init,
                               n_p, seq_s, diff_heads)
            kfs.append(kf)
            vfs.append(vf)
            x = _matmul(jnp.concatenate([o_p, o_s], axis=0), w_diff_out, jl, res=x, name="diff_out")
        h = _rms_norm_bf16(x, norm_ffn, i)
        act, tail = _ffn_up(h, w_ffn_up, ffn_conv_w, ffn_conv_b, state_ffn_conv, i, n_p, seq_s)
        tails.append(tail)
        x = _matmul(act, w_ffn_down, i, res=x, bm=512, bn=512, name="ffn_down")

    kf, vf, tail = jnp.stack(kfs), jnp.stack(vfs), jnp.stack(tails)
    seg_p = n_p // seq_s
    return (x[:n_p].reshape(bp, seq_p, d),
            x[n_p:].reshape(nb, seq_s, d),
            jnp.stack(hgrn_p),
            jnp.stack(hgrn_s),
            kf[:, :n_p].reshape(-1, bp, seq_p, diff_heads, 2, dh),
            vf[:, :n_p].reshape(-1, bp, seq_p, diff_heads, 2 * dh),
            kf[:, n_p:].reshape(-1, nb, seq_s, diff_heads, 2, dh),
            vf[:, n_p:].reshape(-1, nb, seq_s, diff_heads, 2 * dh),
            tail[:, seg_p - 1][:, None],
            tail[:, seg_p:])
```

```python
import functools
import math

import jax
import jax.numpy as jnp
from jax import lax
from jax.experimental import pallas as pl
from jax.experimental.pallas import tpu as pltpu

EPS = 1e-6
LOG2E = math.log2(math.e)
CHUNK = 64
HGRN_BLOCK = 16
ATTN_BQ, ATTN_BK = 512, 1024
HGRN_GROUP = 16
CONV_W = 3
LANES = 128
V7X_VMEM_BYTES = 64 * 1024 * 1024
VMEM_LIMIT = V7X_VMEM_BYTES - 8 * 1024 * 1024

F32 = jnp.float32
BF16 = jnp.bfloat16


def _params(*sem):
    return pltpu.CompilerParams(dimension_semantics=sem, vmem_limit_bytes=VMEM_LIMIT)


def _rows3(a):
    return a.reshape(a.shape[0], 1, a.shape[1])


def _tile(n, pref):
    t = min(n, pref)
    while n % t:
        t //= 2
    return t


def _norm_kernel(x_ref, g_ref, o_ref):
    x = x_ref[...]
    y = x * lax.rsqrt(jnp.mean(x * x, axis=-1, keepdims=True) + EPS)
    o_ref[...] = (y * g_ref[...]).astype(o_ref.dtype)


def _rms_norm_bf16(x, gains, layer):
    m, d = x.shape
    bm = _tile(m, 512)
    return pl.pallas_call(
        _norm_kernel,
        grid=(m // bm,),
        in_specs=[pl.BlockSpec((bm, d), lambda i: (i, 0)),
                  pl.BlockSpec((None, 1, d), lambda i: (layer, 0, 0))],
        out_specs=pl.BlockSpec((bm, d), lambda i: (i, 0)),
        out_shape=jax.ShapeDtypeStruct((m, d), BF16),
        compiler_params=_params("parallel"),
        name="rms_norm",
    )(x, _rows3(gains))


def _split_specs(n_p, n_s, bm, bn, col):
    npt, nst = n_p // bm, n_s // bm
    return [pl.BlockSpec((bm, bn), lambda j, i: (jnp.minimum(i, npt - 1), col(j))),
            pl.BlockSpec((bm, bn), lambda j, i: (jnp.clip(i - npt, 0, nst - 1), col(j)))]


def _on_rows(i, npt, split, fn, *ref_pairs):
    if not split:
        fn(*[p[0] for p in ref_pairs])
        return
    pl.when(i < npt)(lambda: fn(*[p[0] for p in ref_pairs]))
    pl.when(i >= npt)(lambda: fn(*[p[-1] for p in ref_pairs]))


def _mm_kernel(*refs, has_res, split_in, split_out, npt):
    refs = list(refs)
    h_refs = [refs.pop(0) for _ in range(2 if split_in else 1)]
    w_ref = refs.pop(0)
    r_ref = refs.pop(0) if has_res else None
    o_refs = [refs.pop(0) for _ in range(2 if split_out else 1)]
    wb_ref, = refs
    i = pl.program_id(1)

    @pl.when(i == 0)
    def _():
        wb_ref[...] = w_ref[...].astype(BF16)

    def run(h_ref, o_ref):
        acc = jnp.dot(h_ref[...], wb_ref[...], preferred_element_type=F32)
        o_ref[...] = r_ref[...] + acc if has_res else acc

    _on_rows(i, npt, split_in or split_out, run, h_refs, o_refs)


def _matmul(h, w, layer, n_p, *, res=None, split_out=False, bm=1024, bn=512, w_buffers=2, name="matmul"):
    split_in = isinstance(h, tuple)
    k, ncols = w.shape[1], w.shape[2]
    m = sum(a.shape[0] for a in h) if split_in else h.shape[0]
    n_s = m - n_p
    bm, bn = _tile(math.gcd(n_p, n_s), bm), _tile(ncols, bn)
    npt = n_p // bm
    in_specs = (_split_specs(n_p, n_s, bm, k, lambda j: 0) if split_in
                else [pl.BlockSpec((bm, k), lambda j, i: (i, 0))])
    in_specs.append(pl.BlockSpec((None, k, bn), lambda j, i: (layer, 0, j),
                                 pipeline_mode=pl.Buffered(w_buffers)))
    args = list(h) if split_in else [h]
    args.append(w)
    if res is not None:
        in_specs.append(pl.BlockSpec((bm, bn), lambda j, i: (i, j)))
        args.append(res)
    if split_out:
        out_specs = _split_specs(n_p, n_s, bm, bn, lambda j: j)
        out_shape = [jax.ShapeDtypeStruct((n_p, ncols), F32), jax.ShapeDtypeStruct((n_s, ncols), F32)]
    else:
        out_specs = pl.BlockSpec((bm, bn), lambda j, i: (i, j))
        out_shape = jax.ShapeDtypeStruct((m, ncols), F32)
    return pl.pallas_call(
        functools.partial(_mm_kernel, has_res=res is not None, split_in=split_in, split_out=split_out,
                          npt=npt),
        grid=(ncols // bn, m // bm),
        in_specs=in_specs,
        out_specs=out_specs,
        out_shape=out_shape,
        scratch_shapes=[pltpu.VMEM((k, bn), BF16)],
        compiler_params=_params("parallel", "arbitrary"),
        name=name,
    )(*args)


def _headnorm(acc, g, scale):
    outs = []
    for c in range(acc.shape[1] // LANES):
        blk = acc[:, c * LANES:(c + 1) * LANES]
        y = blk * lax.rsqrt(jnp.mean(blk * blk, axis=-1, keepdims=True) + EPS)
        outs.append(y * g * scale if scale != 1.0 else y * g)
    return jnp.concatenate(outs, axis=1) if len(outs) > 1 else outs[0]


def _proj_q_kernel(h_ref, w_ref, g_ref, qb_ref, wb_ref, *, scale):
    @pl.when(pl.program_id(1) == 0)
    def _():
        wb_ref[...] = w_ref[...].astype(BF16)

    acc = jnp.dot(h_ref[...], wb_ref[...], preferred_element_type=F32)
    qb_ref[...] = _headnorm(acc, g_ref[...], scale).astype(BF16)


def _proj_k_kernel(h_ref, w_ref, g_ref, kfp_ref, kfs_ref, kb_ref, wb_ref, *, npt):
    i = pl.program_id(1)

    @pl.when(i == 0)
    def _():
        wb_ref[...] = w_ref[...].astype(BF16)

    acc = jnp.dot(h_ref[...], wb_ref[...], preferred_element_type=F32)
    kn = _headnorm(acc, g_ref[...], 1.0)
    kb_ref[...] = kn.astype(BF16)

    def put(kf_ref):
        kf_ref[...] = kn

    _on_rows(i, npt, True, put, (kfp_ref, kfs_ref))


def _proj_v_kernel(h_ref, w_ref, vfp_ref, vfs_ref, vb_ref, wb_ref, *, npt):
    i = pl.program_id(1)

    @pl.when(i == 0)
    def _():
        wb_ref[...] = w_ref[...].astype(BF16)

    acc = jnp.dot(h_ref[...], wb_ref[...], preferred_element_type=F32)
    vb_ref[...] = acc.astype(BF16)

    def put(vf_ref):
        vf_ref[...] = acc

    _on_rows(i, npt, True, put, (vfp_ref, vfs_ref))


def _diff_project(h, w, layer, q_g, k_g, dh, n_p):
    m, d = h.shape
    n_s = m - n_p
    bm, bn = _tile(math.gcd(n_p, n_s), 1024), _tile(d, 512)
    f_specs = _split_specs(n_p, n_s, bm, bn, lambda j: j)
    f_shapes = [jax.ShapeDtypeStruct((n_p, d), F32), jax.ShapeDtypeStruct((n_s, d), F32)]
    nj = d // bn
    grid = (nj, m // bm)
    h_spec = pl.BlockSpec((bm, d), lambda j, i: (i, 0))
    g_spec = pl.BlockSpec((None, 1, dh), lambda j, i: (layer, 0, 0))
    q_g, k_g = _rows3(q_g), _rows3(k_g)
    o_spec = pl.BlockSpec((bm, bn), lambda j, i: (i, j))

    def w_spec(seg):
        return pl.BlockSpec((None, d, bn), lambda j, i: (layer, 0, seg * nj + j))

    scratch = [pltpu.VMEM((d, bn), BF16)]
    cp = _params("parallel", "arbitrary")
    qb = pl.pallas_call(
        functools.partial(_proj_q_kernel, scale=dh ** -0.5 * LOG2E),
        grid=grid, in_specs=[h_spec, w_spec(0), g_spec], out_specs=o_spec,
        out_shape=jax.ShapeDtypeStruct((m, d), BF16),
        scratch_shapes=scratch, compiler_params=cp, name="diff_proj_q",
    )(h, w, q_g)
    kf_p, kf_s, kb = pl.pallas_call(
        functools.partial(_proj_k_kernel, npt=n_p // bm),
        grid=grid, in_specs=[h_spec, w_spec(1), g_spec], out_specs=f_specs + [o_spec],
        out_shape=f_shapes + [jax.ShapeDtypeStruct((m, d), BF16)],
        scratch_shapes=scratch, compiler_params=cp, name="diff_proj_k",
    )(h, w, k_g)
    vf_p, vf_s, vb = pl.pallas_call(
        functools.partial(_proj_v_kernel, npt=n_p // bm),
        grid=grid, in_specs=[h_spec, w_spec(2)], out_specs=f_specs + [o_spec],
        out_shape=f_shapes + [jax.ShapeDtypeStruct((m, d), BF16)],
        scratch_shapes=scratch, compiler_params=cp, name="diff_proj_v",
    )(h, w)
    return qb, (kf_p, kf_s), kb, (vf_p, vf_s), vb


def _ffn_up_kernel(h_ref, wg_ref, wu_ref, cw_ref, cb_ref, st_ref, a_ref, tail_ref,
                   wgb_ref, wub_ref, g_ref, *, n_prompt_tiles, seq):
    i = pl.program_id(1)
    bm, bn = a_ref.shape
    nseg = bm // seq

    @pl.when(i == 0)
    def _():
        wgb_ref[...] = wg_ref[...].astype(BF16)
        wub_ref[...] = wu_ref[...].astype(BF16)
        g_ref[0:8, :] = jnp.zeros((8, bn), F32)

    h = h_ref[...]
    g = jnp.dot(h, wgb_ref[...], preferred_element_type=F32)
    u = jnp.dot(h, wub_ref[...], preferred_element_type=F32)
    g_ref[8:8 + bm, :] = g
    g1 = g_ref[7:7 + bm, :]
    g2 = g_ref[6:6 + bm, :]
    st = st_ref[...]
    p2 = jnp.broadcast_to(st[:, 0:1, :], (nseg, seq, bn)).reshape(bm, bn)
    p1 = jnp.broadcast_to(st[:, 1:2, :], (nseg, seq, bn)).reshape(bm, bn)
    pos = (lax.broadcasted_iota(jnp.int32, (bm, bn), 0) % seq
           + jnp.where(i >= n_prompt_tiles, 0, seq))
    g1 = jnp.where(pos == 0, p1, g1)
    g2 = jnp.where(pos == 0, p2, jnp.where(pos == 1, p1, g2))
    cw = cw_ref[...]
    conv = cb_ref[...] + cw[2:3, :] * g + cw[1:2, :] * g1 + cw[0:1, :] * g2
    a_ref[...] = (conv * jax.nn.sigmoid(conv) * u).astype(BF16)

    for n in range(nseg):
        end = 8 + (n + 1) * seq
        tail_ref[n] = g_ref[end - (CONV_W - 1):end, :]
    g_ref[0:8, :] = g_ref[bm:bm + 8, :]


def _ffn_up(h, w_up, conv_w, conv_b, conv_state, layer, n_prompt_rows, seq):
    m, d = h.shape
    dff = conv_w.shape[-1]
    nb = conv_state.shape[1]
    bm = _tile(math.gcd(n_prompt_rows, nb * seq), 1024)
    bn = _tile(dff, 512)
    assert bm % seq == 0 and (m - n_prompt_rows) == nb * seq
    nseg = bm // seq
    npt = n_prompt_rows // bm
    nj = dff // bn
    return pl.pallas_call(
        functools.partial(_ffn_up_kernel, n_prompt_tiles=npt, seq=seq),
        grid=(nj, m // bm),
        in_specs=[pl.BlockSpec((bm, d), lambda j, i: (i, 0)),
                  pl.BlockSpec((None, d, bn), lambda j, i: (layer, 0, j)),
                  pl.BlockSpec((None, d, bn), lambda j, i: (layer, 0, nj + j)),
                  pl.BlockSpec((None, CONV_W, bn), lambda j, i: (layer, 0, j)),
                  pl.BlockSpec((None, 1, bn), lambda j, i: (layer, 0, j)),
                  pl.BlockSpec((None, nseg, CONV_W - 1, bn),
                               lambda j, i: (layer, jnp.maximum(i - npt, 0), 0, j))],
        out_specs=[pl.BlockSpec((bm, bn), lambda j, i: (i, j)),
                   pl.BlockSpec((nseg, CONV_W - 1, bn), lambda j, i: (i, 0, j))],
        out_shape=[jax.ShapeDtypeStruct((m, dff), BF16),
                   jax.ShapeDtypeStruct((m // seq, CONV_W - 1, dff), F32)],
        scratch_shapes=[pltpu.VMEM((d, bn), BF16), pltpu.VMEM((d, bn), BF16),
                        pltpu.VMEM((bm + 8, bn), F32)],
        compiler_params=_params("parallel", "arbitrary"),
        name="ffn_up",
    )(h, w_up, w_up, conv_w, _rows3(conv_b), conv_state)


def _hgrn_kernel(q_ref, fz_ref, v_ref, gate_ref, lbp_ref, gout_ref, s0_ref, o_ref, sout_ref,
                 st_ref, b_ref, k_ref, qe_ref, kd_ref, oacc_ref, *, layer, seq_blocks):
    c = pl.program_id(1)
    t_rows = q_ref.shape[0]
    nblk = t_rows // HGRN_BLOCK
    carry = seq_blocks is None

    if carry:
        @pl.when(c == 0)
        def _():
            st_ref[...] = jnp.zeros_like(st_ref)

    lbp = lbp_ref[...]
    e = jnp.exp(lbp - jnp.max(lbp, axis=0, keepdims=True))
    lb = jnp.sum(e[0:layer + 1], axis=0, keepdims=True) / jnp.sum(e, axis=0, keepdims=True)

    f = lb + (1.0 - lb) * jax.nn.sigmoid(fz_ref[...])
    logf = jnp.log(f)
    kk = 1.0 - f
    pos = lax.broadcasted_iota(jnp.int32, (t_rows, LANES), 0) % HGRN_BLOCK
    b = logf
    suf = logf
    sh = 1
    while sh < HGRN_BLOCK:
        b = b + jnp.where(pos >= sh, pltpu.roll(b, sh, 0), 0.0)
        suf = suf + jnp.where(pos < HGRN_BLOCK - sh, pltpu.roll(suf, t_rows - sh, 0), 0.0)
        sh *= 2
    b_ref[...] = b * LOG2E
    k_ref[...] = kk
    qe_ref[...] = (q_ref[...] * jnp.exp(b)).astype(BF16)
    kd_ref[...] = (kk * jnp.exp(suf - logf)).astype(BF16)

    half = HGRN_BLOCK // 2
    row = lax.broadcasted_iota(jnp.int32, (half, LANES), 0)
    lane = lax.broadcasted_iota(jnp.int32, (half, LANES), 1)

    def scores(r0):
        lo, hi = pl.ds(r0, half), pl.ds(r0 + half, half)
        b_lo, b_hi, q_lo, q_hi = b_ref[lo, :], b_ref[hi, :], q_ref[lo, :], q_ref[hi, :]
        sc_lo = jnp.zeros((half, LANES), F32)
        sc_hi = jnp.zeros((half, LANES), F32)
        for s in range(HGRN_BLOCK):
            bs, ks = b_ref[pl.ds(r0 + s, 1), :], k_ref[pl.ds(r0 + s, 1), :]
            col_hi = jnp.sum(jnp.exp2(b_hi - bs) * (q_hi * ks), axis=-1, keepdims=True)
            sc_hi = jnp.where(lane == s, col_hi, sc_hi)
            if s < half:
                col_lo = jnp.sum(jnp.exp2(b_lo - bs) * (q_lo * ks), axis=-1, keepdims=True)
                sc_lo = jnp.where(lane == s, col_lo, sc_lo)
        sc = jnp.concatenate([jnp.where(row >= lane, sc_lo, 0.0),
                              jnp.where(row + half >= lane, sc_hi, 0.0)], axis=0)
        return sc[:, 0:HGRN_BLOCK].astype(BF16)

    group = min(nblk, HGRN_GROUP) if carry else nblk

    def blocks(jg, st):
        rows, sc, vb, decay, upd = [], [], [], [], []
        for g in range(group):
            r0 = pl.multiple_of((jg * group + g) * HGRN_BLOCK, HGRN_BLOCK)
            rows.append(pl.ds(r0, HGRN_BLOCK))
            vb.append(v_ref[rows[g], :].astype(BF16))
            sc.append(scores(r0))
            decay.append(jnp.exp2(b_ref[pl.ds(r0 + HGRN_BLOCK - 1, 1), :]))
            upd.append(lax.dot_general(vb[g], kd_ref[rows[g], :], (((0,), (0,)), ((), ())),
                                       preferred_element_type=F32))
        states = []
        for g in range(group):
            if not carry and g % seq_blocks == 0:
                st = s0_ref[g // seq_blocks].T
            states.append(st.astype(BF16))
            st = st * decay[g] + upd[g]
            if not carry and (g + 1) % seq_blocks == 0:
                sout_ref[g // seq_blocks] = st.T
        for g in range(group):
            o = lax.dot_general(qe_ref[rows[g], :], states[g], (((1,), (1,)), ((), ())),
                                preferred_element_type=F32)
            oacc_ref[rows[g], :] = o + jnp.dot(sc[g], vb[g], preferred_element_type=F32)
        return st

    if carry:
        st_ref[...] = lax.fori_loop(0, nblk // group, blocks, st_ref[...])
    else:
        blocks(0, None)

    o = oacc_ref[...]
    y = o * lax.rsqrt(jnp.mean(o * o, axis=-1, keepdims=True) + EPS) * gout_ref[...]
    gate = gate_ref[...]
    o_ref[...] = (y * (gate * jax.nn.sigmoid(gate))).astype(BF16)

    if carry:
        @pl.when(c == pl.num_programs(1) - 1)
        def _():
            sout_ref[...] = st_ref[...].T


def _hgrn_scan(proj, lower_bounds, out_norm, state, layer, jl, n_prompt_rows, seq):
    nb, nh, dk, dv = state.shape[1:]
    assert dk == LANES and dv == LANES
    scratch = lambda t: [pltpu.VMEM((dv, dk), F32), pltpu.VMEM((t, dk), F32), pltpu.VMEM((t, dk), F32),
                         pltpu.VMEM((t, dk), BF16), pltpu.VMEM((t, dk), BF16), pltpu.VMEM((t, dv), F32)]

    def specs(t, row_of):
        seg = lambda k: pl.BlockSpec((t, LANES), lambda h, c: (row_of(c), k * nh + h))
        return [seg(0), seg(1), seg(2), seg(3),
                pl.BlockSpec((lower_bounds.shape[0], LANES), lambda h, c: (0, h)),
                pl.BlockSpec((None, 1, dv), lambda h, c: (jl, 0, 0))]

    out_norm = _rows3(out_norm)
    tp = _tile(n_prompt_rows, 512)
    o_p, s_p = pl.pallas_call(
        functools.partial(_hgrn_kernel, layer=layer, seq_blocks=None),
        grid=(nh, n_prompt_rows // tp),
        in_specs=specs(tp, lambda c: c) + [pl.BlockSpec((None, None, None, dk, dv),
                                                        lambda h, c: (jl, 0, h, 0, 0))],
        out_specs=[pl.BlockSpec((tp, dv), lambda h, c: (c, h)),
                   pl.BlockSpec((None, dk, dv), lambda h, c: (h, 0, 0))],
        out_shape=[jax.ShapeDtypeStruct((n_prompt_rows, nh * dv), BF16),
                   jax.ShapeDtypeStruct((nh, dk, dv), F32)],
        scratch_shapes=scratch(tp),
        compiler_params=_params("parallel", "arbitrary"),
        name="hgrn_prompt",
    )(proj, proj, proj, proj, lower_bounds, out_norm, state)
    ns = _tile(nb, max(1, HGRN_GROUP * HGRN_BLOCK // seq))
    ts = ns * seq
    assert n_prompt_rows % ts == 0 and seq % HGRN_BLOCK == 0
    r0 = n_prompt_rows // ts
    o_s, s_s = pl.pallas_call(
        functools.partial(_hgrn_kernel, layer=layer, seq_blocks=seq // HGRN_BLOCK),
        grid=(nh, nb // ns),
        in_specs=specs(ts, lambda c: r0 + c) + [pl.BlockSpec((None, ns, None, dk, dv),
                                                              lambda h, c: (jl, c, h, 0, 0))],
        out_specs=[pl.BlockSpec((ts, dv), lambda h, c: (c, h)),
                   pl.BlockSpec((ns, None, dk, dv), lambda h, c: (c, h, 0, 0))],
        out_shape=[jax.ShapeDtypeStruct((nb * seq, nh * dv), BF16),
                   jax.ShapeDtypeStruct((nb, nh, dk, dv), F32)],
        scratch_shapes=scratch(ts),
        compiler_params=_params("parallel", "arbitrary"),
        name="hgrn_sample",
    )(proj, proj, proj, proj, lower_bounds, out_norm, state)
    return (o_p, o_s), s_p, s_s


def _lanes(x, n):
    return x[:, :n] if n <= LANES else jnp.concatenate([x] * (n // LANES), axis=1)


def _softmax_step(c, q, k, v, m_ref, l_ref, acc_ref, mask):
    s = lax.dot_general(q, k, (((1,), (1,)), ((), ())), preferred_element_type=F32)
    if mask is not None:
        s = jnp.where(mask, s, -jnp.inf)
    m_prev = m_ref[c]
    m_new = jnp.maximum(m_prev, jnp.max(s, axis=-1, keepdims=True))
    alpha = jnp.exp2(m_prev - m_new)
    p = jnp.exp2(s - _lanes(m_new, s.shape[1]))
    l_ref[c] = alpha * l_ref[c] + jnp.sum(p, axis=-1, keepdims=True)
    acc_ref[c] = (_lanes(alpha, v.shape[1]) * acc_ref[c]
                  + jnp.dot(p.astype(BF16), v, preferred_element_type=F32))
    m_ref[c] = m_new


def _diff_finish(a0, l0, a1, l1, lam_refs, subln, lam_init):
    lq1, lk1, lq2, lk2 = [r[...] for r in lam_refs]
    lam = (jnp.exp(jnp.sum(lq1 * lk1, axis=-1, keepdims=True))
           - jnp.exp(jnp.sum(lq2 * lk2, axis=-1, keepdims=True)) + lam_init)
    o = a0 * _lanes(1.0 / l0, a0.shape[1]) - lam * (a1 * _lanes(1.0 / l1, a1.shape[1]))
    y = o * lax.rsqrt(jnp.mean(o * o, axis=-1, keepdims=True) + EPS)
    return (y * subln * (1.0 - lam_init)).astype(BF16)


def _attn_prompt_kernel(it_ref, jt_ref, q_ref, k_ref, v_ref, lq1, lk1, lq2, lk2, sub_ref, o_ref,
                        m_ref, l_ref, acc_ref, *, lam_init):
    p = pl.program_id(1)
    i, j = it_ref[p], jt_ref[p]
    bq, bk = q_ref.shape[0], k_ref.shape[0]
    dh = q_ref.shape[1] // 2

    @pl.when(j == 0)
    def _():
        m_ref[...] = jnp.full(m_ref.shape, -jnp.inf, F32)
        l_ref[...] = jnp.zeros(l_ref.shape, F32)
        acc_ref[...] = jnp.zeros(acc_ref.shape, F32)

    def step(mask):
        v = v_ref[...]
        for c in range(2):
            _softmax_step(c, q_ref[:, c * dh:(c + 1) * dh], k_ref[:, c * dh:(c + 1) * dh], v,
                          m_ref, l_ref, acc_ref, mask)

    last = (i * bq) // bk

    @pl.when(j < last)
    def _():
        step(None)

    @pl.when(j == last)
    def _():
        qpos = i * bq + lax.broadcasted_iota(jnp.int32, (bq, bk), 0)
        kpos = j * bk + lax.broadcasted_iota(jnp.int32, (bq, bk), 1)
        step(kpos < (qpos // CHUNK + 1) * CHUNK)
        o_ref[...] = _diff_finish(acc_ref[0], l_ref[0], acc_ref[1], l_ref[1],
                                  (lq1, lk1, lq2, lk2), sub_ref[...], lam_init)


def _attn_prompt(qb, kb, vb, lams, subln, jl, lam_init, n_rows, nh):
    dv = qb.shape[1] // nh
    bq, bk = _tile(n_rows, ATTN_BQ), _tile(n_rows, ATTN_BK)
    assert bq % CHUNK == 0 and bk % bq == 0
    nq = n_rows // bq
    pairs = [(i, j) for i in range(nq) for j in range((i * bq) // bk + 1)]
    it = jnp.asarray([p[0] for p in pairs], jnp.int32)
    jt = jnp.asarray([p[1] for p in pairs], jnp.int32)
    lam_spec = pl.BlockSpec((None, 1, dv // 2), lambda h, p, it, jt: (jl, 0, 0))
    return pl.pallas_call(
        functools.partial(_attn_prompt_kernel, lam_init=lam_init),
        grid_spec=pltpu.PrefetchScalarGridSpec(
            num_scalar_prefetch=2,
            grid=(nh, len(pairs)),
            in_specs=[pl.BlockSpec((bq, dv), lambda h, p, it, jt: (it[p], h)),
                      pl.BlockSpec((bk, dv), lambda h, p, it, jt: (jt[p], h)),
                      pl.BlockSpec((bk, dv), lambda h, p, it, jt: (jt[p], h)),
                      lam_spec, lam_spec, lam_spec, lam_spec,
                      pl.BlockSpec((None, 1, dv), lambda h, p, it, jt: (jl, 0, 0))],
            out_specs=pl.BlockSpec((bq, dv), lambda h, p, it, jt: (it[p], h)),
            scratch_shapes=[pltpu.VMEM((2, bq, LANES), F32), pltpu.VMEM((2, bq, LANES), F32),
                            pltpu.VMEM((2, bq, dv), F32)]),
        out_shape=jax.ShapeDtypeStruct((n_rows, nh * dv), BF16),
        compiler_params=_params("parallel", "arbitrary"),
        name="attn_prompt",
    )(it, jt, qb, kb, vb, *[_rows3(a) for a in lams], _rows3(subln))


def _attn_sample_kernel(q_ref, ck_ref, cv_ref, kn_ref, vn_ref, lq1, lk1, lq2, lk2, sub_ref, o_ref,
                        m_ref, l_ref, acc_ref, *, lam_init, nh):
    j = pl.program_id(1)
    last = pl.num_programs(1) - 1
    dv = q_ref.shape[1] // nh
    dh = dv // 2

    @pl.when(j == 0)
    def _():
        m_ref[...] = jnp.full(m_ref.shape, -jnp.inf, F32)
        l_ref[...] = jnp.zeros(l_ref.shape, F32)
        acc_ref[...] = jnp.zeros(acc_ref.shape, F32)

    def step(k_of, v_of):
        for h in range(nh):
            v = v_of(h)
            for c in range(2):
                lo = h * dv + c * dh
                _softmax_step(2 * h + c, q_ref[:, lo:lo + dh], k_of(h, c), v, m_ref, l_ref, acc_ref, None)

    @pl.when(j < last)
    def _():
        bk = ck_ref.shape[0] // (2 * nh)

        def rows(ref, first):
            return ref[pl.ds(first, bk, stride=2 * nh), :].astype(BF16)

        step(lambda h, c: rows(ck_ref, 2 * h + c),
             lambda h: jnp.concatenate([rows(cv_ref, h), rows(cv_ref, nh + h)], axis=1))

    @pl.when(j == last)
    def _():
        step(lambda h, c: kn_ref[:, h * dv + c * dh:h * dv + (c + 1) * dh],
             lambda h: vn_ref[:, h * dv:(h + 1) * dv])
        for h in range(nh):
            o_ref[:, h * dv:(h + 1) * dv] = _diff_finish(
                acc_ref[2 * h], l_ref[2 * h], acc_ref[2 * h + 1], l_ref[2 * h + 1],
                (lq1, lk1, lq2, lk2), sub_ref[...], lam_init)


def _attn_sample(qb, kb, vb, cache_k, cache_v, lams, subln, jl, lam_init, n_prompt_rows, seq, nh):
    d = qb.shape[1]
    dv = d // nh
    nb, past = cache_k.shape[1], cache_k.shape[2]
    bk = _tile(past, 512)
    nkc = past // bk
    r0 = n_prompt_rows // seq
    new_spec = pl.BlockSpec((seq, d), lambda b, j: (r0 + b, 0))
    nl = cache_k.shape[0]
    ck = cache_k.reshape(nl, nb, past * nh * 2, dv // 2)
    cv = cache_v.reshape(nl, nb, past, nh, 2, dv // 2).transpose(0, 1, 2, 4, 3, 5).reshape(ck.shape)
    cache_spec = pl.BlockSpec((None, None, bk * nh * 2, dv // 2),
                              lambda b, j: (jl, b, jnp.minimum(j, nkc - 1), 0))
    lam_spec = pl.BlockSpec((None, 1, dv // 2), lambda b, j: (jl, 0, 0))
    return pl.pallas_call(
        functools.partial(_attn_sample_kernel, lam_init=lam_init, nh=nh),
        grid=(nb, nkc + 1),
        in_specs=[new_spec, cache_spec, cache_spec, new_spec, new_spec,
                  lam_spec, lam_spec, lam_spec, lam_spec,
                  pl.BlockSpec((None, 1, dv), lambda b, j: (jl, 0, 0))],
        out_specs=pl.BlockSpec((seq, d), lambda b, j: (b, 0)),
        out_shape=jax.ShapeDtypeStruct((nb * seq, d), BF16),
        scratch_shapes=[pltpu.VMEM((2 * nh, seq, LANES), F32), pltpu.VMEM((2 * nh, seq, LANES), F32),
                        pltpu.VMEM((2 * nh, seq, dv), F32)],
        compiler_params=_params("parallel", "arbitrary"),
        name="attn_sample",
    )(qb, ck, cv, kb, vb, *[_rows3(a) for a in lams], _rows3(subln))


def kernel(x_prompt, x_sample, state_hgrn, cache_k, cache_v, state_ffn_conv, norm_mix, norm_ffn, hgrn_lower_bounds, w_hgrn_in, w_hgrn_out, hgrn_out_norm, w_diff_in, w_diff_out, diff_q_norm, diff_k_norm, diff_lambda_q1, diff_lambda_k1, diff_lambda_q2, diff_lambda_k2, diff_subln, w_ffn_up, ffn_conv_w, ffn_conv_b, w_ffn_down):
    bp, seq_p, d = x_prompt.shape
    nb, seq_s, _ = x_sample.shape
    assert bp == 1 and seq_s == CHUNK
    depth = norm_mix.shape[0]
    n_mixers = 2
    n_p = bp * seq_p
    diff_heads, dh = cache_k.shape[3], cache_k.shape[5]
    x = jnp.concatenate([x_prompt.reshape(n_p, d), x_sample.reshape(nb * seq_s, d)], axis=0)

    hgrn_p, hgrn_s, kfs, vfs, tails = [], [], [], [], []
    for i in range(depth):
        jl = i // n_mixers
        h = _rms_norm_bf16(x, norm_mix, i)
        if i % n_mixers == 0:
            proj = _matmul(h, w_hgrn_in, jl, n_p, bn=1024, name="hgrn_in")
            o, s_p, s_s = _hgrn_scan(proj, hgrn_lower_bounds, hgrn_out_norm, state_hgrn, i, jl, n_p, seq_s)
            hgrn_p.append(s_p[None])
            hgrn_s.append(s_s)
            x = _matmul(o, w_hgrn_out, jl, n_p, res=x, name="hgrn_out")
        else:
            lam_init = 0.8 - 0.6 * math.exp(-0.3 * i)
            lams = (diff_lambda_q1, diff_lambda_k1, diff_lambda_q2, diff_lambda_k2)
            qb, kf, kb, vf, vb = _diff_project(h, w_diff_in, jl, diff_q_norm, diff_k_norm, dh, n_p)
            o_p = _attn_prompt(qb, kb, vb, lams, diff_subln, jl, lam_init, n_p, diff_heads)
            o_s = _attn_sample(qb, kb, vb, cache_k, cache_v, lams, diff_subln, jl, lam_init,
                               n_p, seq_s, diff_heads)
            kfs.append(kf)
            vfs.append(vf)
            x = _matmul((o_p, o_s), w_diff_out, jl, n_p, res=x, name="diff_out")
        h = _rms_norm_bf16(x, norm_ffn, i)
        act, tail = _ffn_up(h, w_ffn_up, ffn_conv_w, ffn_conv_b, state_ffn_conv, i, n_p, seq_s)
        tails.append(tail)
        x = _matmul(act, w_ffn_down, i, n_p, res=x, split_out=i == depth - 1, bm=512, bn=1024,
                    w_buffers=1, name="ffn_down")

    x_p, x_s = x
    tail = jnp.stack(tails)
    seg_p = n_p // seq_s
    stack = lambda pairs, k: jnp.stack([p[k] for p in pairs])
    return (x_p.reshape(bp, seq_p, d),
            x_s.reshape(nb, seq_s, d),
            jnp.stack(hgrn_p),
            jnp.stack(hgrn_s),
            stack(kfs, 0).reshape(-1, bp, seq_p, diff_heads, 2, dh),
            stack(vfs, 0).reshape(-1, bp, seq_p, diff_heads, 2 * dh),
            stack(kfs, 1).reshape(-1, nb, seq_s, diff_heads, 2, dh),
            stack(vfs, 1).reshape(-1, nb, seq_s, diff_heads, 2 * dh),
            tail[:, seg_p - 1][:, None],
            tail[:, seg_p:])
```

```python
import functools
import math

import jax
import jax.numpy as jnp
from jax import lax
from jax.experimental import pallas as pl
from jax.experimental.pallas import tpu as pltpu

EPS = 1e-6
LOG2E = math.log2(math.e)
CHUNK = 64
HGRN_BLOCK = 16
ATTN_TILE = 1024
HGRN_GROUP = 16
CONV_W = 3
LANES = 128
V7X_VMEM_BYTES = 64 * 1024 * 1024
VMEM_LIMIT = V7X_VMEM_BYTES - 8 * 1024 * 1024

F32 = jnp.float32
BF16 = jnp.bfloat16


def _params(*sem):
    return pltpu.CompilerParams(dimension_semantics=sem, vmem_limit_bytes=VMEM_LIMIT)


def _rows3(a):
    return a.reshape(a.shape[0], 1, a.shape[1])


def _tile(n, pref):
    t = min(n, pref)
    while n % t:
        t //= 2
    return t


def _split_specs(n_p, n_s, bm, bn, col):
    npt, nst = n_p // bm, n_s // bm
    return [pl.BlockSpec((bm, bn), lambda j, i: (jnp.minimum(i, npt - 1), col(j))),
            pl.BlockSpec((bm, bn), lambda j, i: (jnp.clip(i - npt, 0, nst - 1), col(j)))]


def _on_rows(i, npt, split, fn, *ref_pairs):
    if not split:
        fn(*[p[0] for p in ref_pairs])
        return
    pl.when(i < npt)(lambda: fn(*[p[0] for p in ref_pairs]))
    pl.when(i >= npt)(lambda: fn(*[p[-1] for p in ref_pairs]))


def _norm_kernel(*refs, npt):
    *x_refs, g_ref, o_ref = refs

    def run(x_ref):
        x = x_ref[...]
        y = x * lax.rsqrt(jnp.mean(x * x, axis=-1, keepdims=True) + EPS)
        o_ref[...] = (y * g_ref[...]).astype(o_ref.dtype)

    _on_rows(pl.program_id(0), npt, len(x_refs) == 2, run, x_refs)


def _rms_norm_bf16(x, gains, layer):
    xs = x if isinstance(x, tuple) else (x,)
    d = xs[0].shape[1]
    m = sum(a.shape[0] for a in xs)
    bm = _tile(math.gcd(*[a.shape[0] for a in xs]), 512)
    npt = xs[0].shape[0] // bm
    if len(xs) == 2:
        nst = xs[1].shape[0] // bm
        x_specs = [pl.BlockSpec((bm, d), lambda i: (jnp.minimum(i, npt - 1), 0)),
                   pl.BlockSpec((bm, d), lambda i: (jnp.clip(i - npt, 0, nst - 1), 0))]
    else:
        x_specs = [pl.BlockSpec((bm, d), lambda i: (i, 0))]
    return pl.pallas_call(
        functools.partial(_norm_kernel, npt=npt),
        grid=(m // bm,),
        in_specs=x_specs + [pl.BlockSpec((None, 1, d), lambda i: (layer, 0, 0))],
        out_specs=pl.BlockSpec((bm, d), lambda i: (i, 0)),
        out_shape=jax.ShapeDtypeStruct((m, d), BF16),
        compiler_params=_params("parallel"),
        name="rms_norm",
    )(*xs, _rows3(gains))


def _mm_kernel(*refs, n_res, split_in, split_out, npt):
    refs = list(refs)
    h_refs = [refs.pop(0) for _ in range(2 if split_in else 1)]
    w_ref = refs.pop(0)
    r_refs = [refs.pop(0) for _ in range(n_res)] or [None]
    o_refs = [refs.pop(0) for _ in range(2 if split_out else 1)]
    wb_ref, = refs
    i = pl.program_id(1)

    @pl.when(i == 0)
    def _():
        wb_ref[...] = w_ref[...].astype(BF16)

    def run(h_ref, r_ref, o_ref):
        acc = jnp.dot(h_ref[...], wb_ref[...], preferred_element_type=F32)
        o_ref[...] = acc if r_ref is None else r_ref[...] + acc

    _on_rows(i, npt, split_in or split_out or n_res == 2, run, h_refs, r_refs, o_refs)


def _matmul(h, w, layer, n_p, *, res=None, split_out=False, bm=1024, bn=512, w_buffers=2, name="matmul"):
    split_in = isinstance(h, tuple)
    k, ncols = w.shape[1], w.shape[2]
    m = sum(a.shape[0] for a in h) if split_in else h.shape[0]
    n_s = m - n_p
    bm, bn = _tile(math.gcd(n_p, n_s), bm), _tile(ncols, bn)
    npt = n_p // bm
    in_specs = (_split_specs(n_p, n_s, bm, k, lambda j: 0) if split_in
                else [pl.BlockSpec((bm, k), lambda j, i: (i, 0))])
    in_specs.append(pl.BlockSpec((None, k, bn), lambda j, i: (layer, 0, j),
                                 pipeline_mode=pl.Buffered(w_buffers)))
    args = list(h) if split_in else [h]
    args.append(w)
    res = () if res is None else res if isinstance(res, tuple) else (res,)
    in_specs += (_split_specs(n_p, n_s, bm, bn, lambda j: j) if len(res) == 2
                 else [pl.BlockSpec((bm, bn), lambda j, i: (i, j))] * len(res))
    args += res
    if split_out:
        out_specs = _split_specs(n_p, n_s, bm, bn, lambda j: j)
        out_shape = [jax.ShapeDtypeStruct((n_p, ncols), F32), jax.ShapeDtypeStruct((n_s, ncols), F32)]
    else:
        out_specs = pl.BlockSpec((bm, bn), lambda j, i: (i, j))
        out_shape = jax.ShapeDtypeStruct((m, ncols), F32)
    return pl.pallas_call(
        functools.partial(_mm_kernel, n_res=len(res), split_in=split_in, split_out=split_out, npt=npt),
        grid=(ncols // bn, m // bm),
        in_specs=in_specs,
        out_specs=out_specs,
        out_shape=out_shape,
        scratch_shapes=[pltpu.VMEM((k, bn), BF16)],
        compiler_params=_params("parallel", "arbitrary"),
        name=name,
    )(*args)


def _headnorm(acc, g, scale):
    outs = []
    for c in range(acc.shape[1] // LANES):
        blk = acc[:, c * LANES:(c + 1) * LANES]
        y = blk * lax.rsqrt(jnp.mean(blk * blk, axis=-1, keepdims=True) + EPS)
        outs.append(y * g * scale if scale != 1.0 else y * g)
    return jnp.concatenate(outs, axis=1) if len(outs) > 1 else outs[0]


def _qkv_kernel(h_ref, wq_ref, wk_ref, wv_ref, gq_ref, gk_ref,
                qb_ref, kfp_ref, kfs_ref, kb_ref, vfp_ref, vfs_ref, vb_ref,
                wqb_ref, wkb_ref, wvb_ref, *, scale, npt):
    i = pl.program_id(1)

    @pl.when(i == 0)
    def _():
        wqb_ref[...] = wq_ref[...].astype(BF16)
        wkb_ref[...] = wk_ref[...].astype(BF16)
        wvb_ref[...] = wv_ref[...].astype(BF16)

    h = h_ref[...]
    q = jnp.dot(h, wqb_ref[...], preferred_element_type=F32)
    qb_ref[...] = _headnorm(q, gq_ref[...], scale).astype(BF16)
    kn = _headnorm(jnp.dot(h, wkb_ref[...], preferred_element_type=F32), gk_ref[...], 1.0)
    kb_ref[...] = kn.astype(BF16)
    v = jnp.dot(h, wvb_ref[...], preferred_element_type=F32)
    vb_ref[...] = v.astype(BF16)

    def put(kf_ref, vf_ref):
        kf_ref[...] = kn
        vf_ref[...] = v

    _on_rows(i, npt, True, put, (kfp_ref, kfs_ref), (vfp_ref, vfs_ref))


def _diff_project(h, w, layer, q_g, k_g, dh, n_p):
    m, d = h.shape
    n_s = m - n_p
    bm, bn = _tile(math.gcd(n_p, n_s), 1024), _tile(d, 512)
    nj = d // bn
    f_specs = _split_specs(n_p, n_s, bm, bn, lambda j: j)
    f_shapes = [jax.ShapeDtypeStruct((n_p, d), F32), jax.ShapeDtypeStruct((n_s, d), F32)]
    o_spec = pl.BlockSpec((bm, bn), lambda j, i: (i, j))
    b_shape = jax.ShapeDtypeStruct((m, d), BF16)
    g_spec = pl.BlockSpec((None, 1, dh), lambda j, i: (layer, 0, 0))

    def w_spec(seg):
        return pl.BlockSpec((None, d, bn), lambda j, i: (layer, 0, seg * nj + j),
                            pipeline_mode=pl.Buffered(1))

    qb, kf_p, kf_s, kb, vf_p, vf_s, vb = pl.pallas_call(
        functools.partial(_qkv_kernel, scale=dh ** -0.5 * LOG2E, npt=n_p // bm),
        grid=(nj, m // bm),
        in_specs=[pl.BlockSpec((bm, d), lambda j, i: (i, 0)), w_spec(0), w_spec(1), w_spec(2),
                  g_spec, g_spec],
        out_specs=[o_spec] + f_specs + [o_spec] + f_specs + [o_spec],
        out_shape=[b_shape] + f_shapes + [b_shape] + f_shapes + [b_shape],
        scratch_shapes=[pltpu.VMEM((d, bn), BF16)] * 3,
        compiler_params=_params("parallel", "arbitrary"),
        name="diff_qkv",
    )(h, w, w, w, _rows3(q_g), _rows3(k_g))
    return qb, (kf_p, kf_s), kb, (vf_p, vf_s), vb


def _ffn_up_kernel(h_ref, wg_ref, wu_ref, cw_ref, cb_ref, st_ref, a_ref, tail_ref,
                   wgb_ref, wub_ref, g_ref, *, n_prompt_tiles, seq):
    i = pl.program_id(1)
    bm, bn = a_ref.shape
    nseg = bm // seq

    @pl.when(i == 0)
    def _():
        wgb_ref[...] = wg_ref[...].astype(BF16)
        wub_ref[...] = wu_ref[...].astype(BF16)
        g_ref[0:8, :] = jnp.zeros((8, bn), F32)

    h = h_ref[...]
    g = jnp.dot(h, wgb_ref[...], preferred_element_type=F32)
    u = jnp.dot(h, wub_ref[...], preferred_element_type=F32)
    g_ref[8:8 + bm, :] = g
    g1 = g_ref[7:7 + bm, :]
    g2 = g_ref[6:6 + bm, :]
    st = st_ref[...]
    p2 = jnp.broadcast_to(st[:, 0:1, :], (nseg, seq, bn)).reshape(bm, bn)
    p1 = jnp.broadcast_to(st[:, 1:2, :], (nseg, seq, bn)).reshape(bm, bn)
    pos = (lax.broadcasted_iota(jnp.int32, (bm, bn), 0) % seq
           + jnp.where(i >= n_prompt_tiles, 0, seq))
    g1 = jnp.where(pos == 0, p1, g1)
    g2 = jnp.where(pos == 0, p2, jnp.where(pos == 1, p1, g2))
    cw = cw_ref[...]
    conv = cb_ref[...] + cw[2:3, :] * g + cw[1:2, :] * g1 + cw[0:1, :] * g2
    a_ref[...] = (conv * jax.nn.sigmoid(conv) * u).astype(BF16)

    for n in range(nseg):
        end = 8 + (n + 1) * seq
        tail_ref[n] = g_ref[end - (CONV_W - 1):end, :]
    g_ref[0:8, :] = g_ref[bm:bm + 8, :]


def _ffn_up(h, w_up, conv_w, conv_b, conv_state, layer, n_prompt_rows, seq):
    m, d = h.shape
    dff = conv_w.shape[-1]
    nb = conv_state.shape[1]
    bm = _tile(math.gcd(n_prompt_rows, nb * seq), 1024)
    bn = _tile(dff, 512)
    assert bm % seq == 0 and (m - n_prompt_rows) == nb * seq
    nseg = bm // seq
    npt = n_prompt_rows // bm
    nj = dff // bn
    return pl.pallas_call(
        functools.partial(_ffn_up_kernel, n_prompt_tiles=npt, seq=seq),
        grid=(nj, m // bm),
        in_specs=[pl.BlockSpec((bm, d), lambda j, i: (i, 0)),
                  pl.BlockSpec((None, d, bn), lambda j, i: (layer, 0, j)),
                  pl.BlockSpec((None, d, bn), lambda j, i: (layer, 0, nj + j)),
                  pl.BlockSpec((None, CONV_W, bn), lambda j, i: (layer, 0, j)),
                  pl.BlockSpec((None, 1, bn), lambda j, i: (layer, 0, j)),
                  pl.BlockSpec((None, nseg, CONV_W - 1, bn),
                               lambda j, i: (layer, jnp.maximum(i - npt, 0), 0, j))],
        out_specs=[pl.BlockSpec((bm, bn), lambda j, i: (i, j)),
                   pl.BlockSpec((nseg, CONV_W - 1, bn), lambda j, i: (i, 0, j))],
        out_shape=[jax.ShapeDtypeStruct((m, dff), BF16),
                   jax.ShapeDtypeStruct((m // seq, CONV_W - 1, dff), F32)],
        scratch_shapes=[pltpu.VMEM((d, bn), BF16), pltpu.VMEM((d, bn), BF16),
                        pltpu.VMEM((bm + 8, bn), F32)],
        compiler_params=_params("parallel", "arbitrary"),
        name="ffn_up",
    )(h, w_up, w_up, conv_w, _rows3(conv_b), conv_state)


def _hgrn_kernel(q_ref, fz_ref, v_ref, gate_ref, lbp_ref, gout_ref, s0_ref, o_ref, sout_ref,
                 st_ref, b_ref, k_ref, qe_ref, kd_ref, oacc_ref, *, layer, seq_blocks):
    c = pl.program_id(1)
    t_rows = q_ref.shape[0]
    nblk = t_rows // HGRN_BLOCK
    carry = seq_blocks is None

    if carry:
        @pl.when(c == 0)
        def _():
            st_ref[...] = jnp.zeros_like(st_ref)

    lbp = lbp_ref[...]
    e = jnp.exp(lbp - jnp.max(lbp, axis=0, keepdims=True))
    lb = jnp.sum(e[0:layer + 1], axis=0, keepdims=True) / jnp.sum(e, axis=0, keepdims=True)

    f = lb + (1.0 - lb) * jax.nn.sigmoid(fz_ref[...])
    logf = jnp.log(f)
    kk = 1.0 - f
    pos = lax.broadcasted_iota(jnp.int32, (t_rows, LANES), 0) % HGRN_BLOCK
    b = logf
    sh = 1
    while sh < HGRN_BLOCK:
        b = b + jnp.where(pos >= sh, pltpu.roll(b, sh, 0), 0.0)
        sh *= 2
    b3 = b.reshape(nblk, HGRN_BLOCK, LANES)
    b_last = jnp.broadcast_to(b3[:, HGRN_BLOCK - 1:, :], b3.shape).reshape(t_rows, LANES)
    b_ref[...] = b * LOG2E
    k_ref[...] = kk
    qe_ref[...] = (q_ref[...] * jnp.exp(b)).astype(BF16)
    kd_ref[...] = (kk * jnp.exp(b_last - b)).astype(BF16)

    half = HGRN_BLOCK // 2
    row = lax.broadcasted_iota(jnp.int32, (half, LANES), 0)
    lane = lax.broadcasted_iota(jnp.int32, (half, LANES), 1)

    def scores(r0):
        lo, hi = pl.ds(r0, half), pl.ds(r0 + half, half)
        b_lo, b_hi, q_lo, q_hi = b_ref[lo, :], b_ref[hi, :], q_ref[lo, :], q_ref[hi, :]
        sc_lo = jnp.zeros((half, LANES), F32)
        sc_hi = jnp.zeros((half, LANES), F32)
        for s in range(HGRN_BLOCK):
            bs, ks = b_ref[pl.ds(r0 + s, 1), :], k_ref[pl.ds(r0 + s, 1), :]
            col_hi = jnp.sum(jnp.exp2(b_hi - bs) * (q_hi * ks), axis=-1, keepdims=True)
            sc_hi = jnp.where(lane == s, col_hi, sc_hi)
            if s < half:
                col_lo = jnp.sum(jnp.exp2(b_lo - bs) * (q_lo * ks), axis=-1, keepdims=True)
                sc_lo = jnp.where(lane == s, col_lo, sc_lo)
        sc = jnp.concatenate([jnp.where(row >= lane, sc_lo, 0.0),
                              jnp.where(row + half >= lane, sc_hi, 0.0)], axis=0)
        return sc[:, 0:HGRN_BLOCK].astype(BF16)

    group = min(nblk, HGRN_GROUP) if carry else nblk

    def blocks(jg, st):
        rows, sc, vb, decay, upd = [], [], [], [], []
        for g in range(group):
            r0 = pl.multiple_of((jg * group + g) * HGRN_BLOCK, HGRN_BLOCK)
            rows.append(pl.ds(r0, HGRN_BLOCK))
            vb.append(v_ref[rows[g], :].astype(BF16))
            sc.append(scores(r0))
            decay.append(jnp.exp2(b_ref[pl.ds(r0 + HGRN_BLOCK - 1, 1), :]))
            upd.append(lax.dot_general(vb[g], kd_ref[rows[g], :], (((0,), (0,)), ((), ())),
                                       preferred_element_type=F32))
        states = []
        for g in range(group):
            if not carry and g % seq_blocks == 0:
                st = s0_ref[g // seq_blocks].T
            states.append(st.astype(BF16))
            st = st * decay[g] + upd[g]
            if not carry and (g + 1) % seq_blocks == 0:
                sout_ref[g // seq_blocks] = st.T
        for g in range(group):
            o = lax.dot_general(qe_ref[rows[g], :], states[g], (((1,), (1,)), ((), ())),
                                preferred_element_type=F32)
            oacc_ref[rows[g], :] = o + jnp.dot(sc[g], vb[g], preferred_element_type=F32)
        return st

    if carry:
        st_ref[...] = lax.fori_loop(0, nblk // group, blocks, st_ref[...])
    else:
        blocks(0, None)

    o = oacc_ref[...]
    y = o * lax.rsqrt(jnp.mean(o * o, axis=-1, keepdims=True) + EPS) * gout_ref[...]
    gate = gate_ref[...]
    o_ref[...] = (y * (gate * jax.nn.sigmoid(gate))).astype(BF16)

    if carry:
        @pl.when(c == pl.num_programs(1) - 1)
        def _():
            sout_ref[...] = st_ref[...].T


def _hgrn_scan(proj, lower_bounds, out_norm, state, layer, jl, n_prompt_rows, seq):
    nb, nh, dk, dv = state.shape[1:]
    assert dk == LANES and dv == LANES
    scratch = lambda t: [pltpu.VMEM((dv, dk), F32), pltpu.VMEM((t, dk), F32), pltpu.VMEM((t, dk), F32),
                         pltpu.VMEM((t, dk), BF16), pltpu.VMEM((t, dk), BF16), pltpu.VMEM((t, dv), F32)]

    def specs(t, row_of):
        seg = lambda k: pl.BlockSpec((t, LANES), lambda h, c: (row_of(c), k * nh + h))
        return [seg(0), seg(1), seg(2), seg(3),
                pl.BlockSpec((lower_bounds.shape[0], LANES), lambda h, c: (0, h)),
                pl.BlockSpec((None, 1, dv), lambda h, c: (jl, 0, 0))]

    out_norm = _rows3(out_norm)
    tp = _tile(n_prompt_rows, 512)
    o_p, s_p = pl.pallas_call(
        functools.partial(_hgrn_kernel, layer=layer, seq_blocks=None),
        grid=(nh, n_prompt_rows // tp),
        in_specs=specs(tp, lambda c: c) + [pl.BlockSpec((None, None, None, dk, dv),
                                                        lambda h, c: (jl, 0, h, 0, 0))],
        out_specs=[pl.BlockSpec((tp, dv), lambda h, c: (c, h)),
                   pl.BlockSpec((None, dk, dv), lambda h, c: (h, 0, 0))],
        out_shape=[jax.ShapeDtypeStruct((n_prompt_rows, nh * dv), BF16),
                   jax.ShapeDtypeStruct((nh, dk, dv), F32)],
        scratch_shapes=scratch(tp),
        compiler_params=_params("parallel", "arbitrary"),
        name="hgrn_prompt",
    )(proj, proj, proj, proj, lower_bounds, out_norm, state)
    ns = _tile(nb, max(1, HGRN_GROUP * HGRN_BLOCK // seq))
    ts = ns * seq
    assert n_prompt_rows % ts == 0 and seq % HGRN_BLOCK == 0
    r0 = n_prompt_rows // ts
    o_s, s_s = pl.pallas_call(
        functools.partial(_hgrn_kernel, layer=layer, seq_blocks=seq // HGRN_BLOCK),
        grid=(nh, nb // ns),
        in_specs=specs(ts, lambda c: r0 + c) + [pl.BlockSpec((None, ns, None, dk, dv),
                                                              lambda h, c: (jl, c, h, 0, 0))],
        out_specs=[pl.BlockSpec((ts, dv), lambda h, c: (c, h)),
                   pl.BlockSpec((ns, None, dk, dv), lambda h, c: (c, h, 0, 0))],
        out_shape=[jax.ShapeDtypeStruct((nb * seq, nh * dv), BF16),
                   jax.ShapeDtypeStruct((nb, nh, dk, dv), F32)],
        scratch_shapes=scratch(ts),
        compiler_params=_params("parallel", "arbitrary"),
        name="hgrn_sample",
    )(proj, proj, proj, proj, lower_bounds, out_norm, state)
    return (o_p, o_s), s_p, s_s


def _lanes(x, n):
    return x[:, :n] if n <= LANES else jnp.concatenate([x] * (n // LANES), axis=1)


def _softmax_step(c, q, k, v, m_ref, l_ref, acc_ref, mask, rows=slice(None)):
    s = lax.dot_general(q, k, (((1,), (1,)), ((), ())), preferred_element_type=F32)
    if mask is not None:
        s = jnp.where(mask, s, -jnp.inf)
    m_prev = m_ref[c, rows]
    m_new = jnp.maximum(m_prev, jnp.max(s, axis=-1, keepdims=True))
    alpha = jnp.exp2(m_prev - m_new)
    p = jnp.exp2(s - _lanes(m_new, s.shape[1]))
    l_ref[c, rows] = alpha * l_ref[c, rows] + jnp.sum(p, axis=-1, keepdims=True)
    acc_ref[c, rows] = (_lanes(alpha, v.shape[1]) * acc_ref[c, rows]
                        + jnp.dot(p.astype(BF16), v, preferred_element_type=F32))
    m_ref[c, rows] = m_new


def _diff_finish(a0, l0, a1, l1, lam_refs, subln, lam_init):
    lq1, lk1, lq2, lk2 = [r[...] for r in lam_refs]
    lam = (jnp.exp(jnp.sum(lq1 * lk1, axis=-1, keepdims=True))
           - jnp.exp(jnp.sum(lq2 * lk2, axis=-1, keepdims=True)) + lam_init)
    o = a0 * _lanes(1.0 / l0, a0.shape[1]) - lam * (a1 * _lanes(1.0 / l1, a1.shape[1]))
    y = o * lax.rsqrt(jnp.mean(o * o, axis=-1, keepdims=True) + EPS)
    return (y * subln * (1.0 - lam_init)).astype(BF16)


def _attn_prompt_kernel(it_ref, jt_ref, q_ref, k_ref, v_ref, lq1, lk1, lq2, lk2, sub_ref, o_ref,
                        m_ref, l_ref, acc_ref, *, lam_init):
    p = pl.program_id(1)
    i, j = it_ref[p], jt_ref[p]
    bq, bk = q_ref.shape[0], k_ref.shape[0]
    dh = q_ref.shape[1] // 2

    @pl.when(j == 0)
    def _():
        m_ref[...] = jnp.full(m_ref.shape, -jnp.inf, F32)
        l_ref[...] = jnp.zeros(l_ref.shape, F32)
        acc_ref[...] = jnp.zeros(acc_ref.shape, F32)

    def step(mask, rows=slice(None), keys=slice(None)):
        v = v_ref[keys, :]
        for c in range(2):
            _softmax_step(c, q_ref[rows, c * dh:(c + 1) * dh], k_ref[keys, c * dh:(c + 1) * dh], v,
                          m_ref, l_ref, acc_ref, mask, rows)

    @pl.when(j < i)
    def _():
        step(None)

    @pl.when(j == i)
    def _():
        half = bq // 2
        qpos = lax.broadcasted_iota(jnp.int32, (half, half), 0)
        kpos = lax.broadcasted_iota(jnp.int32, (half, half), 1)
        chunk_mask = kpos < (qpos // CHUNK + 1) * CHUNK
        top, bottom = slice(0, half), slice(half, bq)
        step(chunk_mask, top, top)
        step(None, bottom, top)
        step(chunk_mask, bottom, bottom)
        o_ref[...] = _diff_finish(acc_ref[0], l_ref[0], acc_ref[1], l_ref[1],
                                  (lq1, lk1, lq2, lk2), sub_ref[...], lam_init)


def _attn_prompt(qb, kb, vb, lams, subln, jl, lam_init, n_rows, nh):
    dv = qb.shape[1] // nh
    bq = bk = _tile(n_rows, ATTN_TILE)
    assert (bq // 2) % CHUNK == 0
    nq = n_rows // bq
    pairs = [(i, j) for i in range(nq) for j in range(i + 1)]
    it = jnp.asarray([p[0] for p in pairs], jnp.int32)
    jt = jnp.asarray([p[1] for p in pairs], jnp.int32)
    lam_spec = pl.BlockSpec((None, 1, dv // 2), lambda h, p, it, jt: (jl, 0, 0))
    return pl.pallas_call(
        functools.partial(_attn_prompt_kernel, lam_init=lam_init),
        grid_spec=pltpu.PrefetchScalarGridSpec(
            num_scalar_prefetch=2,
            grid=(nh, len(pairs)),
            in_specs=[pl.BlockSpec((bq, dv), lambda h, p, it, jt: (it[p], h)),
                      pl.BlockSpec((bk, dv), lambda h, p, it, jt: (jt[p], h)),
                      pl.BlockSpec((bk, dv), lambda h, p, it, jt: (jt[p], h)),
                      lam_spec, lam_spec, lam_spec, lam_spec,
                      pl.BlockSpec((None, 1, dv), lambda h, p, it, jt: (jl, 0, 0))],
            out_specs=pl.BlockSpec((bq, dv), lambda h, p, it, jt: (it[p], h)),
            scratch_shapes=[pltpu.VMEM((2, bq, LANES), F32), pltpu.VMEM((2, bq, LANES), F32),
                            pltpu.VMEM((2, bq, dv), F32)]),
        out_shape=jax.ShapeDtypeStruct((n_rows, nh * dv), BF16),
        compiler_params=_params("parallel", "arbitrary"),
        name="attn_prompt",
    )(it, jt, qb, kb, vb, *[_rows3(a) for a in lams], _rows3(subln))


def _attn_sample_kernel(q_ref, ck_ref, cv_ref, kn_ref, vn_ref, lq1, lk1, lq2, lk2, sub_ref, o_ref,
                        m_ref, l_ref, acc_ref, *, lam_init, nh):
    j = pl.program_id(1)
    last = pl.num_programs(1) - 1
    dv = q_ref.shape[1] // nh
    dh = dv // 2

    @pl.when(j == 0)
    def _():
        m_ref[...] = jnp.full(m_ref.shape, -jnp.inf, F32)
        l_ref[...] = jnp.zeros(l_ref.shape, F32)
        acc_ref[...] = jnp.zeros(acc_ref.shape, F32)

    def step(k_of, v_of):
        for h in range(nh):
            v = v_of(h)
            for c in range(2):
                lo = h * dv + c * dh
                _softmax_step(2 * h + c, q_ref[:, lo:lo + dh], k_of(h, c), v, m_ref, l_ref, acc_ref, None)

    @pl.when(j < last)
    def _():
        bk = ck_ref.shape[0] // (2 * nh)

        def rows(ref, first):
            return ref[pl.ds(first, bk, stride=2 * nh), :].astype(BF16)

        step(lambda h, c: rows(ck_ref, 2 * h + c),
             lambda h: jnp.concatenate([rows(cv_ref, h), rows(cv_ref, nh + h)], axis=1))

    @pl.when(j == last)
    def _():
        step(lambda h, c: kn_ref[:, h * dv + c * dh:h * dv + (c + 1) * dh],
             lambda h: vn_ref[:, h * dv:(h + 1) * dv])
        for h in range(nh):
            o_ref[:, h * dv:(h + 1) * dv] = _diff_finish(
                acc_ref[2 * h], l_ref[2 * h], acc_ref[2 * h + 1], l_ref[2 * h + 1],
                (lq1, lk1, lq2, lk2), sub_ref[...], lam_init)


def _attn_sample(qb, kb, vb, cache_k, cache_v, lams, subln, jl, lam_init, n_prompt_rows, seq, nh):
    d = qb.shape[1]
    dv = d // nh
    nb, past = cache_k.shape[1], cache_k.shape[2]
    bk = _tile(past, 512)
    nkc = past // bk
    r0 = n_prompt_rows // seq
    new_spec = pl.BlockSpec((seq, d), lambda b, j: (r0 + b, 0))
    nl = cache_k.shape[0]
    ck = cache_k.reshape(nl, nb, past * nh * 2, dv // 2)
    cv = cache_v.reshape(nl, nb, past, nh, 2, dv // 2).transpose(0, 1, 2, 4, 3, 5).reshape(ck.shape)
    cache_spec = pl.BlockSpec((None, None, bk * nh * 2, dv // 2),
                              lambda b, j: (jl, b, jnp.minimum(j, nkc - 1), 0))
    lam_spec = pl.BlockSpec((None, 1, dv // 2), lambda b, j: (jl, 0, 0))
    return pl.pallas_call(
        functools.partial(_attn_sample_kernel, lam_init=lam_init, nh=nh),
        grid=(nb, nkc + 1),
        in_specs=[new_spec, cache_spec, cache_spec, new_spec, new_spec,
                  lam_spec, lam_spec, lam_spec, lam_spec,
                  pl.BlockSpec((None, 1, dv), lambda b, j: (jl, 0, 0))],
        out_specs=pl.BlockSpec((seq, d), lambda b, j: (b, 0)),
        out_shape=jax.ShapeDtypeStruct((nb * seq, d), BF16),
        scratch_shapes=[pltpu.VMEM((2 * nh, seq, LANES), F32), pltpu.VMEM((2 * nh, seq, LANES), F32),
                        pltpu.VMEM((2 * nh, seq, dv), F32)],
        compiler_params=_params("parallel", "arbitrary"),
        name="attn_sample",
    )(qb, ck, cv, kb, vb, *[_rows3(a) for a in lams], _rows3(subln))


def kernel(x_prompt, x_sample, state_hgrn, cache_k, cache_v, state_ffn_conv, norm_mix, norm_ffn, hgrn_lower_bounds, w_hgrn_in, w_hgrn_out, hgrn_out_norm, w_diff_in, w_diff_out, diff_q_norm, diff_k_norm, diff_lambda_q1, diff_lambda_k1, diff_lambda_q2, diff_lambda_k2, diff_subln, w_ffn_up, ffn_conv_w, ffn_conv_b, w_ffn_down):
    bp, seq_p, d = x_prompt.shape
    nb, seq_s, _ = x_sample.shape
    assert bp == 1 and seq_s == CHUNK
    depth = norm_mix.shape[0]
    n_mixers = 2
    n_p = bp * seq_p
    diff_heads, dh = cache_k.shape[3], cache_k.shape[5]
    x = (x_prompt.reshape(n_p, d), x_sample.reshape(nb * seq_s, d))

    hgrn_p, hgrn_s, kfs, vfs, tails = [], [], [], [], []
    for i in range(depth):
        jl = i // n_mixers
        h = _rms_norm_bf16(x, norm_mix, i)
        if i % n_mixers == 0:
            proj = _matmul(h, w_hgrn_in, jl, n_p, bn=1024, name="hgrn_in")
            o, s_p, s_s = _hgrn_scan(proj, hgrn_lower_bounds, hgrn_out_norm, state_hgrn, i, jl, n_p, seq_s)
            hgrn_p.append(s_p[None])
            hgrn_s.append(s_s)
            x = _matmul(o, w_hgrn_out, jl, n_p, res=x, name="hgrn_out")
        else:
            lam_init = 0.8 - 0.6 * math.exp(-0.3 * i)
            lams = (diff_lambda_q1, diff_lambda_k1, diff_lambda_q2, diff_lambda_k2)
            qb, kf, kb, vf, vb = _diff_project(h, w_diff_in, jl, diff_q_norm, diff_k_norm, dh, n_p)
            o_p = _attn_prompt(qb, kb, vb, lams, diff_subln, jl, lam_init, n_p, diff_heads)
            o_s = _attn_sample(qb, kb, vb, cache_k, cache_v, lams, diff_subln, jl, lam_init,
                               n_p, seq_s, diff_heads)
            kfs.append(kf)
            vfs.append(vf)
            x = _matmul((o_p, o_s), w_diff_out, jl, n_p, res=x, name="diff_out")
        h = _rms_norm_bf16(x, norm_ffn, i)
        act, tail = _ffn_up(h, w_ffn_up, ffn_conv_w, ffn_conv_b, state_ffn_conv, i, n_p, seq_s)
        tails.append(tail)
        x = _matmul(act, w_ffn_down, i, n_p, res=x, split_out=i == depth - 1, bm=512, bn=1024,
                    w_buffers=1, name="ffn_down")

    x_p, x_s = x
    tail = jnp.stack(tails)
    seg_p = n_p // seq_s
    stack = lambda pairs, k: jnp.stack([p[k] for p in pairs])
    return (x_p.reshape(bp, seq_p, d),
            x_s.reshape(nb, seq_s, d),
            jnp.stack(hgrn_p),
            jnp.stack(hgrn_s),
            stack(kfs, 0).reshape(-1, bp, seq_p, diff_heads, 2, dh),
            stack(vfs, 0).reshape(-1, bp, seq_p, diff_heads, 2 * dh),
            stack(kfs, 1).reshape(-1, nb, seq_s, diff_heads, 2, dh),
            stack(vfs, 1).reshape(-1, nb, seq_s, diff_heads, 2 * dh),
            tail[:, seg_p - 1][:, None],
            tail[:, seg_p:])
```

```python
import functools
import math

import jax
import jax.numpy as jnp
from jax import lax
from jax.experimental import pallas as pl
from jax.experimental.pallas import tpu as pltpu

EPS = 1e-6
LOG2E = math.log2(math.e)
CHUNK = 64
HGRN_BLOCK = 16
ATTN_TILE = 1024
HGRN_TILE = 2048
HGRN_GROUP = 16
CONV_W = 3
LANES = 128
V7X_VMEM_BYTES = 64 * 1024 * 1024
VMEM_LIMIT = V7X_VMEM_BYTES - 8 * 1024 * 1024

F32 = jnp.float32
BF16 = jnp.bfloat16


def _params(*sem):
    return pltpu.CompilerParams(dimension_semantics=sem, vmem_limit_bytes=VMEM_LIMIT)


def _rows3(a):
    return a.reshape(a.shape[0], 1, a.shape[1])


def _tile(n, pref):
    t = min(n, pref)
    while n % t:
        t //= 2
    return t


def _split_specs(n_p, n_s, bm, bn, col):
    npt, nst = n_p // bm, n_s // bm
    return [pl.BlockSpec((bm, bn), lambda j, i: (jnp.minimum(i, npt - 1), col(j))),
            pl.BlockSpec((bm, bn), lambda j, i: (jnp.clip(i - npt, 0, nst - 1), col(j)))]


def _on_rows(i, npt, split, fn, *ref_pairs):
    if not split:
        fn(*[p[0] for p in ref_pairs])
        return
    pl.when(i < npt)(lambda: fn(*[p[0] for p in ref_pairs]))
    pl.when(i >= npt)(lambda: fn(*[p[-1] for p in ref_pairs]))


def _norm_kernel(*refs, npt):
    *x_refs, g_ref, o_ref = refs

    def run(x_ref):
        x = x_ref[...]
        y = x * lax.rsqrt(jnp.mean(x * x, axis=-1, keepdims=True) + EPS)
        o_ref[...] = (y * g_ref[...]).astype(o_ref.dtype)

    _on_rows(pl.program_id(0), npt, len(x_refs) == 2, run, x_refs)


def _rms_norm_bf16(x, gains, layer):
    xs = x if isinstance(x, tuple) else (x,)
    d = xs[0].shape[1]
    m = sum(a.shape[0] for a in xs)
    bm = _tile(math.gcd(*[a.shape[0] for a in xs]), 512)
    npt = xs[0].shape[0] // bm
    if len(xs) == 2:
        nst = xs[1].shape[0] // bm
        x_specs = [pl.BlockSpec((bm, d), lambda i: (jnp.minimum(i, npt - 1), 0)),
                   pl.BlockSpec((bm, d), lambda i: (jnp.clip(i - npt, 0, nst - 1), 0))]
    else:
        x_specs = [pl.BlockSpec((bm, d), lambda i: (i, 0))]
    return pl.pallas_call(
        functools.partial(_norm_kernel, npt=npt),
        grid=(m // bm,),
        in_specs=x_specs + [pl.BlockSpec((None, 1, d), lambda i: (layer, 0, 0))],
        out_specs=pl.BlockSpec((bm, d), lambda i: (i, 0)),
        out_shape=jax.ShapeDtypeStruct((m, d), BF16),
        compiler_params=_params("parallel"),
        name="rms_norm",
    )(*xs, _rows3(gains))


def _mm_kernel(*refs, n_res, split_in, split_out, npt):
    refs = list(refs)
    h_refs = [refs.pop(0) for _ in range(2 if split_in else 1)]
    w_ref = refs.pop(0)
    r_refs = [refs.pop(0) for _ in range(n_res)] or [None]
    o_refs = [refs.pop(0) for _ in range(2 if split_out else 1)]
    wb_ref, = refs
    i = pl.program_id(1)

    @pl.when(i == 0)
    def _():
        wb_ref[...] = w_ref[...].astype(BF16)

    def run(h_ref, r_ref, o_ref):
        acc = jnp.dot(h_ref[...], wb_ref[...], preferred_element_type=F32)
        o_ref[...] = acc if r_ref is None else r_ref[...] + acc

    _on_rows(i, npt, split_in or split_out or n_res == 2, run, h_refs, r_refs, o_refs)


def _matmul(h, w, layer, n_p, *, res=None, split_out=False, bm=1024, bn=512, w_buffers=2, name="matmul"):
    split_in = isinstance(h, tuple)
    k, ncols = w.shape[1], w.shape[2]
    m = sum(a.shape[0] for a in h) if split_in else h.shape[0]
    n_s = m - n_p
    bm, bn = _tile(math.gcd(n_p, n_s), bm), _tile(ncols, bn)
    npt = n_p // bm
    in_specs = (_split_specs(n_p, n_s, bm, k, lambda j: 0) if split_in
                else [pl.BlockSpec((bm, k), lambda j, i: (i, 0))])
    in_specs.append(pl.BlockSpec((None, k, bn), lambda j, i: (layer, 0, j),
                                 pipeline_mode=pl.Buffered(w_buffers)))
    args = list(h) if split_in else [h]
    args.append(w)
    res = () if res is None else res if isinstance(res, tuple) else (res,)
    in_specs += (_split_specs(n_p, n_s, bm, bn, lambda j: j) if len(res) == 2
                 else [pl.BlockSpec((bm, bn), lambda j, i: (i, j))] * len(res))
    args += res
    if split_out:
        out_specs = _split_specs(n_p, n_s, bm, bn, lambda j: j)
        out_shape = [jax.ShapeDtypeStruct((n_p, ncols), F32), jax.ShapeDtypeStruct((n_s, ncols), F32)]
    else:
        out_specs = pl.BlockSpec((bm, bn), lambda j, i: (i, j))
        out_shape = jax.ShapeDtypeStruct((m, ncols), F32)
    return pl.pallas_call(
        functools.partial(_mm_kernel, n_res=len(res), split_in=split_in, split_out=split_out, npt=npt),
        grid=(ncols // bn, m // bm),
        in_specs=in_specs,
        out_specs=out_specs,
        out_shape=out_shape,
        scratch_shapes=[pltpu.VMEM((k, bn), BF16)],
        compiler_params=_params("parallel", "arbitrary"),
        name=name,
    )(*args)


def _headnorm(acc, g, scale):
    outs = []
    for c in range(acc.shape[1] // LANES):
        blk = acc[:, c * LANES:(c + 1) * LANES]
        y = blk * lax.rsqrt(jnp.mean(blk * blk, axis=-1, keepdims=True) + EPS)
        outs.append(y * g * scale if scale != 1.0 else y * g)
    return jnp.concatenate(outs, axis=1) if len(outs) > 1 else outs[0]


def _qkv_kernel(h_ref, wq_ref, wk_ref, wv_ref, gq_ref, gk_ref,
                qb_ref, kfp_ref, kfs_ref, kb_ref, vfp_ref, vfs_ref, vb_ref,
                wqb_ref, wkb_ref, wvb_ref, *, scale, npt):
    i = pl.program_id(1)

    @pl.when(i == 0)
    def _():
        wqb_ref[...] = wq_ref[...].astype(BF16)
        wkb_ref[...] = wk_ref[...].astype(BF16)
        wvb_ref[...] = wv_ref[...].astype(BF16)

    h = h_ref[...]
    q = jnp.dot(h, wqb_ref[...], preferred_element_type=F32)
    qb_ref[...] = _headnorm(q, gq_ref[...], scale).astype(BF16)
    kn = _headnorm(jnp.dot(h, wkb_ref[...], preferred_element_type=F32), gk_ref[...], 1.0)
    kb_ref[...] = kn.astype(BF16)
    v = jnp.dot(h, wvb_ref[...], preferred_element_type=F32)
    vb_ref[...] = v.astype(BF16)

    def put(kf_ref, vf_ref):
        kf_ref[...] = kn
        vf_ref[...] = v

    _on_rows(i, npt, True, put, (kfp_ref, kfs_ref), (vfp_ref, vfs_ref))


def _diff_project(h, w, layer, q_g, k_g, dh, n_p):
    m, d = h.shape
    n_s = m - n_p
    bm, bn = _tile(math.gcd(n_p, n_s), 1024), _tile(d, 512)
    nj = d // bn
    f_specs = _split_specs(n_p, n_s, bm, bn, lambda j: j)
    f_shapes = [jax.ShapeDtypeStruct((n_p, d), F32), jax.ShapeDtypeStruct((n_s, d), F32)]
    o_spec = pl.BlockSpec((bm, bn), lambda j, i: (i, j))
    b_shape = jax.ShapeDtypeStruct((m, d), BF16)
    g_spec = pl.BlockSpec((None, 1, dh), lambda j, i: (layer, 0, 0))

    def w_spec(seg):
        return pl.BlockSpec((None, d, bn), lambda j, i: (layer, 0, seg * nj + j),
                            pipeline_mode=pl.Buffered(1))

    qb, kf_p, kf_s, kb, vf_p, vf_s, vb = pl.pallas_call(
        functools.partial(_qkv_kernel, scale=dh ** -0.5 * LOG2E, npt=n_p // bm),
        grid=(nj, m // bm),
        in_specs=[pl.BlockSpec((bm, d), lambda j, i: (i, 0)), w_spec(0), w_spec(1), w_spec(2),
                  g_spec, g_spec],
        out_specs=[o_spec] + f_specs + [o_spec] + f_specs + [o_spec],
        out_shape=[b_shape] + f_shapes + [b_shape] + f_shapes + [b_shape],
        scratch_shapes=[pltpu.VMEM((d, bn), BF16)] * 3,
        compiler_params=_params("parallel", "arbitrary"),
        name="diff_qkv",
    )(h, w, w, w, _rows3(q_g), _rows3(k_g))
    return qb, (kf_p, kf_s), kb, (vf_p, vf_s), vb


def _ffn_up_kernel(h_ref, wg_ref, wu_ref, cw_ref, cb_ref, st_ref, a_ref, tail_ref,
                   wgb_ref, wub_ref, g_ref, *, n_prompt_tiles, seq):
    i = pl.program_id(1)
    bm, bn = a_ref.shape
    nseg = bm // seq

    @pl.when(i == 0)
    def _():
        wgb_ref[...] = wg_ref[...].astype(BF16)
        wub_ref[...] = wu_ref[...].astype(BF16)
        g_ref[0:8, :] = jnp.zeros((8, bn), F32)

    h = h_ref[...]
    g = jnp.dot(h, wgb_ref[...], preferred_element_type=F32)
    u = jnp.dot(h, wub_ref[...], preferred_element_type=F32)
    g_ref[8:8 + bm, :] = g
    g1 = g_ref[7:7 + bm, :]
    g2 = g_ref[6:6 + bm, :]
    st = st_ref[...]
    p2 = jnp.broadcast_to(st[:, 0:1, :], (nseg, seq, bn)).reshape(bm, bn)
    p1 = jnp.broadcast_to(st[:, 1:2, :], (nseg, seq, bn)).reshape(bm, bn)
    pos = (lax.broadcasted_iota(jnp.int32, (bm, bn), 0) % seq
           + jnp.where(i >= n_prompt_tiles, 0, seq))
    g1 = jnp.where(pos == 0, p1, g1)
    g2 = jnp.where(pos == 0, p2, jnp.where(pos == 1, p1, g2))
    cw = cw_ref[...]
    conv = cb_ref[...] + cw[2:3, :] * g + cw[1:2, :] * g1 + cw[0:1, :] * g2
    a_ref[...] = (conv * jax.nn.sigmoid(conv) * u).astype(BF16)

    for n in range(nseg):
        end = 8 + (n + 1) * seq
        tail_ref[n] = g_ref[end - (CONV_W - 1):end, :]
    g_ref[0:8, :] = g_ref[bm:bm + 8, :]


def _ffn_up(h, w_up, conv_w, conv_b, conv_state, layer, n_prompt_rows, seq):
    m, d = h.shape
    dff = conv_w.shape[-1]
    nb = conv_state.shape[1]
    bm = _tile(math.gcd(n_prompt_rows, nb * seq), 1024)
    bn = _tile(dff, 512)
    assert bm % seq == 0 and (m - n_prompt_rows) == nb * seq
    nseg = bm // seq
    npt = n_prompt_rows // bm
    nj = dff // bn
    return pl.pallas_call(
        functools.partial(_ffn_up_kernel, n_prompt_tiles=npt, seq=seq),
        grid=(nj, m // bm),
        in_specs=[pl.BlockSpec((bm, d), lambda j, i: (i, 0)),
                  pl.BlockSpec((None, d, bn), lambda j, i: (layer, 0, j)),
                  pl.BlockSpec((None, d, bn), lambda j, i: (layer, 0, nj + j)),
                  pl.BlockSpec((None, CONV_W, bn), lambda j, i: (layer, 0, j)),
                  pl.BlockSpec((None, 1, bn), lambda j, i: (layer, 0, j)),
                  pl.BlockSpec((None, nseg, CONV_W - 1, bn),
                               lambda j, i: (layer, jnp.maximum(i - npt, 0), 0, j))],
        out_specs=[pl.BlockSpec((bm, bn), lambda j, i: (i, j)),
                   pl.BlockSpec((nseg, CONV_W - 1, bn), lambda j, i: (i, 0, j))],
        out_shape=[jax.ShapeDtypeStruct((m, dff), BF16),
                   jax.ShapeDtypeStruct((m // seq, CONV_W - 1, dff), F32)],
        scratch_shapes=[pltpu.VMEM((d, bn), BF16), pltpu.VMEM((d, bn), BF16),
                        pltpu.VMEM((bm + 8, bn), F32)],
        compiler_params=_params("parallel", "arbitrary"),
        name="ffn_up",
    )(h, w_up, w_up, conv_w, _rows3(conv_b), conv_state)


def _hgrn_in_kernel(h_ref, wq_ref, wf_ref, wv_ref, wg_ref, lbp_ref,
                    q_ref, b2_ref, k_ref, qe_ref, kd_ref, v_ref, gate_ref,
                    wqb_ref, wfb_ref, wvb_ref, wgb_ref, *, layer):
    @pl.when(pl.program_id(1) == 0)
    def _():
        wqb_ref[...] = wq_ref[...].astype(BF16)
        wfb_ref[...] = wf_ref[...].astype(BF16)
        wvb_ref[...] = wv_ref[...].astype(BF16)
        wgb_ref[...] = wg_ref[...].astype(BF16)

    h = h_ref[...]
    rows, width = q_ref.shape
    q = jnp.dot(h, wqb_ref[...], preferred_element_type=F32)
    fz = jnp.dot(h, wfb_ref[...], preferred_element_type=F32)
    v_ref[...] = jnp.dot(h, wvb_ref[...], preferred_element_type=F32).astype(BF16)
    gate_ref[...] = jnp.dot(h, wgb_ref[...], preferred_element_type=F32)

    lbp = lbp_ref[...]
    e = jnp.exp(lbp - jnp.max(lbp, axis=0, keepdims=True))
    lb = jnp.sum(e[0:layer + 1], axis=0, keepdims=True) / jnp.sum(e, axis=0, keepdims=True)

    f = lb + (1.0 - lb) * jax.nn.sigmoid(fz)
    kk = 1.0 - f
    pos = lax.broadcasted_iota(jnp.int32, (rows, width), 0) % HGRN_BLOCK
    b = jnp.log(f)
    sh = 1
    while sh < HGRN_BLOCK:
        b = b + jnp.where(pos >= sh, pltpu.roll(b, sh, 0), 0.0)
        sh *= 2
    b3 = b.reshape(rows // HGRN_BLOCK, HGRN_BLOCK, width)
    b_last = jnp.broadcast_to(b3[:, HGRN_BLOCK - 1:, :], b3.shape).reshape(rows, width)
    q_ref[...] = q
    b2_ref[...] = b * LOG2E
    k_ref[...] = kk
    qe_ref[...] = (q * jnp.exp(b)).astype(BF16)
    kd_ref[...] = (kk * jnp.exp(b_last - b)).astype(BF16)


def _hgrn_project(h, w, lower_bounds, layer, jl):
    m, d = h.shape
    dk = w.shape[2] // 4
    bm, bn = _tile(m, 1024), _tile(dk, 256)
    nj = dk // bn

    def w_spec(seg):
        return pl.BlockSpec((None, d, bn), lambda j, i: (jl, 0, seg * nj + j))

    o_spec = pl.BlockSpec((bm, bn), lambda j, i: (i, j))
    f32_out, bf16_out = jax.ShapeDtypeStruct((m, dk), F32), jax.ShapeDtypeStruct((m, dk), BF16)
    return pl.pallas_call(
        functools.partial(_hgrn_in_kernel, layer=layer),
        grid=(nj, m // bm),
        in_specs=[pl.BlockSpec((bm, d), lambda j, i: (i, 0)), w_spec(0), w_spec(1), w_spec(2), w_spec(3),
                  pl.BlockSpec((lower_bounds.shape[0], bn), lambda j, i: (0, j))],
        out_specs=[o_spec] * 7,
        out_shape=[f32_out, f32_out, f32_out, bf16_out, bf16_out, bf16_out, f32_out],
        scratch_shapes=[pltpu.VMEM((d, bn), BF16)] * 4,
        compiler_params=_params("parallel", "arbitrary"),
        name="hgrn_in",
    )(h, w, w, w, w, lower_bounds)


def _hgrn_kernel(q_ref, b_ref, k_ref, qe_ref, kd_ref, v_ref, gate_ref, gout_ref, s0_ref, o_ref, sout_ref,
                 st_ref, oacc_ref, *, seq_blocks):
    c = pl.program_id(1)
    t_rows = q_ref.shape[0]
    nblk = t_rows // HGRN_BLOCK
    carry = seq_blocks is None

    if carry:
        @pl.when(c == 0)
        def _():
            st_ref[...] = jnp.zeros_like(st_ref)

    half = HGRN_BLOCK // 2
    row = lax.broadcasted_iota(jnp.int32, (half, LANES), 0)
    lane = lax.broadcasted_iota(jnp.int32, (half, LANES), 1)

    def scores(r0):
        lo, hi = pl.ds(r0, half), pl.ds(r0 + half, half)
        b_lo, b_hi, q_lo, q_hi = b_ref[lo, :], b_ref[hi, :], q_ref[lo, :], q_ref[hi, :]
        sc_lo = jnp.zeros((half, LANES), F32)
        sc_hi = jnp.zeros((half, LANES), F32)
        for s in range(HGRN_BLOCK):
            bs, ks = b_ref[pl.ds(r0 + s, 1), :], k_ref[pl.ds(r0 + s, 1), :]
            col_hi = jnp.sum(jnp.exp2(b_hi - bs) * (q_hi * ks), axis=-1, keepdims=True)
            sc_hi = jnp.where(lane == s, col_hi, sc_hi)
            if s < half:
                col_lo = jnp.sum(jnp.exp2(b_lo - bs) * (q_lo * ks), axis=-1, keepdims=True)
                sc_lo = jnp.where(lane == s, col_lo, sc_lo)
        sc = jnp.concatenate([jnp.where(row >= lane, sc_lo, 0.0),
                              jnp.where(row + half >= lane, sc_hi, 0.0)], axis=0)
        return sc[:, 0:HGRN_BLOCK].astype(BF16)

    group = min(nblk, HGRN_GROUP) if carry else nblk

    def blocks(jg, st):
        rows, sc, vb, decay, upd = [], [], [], [], []
        for g in range(group):
            r0 = pl.multiple_of((jg * group + g) * HGRN_BLOCK, HGRN_BLOCK)
            rows.append(pl.ds(r0, HGRN_BLOCK))
            vb.append(v_ref[rows[g], :])
            sc.append(scores(r0))
            decay.append(jnp.exp2(b_ref[pl.ds(r0 + HGRN_BLOCK - 1, 1), :]))
            upd.append(lax.dot_general(vb[g], kd_ref[rows[g], :], (((0,), (0,)), ((), ())),
                                       preferred_element_type=F32))
        states = []
        for g in range(group):
            if not carry and g % seq_blocks == 0:
                st = s0_ref[g // seq_blocks].T
            states.append(st.astype(BF16))
            st = st * decay[g] + upd[g]
            if not carry and (g + 1) % seq_blocks == 0:
                sout_ref[g // seq_blocks] = st.T
        for g in range(group):
            o = lax.dot_general(qe_ref[rows[g], :], states[g], (((1,), (1,)), ((), ())),
                                preferred_element_type=F32)
            oacc_ref[rows[g], :] = o + jnp.dot(sc[g], vb[g], preferred_element_type=F32)
        return st

    if carry:
        st_ref[...] = lax.fori_loop(0, nblk // group, blocks, st_ref[...])
    else:
        blocks(0, None)

    o = oacc_ref[...]
    y = o * lax.rsqrt(jnp.mean(o * o, axis=-1, keepdims=True) + EPS) * gout_ref[...]
    gate = gate_ref[...]
    o_ref[...] = (y * (gate * jax.nn.sigmoid(gate))).astype(BF16)

    if carry:
        @pl.when(c == pl.num_programs(1) - 1)
        def _():
            sout_ref[...] = st_ref[...].T


def _hgrn_scan(ops, out_norm, state, jl, n_prompt_rows, seq):
    nb, nh, dk, dv = state.shape[1:]
    assert dk == LANES and dv == LANES
    scratch = lambda t: [pltpu.VMEM((dv, dk), F32), pltpu.VMEM((t, dv), F32)]

    def specs(t, row_of):
        return ([pl.BlockSpec((t, LANES), lambda h, c: (row_of(c), h))] * len(ops)
                + [pl.BlockSpec((None, 1, dv), lambda h, c: (jl, 0, 0))])

    out_norm = _rows3(out_norm)
    tp = _tile(n_prompt_rows, HGRN_TILE)
    o_p, s_p = pl.pallas_call(
        functools.partial(_hgrn_kernel, seq_blocks=None),
        grid=(nh, n_prompt_rows // tp),
        in_specs=specs(tp, lambda c: c) + [pl.BlockSpec((None, None, None, dk, dv),
                                                        lambda h, c: (jl, 0, h, 0, 0))],
        out_specs=[pl.BlockSpec((tp, dv), lambda h, c: (c, h)),
                   pl.BlockSpec((None, dk, dv), lambda h, c: (h, 0, 0))],
        out_shape=[jax.ShapeDtypeStruct((n_prompt_rows, nh * dv), BF16),
                   jax.ShapeDtypeStruct((nh, dk, dv), F32)],
        scratch_shapes=scratch(tp),
        compiler_params=_params("parallel", "arbitrary"),
        name="hgrn_prompt",
    )(*ops, out_norm, state)
    ns = _tile(nb, max(1, HGRN_GROUP * HGRN_BLOCK // seq))
    ts = ns * seq
    assert n_prompt_rows % ts == 0 and seq % HGRN_BLOCK == 0
    r0 = n_prompt_rows // ts
    o_s, s_s = pl.pallas_call(
        functools.partial(_hgrn_kernel, seq_blocks=seq // HGRN_BLOCK),
        grid=(nh, nb // ns),
        in_specs=specs(ts, lambda c: r0 + c) + [pl.BlockSpec((None, ns, None, dk, dv),
                                                              lambda h, c: (jl, c, h, 0, 0))],
        out_specs=[pl.BlockSpec((ts, dv), lambda h, c: (c, h)),
                   pl.BlockSpec((ns, None, dk, dv), lambda h, c: (c, h, 0, 0))],
        out_shape=[jax.ShapeDtypeStruct((nb * seq, nh * dv), BF16),
                   jax.ShapeDtypeStruct((nb, nh, dk, dv), F32)],
        scratch_shapes=scratch(ts),
        compiler_params=_params("parallel", "arbitrary"),
        name="hgrn_sample",
    )(*ops, out_norm, state)
    return (o_p, o_s), s_p, s_s


def _lanes(x, n):
    return x[:, :n] if n <= LANES else jnp.concatenate([x] * (n // LANES), axis=1)


def _softmax_step(c, q, k, v, m_ref, l_ref, acc_ref, mask, rows=slice(None)):
    s = lax.dot_general(q, k, (((1,), (1,)), ((), ())), preferred_element_type=F32)
    if mask is not None:
        s = jnp.where(mask, s, -jnp.inf)
    m_prev = m_ref[c, rows]
    m_new = jnp.maximum(m_prev, jnp.max(s, axis=-1, keepdims=True))
    alpha = jnp.exp2(m_prev - m_new)
    p = jnp.exp2(s - _lanes(m_new, s.shape[1]))
    l_ref[c, rows] = alpha * l_ref[c, rows] + jnp.sum(p, axis=-1, keepdims=True)
    acc_ref[c, rows] = (_lanes(alpha, v.shape[1]) * acc_ref[c, rows]
                        + jnp.dot(p.astype(BF16), v, preferred_element_type=F32))
    m_ref[c, rows] = m_new


def _diff_finish(a0, l0, a1, l1, lam_refs, subln, lam_init):
    lq1, lk1, lq2, lk2 = [r[...] for r in lam_refs]
    lam = (jnp.exp(jnp.sum(lq1 * lk1, axis=-1, keepdims=True))
           - jnp.exp(jnp.sum(lq2 * lk2, axis=-1, keepdims=True)) + lam_init)
    o = a0 * _lanes(1.0 / l0, a0.shape[1]) - lam * (a1 * _lanes(1.0 / l1, a1.shape[1]))
    y = o * lax.rsqrt(jnp.mean(o * o, axis=-1, keepdims=True) + EPS)
    return (y * subln * (1.0 - lam_init)).astype(BF16)


def _attn_prompt_kernel(it_ref, jt_ref, q_ref, k_ref, v_ref, lq1, lk1, lq2, lk2, sub_ref, o_ref,
                        m_ref, l_ref, acc_ref, *, lam_init):
    p = pl.program_id(1)
    i, j = it_ref[p], jt_ref[p]
    bq, bk = q_ref.shape[0], k_ref.shape[0]
    dh = q_ref.shape[1] // 2

    @pl.when(j == 0)
    def _():
        m_ref[...] = jnp.full(m_ref.shape, -jnp.inf, F32)
        l_ref[...] = jnp.zeros(l_ref.shape, F32)
        acc_ref[...] = jnp.zeros(acc_ref.shape, F32)

    def step(mask, rows=slice(None), keys=slice(None)):
        v = v_ref[keys, :]
        for c in range(2):
            _softmax_step(c, q_ref[rows, c * dh:(c + 1) * dh], k_ref[keys, c * dh:(c + 1) * dh], v,
                          m_ref, l_ref, acc_ref, mask, rows)

    @pl.when(j < i)
    def _():
        step(None)

    @pl.when(j == i)
    def _():
        half = bq // 2
        qpos = lax.broadcasted_iota(jnp.int32, (half, half), 0)
        kpos = lax.broadcasted_iota(jnp.int32, (half, half), 1)
        chunk_mask = kpos < (qpos // CHUNK + 1) * CHUNK
        top, bottom = slice(0, half), slice(half, bq)
        step(chunk_mask, top, top)
        step(None, bottom, top)
        step(chunk_mask, bottom, bottom)
        o_ref[...] = _diff_finish(acc_ref[0], l_ref[0], acc_ref[1], l_ref[1],
                                  (lq1, lk1, lq2, lk2), sub_ref[...], lam_init)


def _attn_prompt(qb, kb, vb, lams, subln, jl, lam_init, n_rows, nh):
    dv = qb.shape[1] // nh
    bq = bk = _tile(n_rows, ATTN_TILE)
    assert (bq // 2) % CHUNK == 0
    nq = n_rows // bq
    pairs = [(i, j) for i in range(nq) for j in range(i + 1)]
    it = jnp.asarray([p[0] for p in pairs], jnp.int32)
    jt = jnp.asarray([p[1] for p in pairs], jnp.int32)
    lam_spec = pl.BlockSpec((None, 1, dv // 2), lambda h, p, it, jt: (jl, 0, 0))
    return pl.pallas_call(
        functools.partial(_attn_prompt_kernel, lam_init=lam_init),
        grid_spec=pltpu.PrefetchScalarGridSpec(
            num_scalar_prefetch=2,
            grid=(nh, len(pairs)),
            in_specs=[pl.BlockSpec((bq, dv), lambda h, p, it, jt: (it[p], h)),
                      pl.BlockSpec((bk, dv), lambda h, p, it, jt: (jt[p], h)),
                      pl.BlockSpec((bk, dv), lambda h, p, it, jt: (jt[p], h)),
                      lam_spec, lam_spec, lam_spec, lam_spec,
                      pl.BlockSpec((None, 1, dv), lambda h, p, it, jt: (jl, 0, 0))],
            out_specs=pl.BlockSpec((bq, dv), lambda h, p, it, jt: (it[p], h)),
            scratch_shapes=[pltpu.VMEM((2, bq, LANES), F32), pltpu.VMEM((2, bq, LANES), F32),
                            pltpu.VMEM((2, bq, dv), F32)]),
        out_shape=jax.ShapeDtypeStruct((n_rows, nh * dv), BF16),
        compiler_params=_params("parallel", "arbitrary"),
        name="attn_prompt",
    )(it, jt, qb, kb, vb, *[_rows3(a) for a in lams], _rows3(subln))


def _attn_sample_kernel(q_ref, ck_ref, cv_ref, kn_ref, vn_ref, lq1, lk1, lq2, lk2, sub_ref, o_ref,
                        m_ref, l_ref, acc_ref, *, lam_init, nh):
    j = pl.program_id(1)
    last = pl.num_programs(1) - 1
    dv = q_ref.shape[1] // nh
    dh = dv // 2

    @pl.when(j == 0)
    def _():
        m_ref[...] = jnp.full(m_ref.shape, -jnp.inf, F32)
        l_ref[...] = jnp.zeros(l_ref.shape, F32)
        acc_ref[...] = jnp.zeros(acc_ref.shape, F32)

    def step(k_of, v_of):
        for h in range(nh):
            v = v_of(h)
            for c in range(2):
                lo = h * dv + c * dh
                _softmax_step(2 * h + c, q_ref[:, lo:lo + dh], k_of(h, c), v, m_ref, l_ref, acc_ref, None)

    @pl.when(j < last)
    def _():
        bk = ck_ref.shape[0] // (2 * nh)

        def rows(ref, first):
            return ref[pl.ds(first, bk, stride=2 * nh), :].astype(BF16)

        step(lambda h, c: rows(ck_ref, 2 * h + c),
             lambda h: jnp.concatenate([rows(cv_ref, h), rows(cv_ref, nh + h)], axis=1))

    @pl.when(j == last)
    def _():
        step(lambda h, c: kn_ref[:, h * dv + c * dh:h * dv + (c + 1) * dh],
             lambda h: vn_ref[:, h * dv:(h + 1) * dv])
        for h in range(nh):
            o_ref[:, h * dv:(h + 1) * dv] = _diff_finish(
                acc_ref[2 * h], l_ref[2 * h], acc_ref[2 * h + 1], l_ref[2 * h + 1],
                (lq1, lk1, lq2, lk2), sub_ref[...], lam_init)


def _attn_sample(qb, kb, vb, cache_k, cache_v, lams, subln, jl, lam_init, n_prompt_rows, seq, nh):
    d = qb.shape[1]
    dv = d // nh
    nb, past = cache_k.shape[1], cache_k.shape[2]
    bk = _tile(past, 512)
    nkc = past // bk
    r0 = n_prompt_rows // seq
    new_spec = pl.BlockSpec((seq, d), lambda b, j: (r0 + b, 0))
    nl = cache_k.shape[0]
    ck = cache_k.reshape(nl, nb, past * nh * 2, dv // 2)
    cv = cache_v.reshape(nl, nb, past, nh, 2, dv // 2).transpose(0, 1, 2, 4, 3, 5).reshape(ck.shape)
    cache_spec = pl.BlockSpec((None, None, bk * nh * 2, dv // 2),
                              lambda b, j: (jl, b, jnp.minimum(j, nkc - 1), 0))
    lam_spec = pl.BlockSpec((None, 1, dv // 2), lambda b, j: (jl, 0, 0))
    return pl.pallas_call(
        functools.partial(_attn_sample_kernel, lam_init=lam_init, nh=nh),
        grid=(nb, nkc + 1),
        in_specs=[new_spec, cache_spec, cache_spec, new_spec, new_spec,
                  lam_spec, lam_spec, lam_spec, lam_spec,
                  pl.BlockSpec((None, 1, dv), lambda b, j: (jl, 0, 0))],
        out_specs=pl.BlockSpec((seq, d), lambda b, j: (b, 0)),
        out_shape=jax.ShapeDtypeStruct((nb * seq, d), BF16),
        scratch_shapes=[pltpu.VMEM((2 * nh, seq, LANES), F32), pltpu.VMEM((2 * nh, seq, LANES), F32),
                        pltpu.VMEM((2 * nh, seq, dv), F32)],
        compiler_params=_params("parallel", "arbitrary"),
        name="attn_sample",
    )(qb, ck, cv, kb, vb, *[_rows3(a) for a in lams], _rows3(subln))


def kernel(x_prompt, x_sample, state_hgrn, cache_k, cache_v, state_ffn_conv, norm_mix, norm_ffn, hgrn_lower_bounds, w_hgrn_in, w_hgrn_out, hgrn_out_norm, w_diff_in, w_diff_out, diff_q_norm, diff_k_norm, diff_lambda_q1, diff_lambda_k1, diff_lambda_q2, diff_lambda_k2, diff_subln, w_ffn_up, ffn_conv_w, ffn_conv_b, w_ffn_down):
    bp, seq_p, d = x_prompt.shape
    nb, seq_s, _ = x_sample.shape
    assert bp == 1 and seq_s == CHUNK
    depth = norm_mix.shape[0]
    n_mixers = 2
    n_p = bp * seq_p
    diff_heads, dh = cache_k.shape[3], cache_k.shape[5]
    x = (x_prompt.reshape(n_p, d), x_sample.reshape(nb * seq_s, d))

    hgrn_p, hgrn_s, kfs, vfs, tails = [], [], [], [], []
    for i in range(depth):
        jl = i // n_mixers
        h = _rms_norm_bf16(x, norm_mix, i)
        if i % n_mixers == 0:
            ops = _hgrn_project(h, w_hgrn_in, hgrn_lower_bounds, i, jl)
            o, s_p, s_s = _hgrn_scan(ops, hgrn_out_norm, state_hgrn, jl, n_p, seq_s)
            hgrn_p.append(s_p[None])
            hgrn_s.append(s_s)
            x = _matmul(o, w_hgrn_out, jl, n_p, res=x, name="hgrn_out")
        else:
            lam_init = 0.8 - 0.6 * math.exp(-0.3 * i)
            lams = (diff_lambda_q1, diff_lambda_k1, diff_lambda_q2, diff_lambda_k2)
            qb, kf, kb, vf, vb = _diff_project(h, w_diff_in, jl, diff_q_norm, diff_k_norm, dh, n_p)
            o_p = _attn_prompt(qb, kb, vb, lams, diff_subln, jl, lam_init, n_p, diff_heads)
            o_s = _attn_sample(qb, kb, vb, cache_k, cache_v, lams, diff_subln, jl, lam_init,
                               n_p, seq_s, diff_heads)
            kfs.append(kf)
            vfs.append(vf)
            x = _matmul((o_p, o_s), w_diff_out, jl, n_p, res=x, name="diff_out")
        h = _rms_norm_bf16(x, norm_ffn, i)
        act, tail = _ffn_up(h, w_ffn_up, ffn_conv_w, ffn_conv_b, state_ffn_conv, i, n_p, seq_s)
        tails.append(tail)
        x = _matmul(act, w_ffn_down, i, n_p, res=x, split_out=i == depth - 1, bm=512, bn=1024,
                    w_buffers=1, name="ffn_down")

    x_p, x_s = x
    tail = jnp.stack(tails)
    seg_p = n_p // seq_s
    stack = lambda pairs, k: jnp.stack([p[k] for p in pairs])
    return (x_p.reshape(bp, seq_p, d),
            x_s.reshape(nb, seq_s, d),
            jnp.stack(hgrn_p),
            jnp.stack(hgrn_s),
            stack(kfs, 0).reshape(-1, bp, seq_p, diff_heads, 2, dh),
            stack(vfs, 0).reshape(-1, bp, seq_p, diff_heads, 2 * dh),
            stack(kfs, 1).reshape(-1, nb, seq_s, diff_heads, 2, dh),
            stack(vfs, 1).reshape(-1, nb, seq_s, diff_heads, 2 * dh),
            tail[:, seg_p - 1][:, None],
            tail[:, seg_p:])
```

```python
import functools
import math

import jax
import jax.numpy as jnp
from jax import lax
from jax.experimental import pallas as pl
from jax.experimental.pallas import tpu as pltpu

EPS = 1e-6
LOG2E = math.log2(math.e)
CHUNK = 64
HGRN_BLOCK = 16
ATTN_TILE = 1024
HGRN_TILE = 2048
HGRN_GROUP = 16
CONV_W = 3
LANES = 128
V7X_VMEM_BYTES = 64 * 1024 * 1024
VMEM_LIMIT = V7X_VMEM_BYTES - 8 * 1024 * 1024

F32 = jnp.float32
BF16 = jnp.bfloat16


def _params(*sem):
    return pltpu.CompilerParams(dimension_semantics=sem, vmem_limit_bytes=VMEM_LIMIT)


def _rows3(a):
    return a.reshape(a.shape[0], 1, a.shape[1])


def _tile(n, pref):
    t = min(n, pref)
    while n % t:
        t //= 2
    return t


def _split_specs(n_p, n_s, bm, bn, col):
    npt, nst = n_p // bm, n_s // bm
    return [pl.BlockSpec((bm, bn), lambda j, i: (jnp.minimum(i, npt - 1), col(j))),
            pl.BlockSpec((bm, bn), lambda j, i: (jnp.clip(i - npt, 0, nst - 1), col(j)))]


def _on_rows(i, npt, split, fn, *ref_pairs):
    if not split:
        fn(*[p[0] for p in ref_pairs])
        return
    pl.when(i < npt)(lambda: fn(*[p[0] for p in ref_pairs]))
    pl.when(i >= npt)(lambda: fn(*[p[-1] for p in ref_pairs]))


def _norm_kernel(*refs, npt):
    *x_refs, g_ref, o_ref = refs

    def run(x_ref):
        x = x_ref[...]
        y = x * lax.rsqrt(jnp.mean(x * x, axis=-1, keepdims=True) + EPS)
        o_ref[...] = (y * g_ref[...]).astype(o_ref.dtype)

    _on_rows(pl.program_id(0), npt, len(x_refs) == 2, run, x_refs)


def _rms_norm_bf16(x, gains, layer):
    xs = x if isinstance(x, tuple) else (x,)
    d = xs[0].shape[1]
    m = sum(a.shape[0] for a in xs)
    bm = _tile(math.gcd(*[a.shape[0] for a in xs]), 512)
    npt = xs[0].shape[0] // bm
    if len(xs) == 2:
        nst = xs[1].shape[0] // bm
        x_specs = [pl.BlockSpec((bm, d), lambda i: (jnp.minimum(i, npt - 1), 0)),
                   pl.BlockSpec((bm, d), lambda i: (jnp.clip(i - npt, 0, nst - 1), 0))]
    else:
        x_specs = [pl.BlockSpec((bm, d), lambda i: (i, 0))]
    return pl.pallas_call(
        functools.partial(_norm_kernel, npt=npt),
        grid=(m // bm,),
        in_specs=x_specs + [pl.BlockSpec((None, 1, d), lambda i: (layer, 0, 0))],
        out_specs=pl.BlockSpec((bm, d), lambda i: (i, 0)),
        out_shape=jax.ShapeDtypeStruct((m, d), BF16),
        compiler_params=_params("parallel"),
        name="rms_norm",
    )(*xs, _rows3(gains))


def _mm_kernel(*refs, n_res, split_in, split_out, npt):
    refs = list(refs)
    h_refs = [refs.pop(0) for _ in range(2 if split_in else 1)]
    w_ref = refs.pop(0)
    r_refs = [refs.pop(0) for _ in range(n_res)] or [None]
    o_refs = [refs.pop(0) for _ in range(2 if split_out else 1)]
    wb_ref, = refs
    i = pl.program_id(1)

    @pl.when(i == 0)
    def _():
        wb_ref[...] = w_ref[...].astype(BF16)

    def run(h_ref, r_ref, o_ref):
        acc = jnp.dot(h_ref[...], wb_ref[...], preferred_element_type=F32)
        o_ref[...] = acc if r_ref is None else r_ref[...] + acc

    _on_rows(i, npt, split_in or split_out or n_res == 2, run, h_refs, r_refs, o_refs)


def _matmul(h, w, layer, n_p, *, res=None, split_out=False, bm=1024, bn=512, w_buffers=2, name="matmul"):
    split_in = isinstance(h, tuple)
    k, ncols = w.shape[1], w.shape[2]
    m = sum(a.shape[0] for a in h) if split_in else h.shape[0]
    n_s = m - n_p
    bm, bn = _tile(math.gcd(n_p, n_s), bm), _tile(ncols, bn)
    npt = n_p // bm
    in_specs = (_split_specs(n_p, n_s, bm, k, lambda j: 0) if split_in
                else [pl.BlockSpec((bm, k), lambda j, i: (i, 0))])
    in_specs.append(pl.BlockSpec((None, k, bn), lambda j, i: (layer, 0, j),
                                 pipeline_mode=pl.Buffered(w_buffers)))
    args = list(h) if split_in else [h]
    args.append(w)
    res = () if res is None else res if isinstance(res, tuple) else (res,)
    in_specs += (_split_specs(n_p, n_s, bm, bn, lambda j: j) if len(res) == 2
                 else [pl.BlockSpec((bm, bn), lambda j, i: (i, j))] * len(res))
    args += res
    if split_out:
        out_specs = _split_specs(n_p, n_s, bm, bn, lambda j: j)
        out_shape = [jax.ShapeDtypeStruct((n_p, ncols), F32), jax.ShapeDtypeStruct((n_s, ncols), F32)]
    else:
        out_specs = pl.BlockSpec((bm, bn), lambda j, i: (i, j))
        out_shape = jax.ShapeDtypeStruct((m, ncols), F32)
    return pl.pallas_call(
        functools.partial(_mm_kernel, n_res=len(res), split_in=split_in, split_out=split_out, npt=npt),
        grid=(ncols // bn, m // bm),
        in_specs=in_specs,
        out_specs=out_specs,
        out_shape=out_shape,
        scratch_shapes=[pltpu.VMEM((k, bn), BF16)],
        compiler_params=_params("parallel", "arbitrary"),
        name=name,
    )(*args)


def _headnorm(acc, g, scale):
    outs = []
    for c in range(acc.shape[1] // LANES):
        blk = acc[:, c * LANES:(c + 1) * LANES]
        y = blk * lax.rsqrt(jnp.mean(blk * blk, axis=-1, keepdims=True) + EPS)
        outs.append(y * g * scale if scale != 1.0 else y * g)
    return jnp.concatenate(outs, axis=1) if len(outs) > 1 else outs[0]


def _qkv_kernel(h_ref, wq_ref, wk_ref, wv_ref, gq_ref, gk_ref,
                qb_ref, kfp_ref, kfs_ref, kb_ref, vfp_ref, vfs_ref, vb_ref,
                wqb_ref, wkb_ref, wvb_ref, *, scale, npt):
    i = pl.program_id(1)

    @pl.when(i == 0)
    def _():
        wqb_ref[...] = wq_ref[...].astype(BF16)
        wkb_ref[...] = wk_ref[...].astype(BF16)
        wvb_ref[...] = wv_ref[...].astype(BF16)

    h = h_ref[...]
    q = jnp.dot(h, wqb_ref[...], preferred_element_type=F32)
    k = jnp.dot(h, wkb_ref[...], preferred_element_type=F32)
    v = jnp.dot(h, wvb_ref[...], preferred_element_type=F32)
    qb_ref[...] = _headnorm(q, gq_ref[...], scale).astype(BF16)
    kn = _headnorm(k, gk_ref[...], 1.0)
    kb_ref[...] = kn.astype(BF16)
    vb_ref[...] = v.astype(BF16)

    def put(kf_ref, vf_ref):
        kf_ref[...] = kn
        vf_ref[...] = v

    _on_rows(i, npt, True, put, (kfp_ref, kfs_ref), (vfp_ref, vfs_ref))


def _diff_project(h, w, layer, q_g, k_g, dh, n_p):
    m, d = h.shape
    n_s = m - n_p
    bm, bn = _tile(math.gcd(n_p, n_s), 1024), _tile(d, 512)
    nj = d // bn
    f_specs = _split_specs(n_p, n_s, bm, bn, lambda j: j)
    f_shapes = [jax.ShapeDtypeStruct((n_p, d), F32), jax.ShapeDtypeStruct((n_s, d), F32)]
    o_spec = pl.BlockSpec((bm, bn), lambda j, i: (i, j))
    b_shape = jax.ShapeDtypeStruct((m, d), BF16)
    g_spec = pl.BlockSpec((None, 1, dh), lambda j, i: (layer, 0, 0))

    def w_spec(seg):
        return pl.BlockSpec((None, d, bn), lambda j, i: (layer, 0, seg * nj + j),
                            pipeline_mode=pl.Buffered(1))

    qb, kf_p, kf_s, kb, vf_p, vf_s, vb = pl.pallas_call(
        functools.partial(_qkv_kernel, scale=dh ** -0.5 * LOG2E, npt=n_p // bm),
        grid=(nj, m // bm),
        in_specs=[pl.BlockSpec((bm, d), lambda j, i: (i, 0)), w_spec(0), w_spec(1), w_spec(2),
                  g_spec, g_spec],
        out_specs=[o_spec] + f_specs + [o_spec] + f_specs + [o_spec],
        out_shape=[b_shape] + f_shapes + [b_shape] + f_shapes + [b_shape],
        scratch_shapes=[pltpu.VMEM((d, bn), BF16)] * 3,
        compiler_params=_params("parallel", "arbitrary"),
        name="diff_qkv",
    )(h, w, w, w, _rows3(q_g), _rows3(k_g))
    return qb, (kf_p, kf_s), kb, (vf_p, vf_s), vb


def _ffn_up_kernel(h_ref, wg_ref, wu_ref, cw_ref, cb_ref, st_ref, a_ref, tail_ref,
                   wgb_ref, wub_ref, g_ref, *, n_prompt_tiles, seq):
    i = pl.program_id(1)
    bm, bn = a_ref.shape
    nseg = bm // seq

    @pl.when(i == 0)
    def _():
        wgb_ref[...] = wg_ref[...].astype(BF16)
        wub_ref[...] = wu_ref[...].astype(BF16)
        g_ref[0:8, :] = jnp.zeros((8, bn), F32)

    h = h_ref[...]
    g = jnp.dot(h, wgb_ref[...], preferred_element_type=F32)
    u = jnp.dot(h, wub_ref[...], preferred_element_type=F32)
    g_ref[8:8 + bm, :] = g
    g1 = g_ref[7:7 + bm, :]
    g2 = g_ref[6:6 + bm, :]
    st = st_ref[...]
    p2 = jnp.broadcast_to(st[:, 0:1, :], (nseg, seq, bn)).reshape(bm, bn)
    p1 = jnp.broadcast_to(st[:, 1:2, :], (nseg, seq, bn)).reshape(bm, bn)
    pos = (lax.broadcasted_iota(jnp.int32, (bm, bn), 0) % seq
           + jnp.where(i >= n_prompt_tiles, 0, seq))
    g1 = jnp.where(pos == 0, p1, g1)
    g2 = jnp.where(pos == 0, p2, jnp.where(pos == 1, p1, g2))
    cw = cw_ref[...]
    conv = cb_ref[...] + cw[2:3, :] * g + cw[1:2, :] * g1 + cw[0:1, :] * g2
    a_ref[...] = (conv * jax.nn.sigmoid(conv) * u).astype(BF16)

    for n in range(nseg):
        end = 8 + (n + 1) * seq
        tail_ref[n] = g_ref[end - (CONV_W - 1):end, :]
    g_ref[0:8, :] = g_ref[bm:bm + 8, :]


def _ffn_up(h, w_up, conv_w, conv_b, conv_state, layer, n_prompt_rows, seq):
    m, d = h.shape
    dff = conv_w.shape[-1]
    nb = conv_state.shape[1]
    bm = _tile(math.gcd(n_prompt_rows, nb * seq), 1024)
    bn = _tile(dff, 512)
    assert bm % seq == 0 and (m - n_prompt_rows) == nb * seq
    nseg = bm // seq
    npt = n_prompt_rows // bm
    nj = dff // bn
    return pl.pallas_call(
        functools.partial(_ffn_up_kernel, n_prompt_tiles=npt, seq=seq),
        grid=(nj, m // bm),
        in_specs=[pl.BlockSpec((bm, d), lambda j, i: (i, 0)),
                  pl.BlockSpec((None, d, bn), lambda j, i: (layer, 0, j)),
                  pl.BlockSpec((None, d, bn), lambda j, i: (layer, 0, nj + j)),
                  pl.BlockSpec((None, CONV_W, bn), lambda j, i: (layer, 0, j)),
                  pl.BlockSpec((None, 1, bn), lambda j, i: (layer, 0, j)),
                  pl.BlockSpec((None, nseg, CONV_W - 1, bn),
                               lambda j, i: (layer, jnp.maximum(i - npt, 0), 0, j))],
        out_specs=[pl.BlockSpec((bm, bn), lambda j, i: (i, j)),
                   pl.BlockSpec((nseg, CONV_W - 1, bn), lambda j, i: (i, 0, j))],
        out_shape=[jax.ShapeDtypeStruct((m, dff), BF16),
                   jax.ShapeDtypeStruct((m // seq, CONV_W - 1, dff), F32)],
        scratch_shapes=[pltpu.VMEM((d, bn), BF16), pltpu.VMEM((d, bn), BF16),
                        pltpu.VMEM((bm + 8, bn), F32)],
        compiler_params=_params("parallel", "arbitrary"),
        name="ffn_up",
    )(h, w_up, w_up, conv_w, _rows3(conv_b), conv_state)


def _hgrn_in_kernel(h_ref, wq_ref, wf_ref, wv_ref, wg_ref, lbp_ref,
                    q_ref, b2_ref, k_ref, qe_ref, kd_ref, v_ref, gate_ref,
                    wqb_ref, wfb_ref, wvb_ref, wgb_ref, *, layer):
    @pl.when(pl.program_id(1) == 0)
    def _():
        wqb_ref[...] = wq_ref[...].astype(BF16)
        wfb_ref[...] = wf_ref[...].astype(BF16)
        wvb_ref[...] = wv_ref[...].astype(BF16)
        wgb_ref[...] = wg_ref[...].astype(BF16)

    h = h_ref[...]
    rows, width = q_ref.shape
    q = jnp.dot(h, wqb_ref[...], preferred_element_type=F32)
    fz = jnp.dot(h, wfb_ref[...], preferred_element_type=F32)
    v_ref[...] = jnp.dot(h, wvb_ref[...], preferred_element_type=F32).astype(BF16)
    gate_ref[...] = jnp.dot(h, wgb_ref[...], preferred_element_type=F32)

    lbp = lbp_ref[...]
    e = jnp.exp(lbp - jnp.max(lbp, axis=0, keepdims=True))
    lb = jnp.sum(e[0:layer + 1], axis=0, keepdims=True) / jnp.sum(e, axis=0, keepdims=True)

    f = lb + (1.0 - lb) * jax.nn.sigmoid(fz)
    kk = 1.0 - f
    pos = lax.broadcasted_iota(jnp.int32, (rows, width), 0) % HGRN_BLOCK
    b = jnp.log(f)
    sh = 1
    while sh < HGRN_BLOCK:
        b = b + jnp.where(pos >= sh, pltpu.roll(b, sh, 0), 0.0)
        sh *= 2
    b3 = b.reshape(rows // HGRN_BLOCK, HGRN_BLOCK, width)
    b_last = jnp.broadcast_to(b3[:, HGRN_BLOCK - 1:, :], b3.shape).reshape(rows, width)
    q_ref[...] = q
    b2_ref[...] = b * LOG2E
    k_ref[...] = kk
    qe_ref[...] = (q * jnp.exp(b)).astype(BF16)
    kd_ref[...] = (kk * jnp.exp(b_last - b)).astype(BF16)


def _hgrn_project(h, w, lower_bounds, layer, jl):
    m, d = h.shape
    dk = w.shape[2] // 4
    bm, bn = _tile(m, 1024), _tile(dk, 256)
    nj = dk // bn

    def w_spec(seg):
        return pl.BlockSpec((None, d, bn), lambda j, i: (jl, 0, seg * nj + j))

    o_spec = pl.BlockSpec((bm, bn), lambda j, i: (i, j))
    f32_out, bf16_out = jax.ShapeDtypeStruct((m, dk), F32), jax.ShapeDtypeStruct((m, dk), BF16)
    return pl.pallas_call(
        functools.partial(_hgrn_in_kernel, layer=layer),
        grid=(nj, m // bm),
        in_specs=[pl.BlockSpec((bm, d), lambda j, i: (i, 0)), w_spec(0), w_spec(1), w_spec(2), w_spec(3),
                  pl.BlockSpec((lower_bounds.shape[0], bn), lambda j, i: (0, j))],
        out_specs=[o_spec] * 7,
        out_shape=[f32_out, f32_out, f32_out, bf16_out, bf16_out, bf16_out, f32_out],
        scratch_shapes=[pltpu.VMEM((d, bn), BF16)] * 4,
        compiler_params=_params("parallel", "arbitrary"),
        name="hgrn_in",
    )(h, w, w, w, w, lower_bounds)


def _hgrn_kernel(q_ref, b_ref, k_ref, qe_ref, kd_ref, v_ref, gate_ref, gout_ref, s0_ref, o_ref, sout_ref,
                 st_ref, oacc_ref, *, seq_blocks):
    c = pl.program_id(1)
    t_rows = q_ref.shape[0]
    nblk = t_rows // HGRN_BLOCK
    carry = seq_blocks is None

    if carry:
        @pl.when(c == 0)
        def _():
            st_ref[...] = jnp.zeros_like(st_ref)

    half = HGRN_BLOCK // 2
    row = lax.broadcasted_iota(jnp.int32, (half, LANES), 0)
    lane = lax.broadcasted_iota(jnp.int32, (half, LANES), 1)

    def scores(r0):
        lo, hi = pl.ds(r0, half), pl.ds(r0 + half, half)
        b_lo, b_hi, q_lo, q_hi = b_ref[lo, :], b_ref[hi, :], q_ref[lo, :], q_ref[hi, :]
        sc_lo = jnp.zeros((half, LANES), F32)
        sc_hi = jnp.zeros((half, LANES), F32)
        for s in range(HGRN_BLOCK):
            bs, ks = b_ref[pl.ds(r0 + s, 1), :], k_ref[pl.ds(r0 + s, 1), :]
            col_hi = jnp.sum(jnp.exp2(b_hi - bs) * (q_hi * ks), axis=-1, keepdims=True)
            sc_hi = jnp.where(lane == s, col_hi, sc_hi)
            if s < half:
                col_lo = jnp.sum(jnp.exp2(b_lo - bs) * (q_lo * ks), axis=-1, keepdims=True)
                sc_lo = jnp.where(lane == s, col_lo, sc_lo)
        sc = jnp.concatenate([jnp.where(row >= lane, sc_lo, 0.0),
                              jnp.where(row + half >= lane, sc_hi, 0.0)], axis=0)
        return sc[:, 0:HGRN_BLOCK].astype(BF16)

    group = min(nblk, HGRN_GROUP) if carry else nblk

    def blocks(jg, st):
        rows, sc, vb, decay, upd = [], [], [], [], []
        for g in range(group):
            r0 = pl.multiple_of((jg * group + g) * HGRN_BLOCK, HGRN_BLOCK)
            rows.append(pl.ds(r0, HGRN_BLOCK))
            vb.append(v_ref[rows[g], :])
            sc.append(scores(r0))
            decay.append(jnp.exp2(b_ref[pl.ds(r0 + HGRN_BLOCK - 1, 1), :]))
            upd.append(lax.dot_general(vb[g], kd_ref[rows[g], :], (((0,), (0,)), ((), ())),
                                       preferred_element_type=F32))
        states = []
        for g in range(group):
            if not carry and g % seq_blocks == 0:
                st = s0_ref[g // seq_blocks].T
            states.append(st.astype(BF16))
            st = st * decay[g] + upd[g]
            if not carry and (g + 1) % seq_blocks == 0:
                sout_ref[g // seq_blocks] = st.T
        for g in range(group):
            o = lax.dot_general(qe_ref[rows[g], :], states[g], (((1,), (1,)), ((), ())),
                                preferred_element_type=F32)
            oacc_ref[rows[g], :] = o + jnp.dot(sc[g], vb[g], preferred_element_type=F32)
        return st

    if carry:
        st_ref[...] = lax.fori_loop(0, nblk // group, blocks, st_ref[...])
    else:
        blocks(0, None)

    o = oacc_ref[...]
    y = o * lax.rsqrt(jnp.mean(o * o, axis=-1, keepdims=True) + EPS) * gout_ref[...]
    gate = gate_ref[...]
    o_ref[...] = (y * (gate * jax.nn.sigmoid(gate))).astype(BF16)

    if carry:
        @pl.when(c == pl.num_programs(1) - 1)
        def _():
            sout_ref[...] = st_ref[...].T


def _hgrn_scan(ops, out_norm, state, jl, n_prompt_rows, seq):
    nb, nh, dk, dv = state.shape[1:]
    assert dk == LANES and dv == LANES
    scratch = lambda t: [pltpu.VMEM((dv, dk), F32), pltpu.VMEM((t, dv), F32)]

    def specs(t, row_of):
        return ([pl.BlockSpec((t, LANES), lambda h, c: (row_of(c), h))] * len(ops)
                + [pl.BlockSpec((None, 1, dv), lambda h, c: (jl, 0, 0))])

    out_norm = _rows3(out_norm)
    tp = _tile(n_prompt_rows, HGRN_TILE)
    o_p, s_p = pl.pallas_call(
        functools.partial(_hgrn_kernel, seq_blocks=None),
        grid=(nh, n_prompt_rows // tp),
        in_specs=specs(tp, lambda c: c) + [pl.BlockSpec((None, None, None, dk, dv),
                                                        lambda h, c: (jl, 0, h, 0, 0))],
        out_specs=[pl.BlockSpec((tp, dv), lambda h, c: (c, h)),
                   pl.BlockSpec((None, dk, dv), lambda h, c: (h, 0, 0))],
        out_shape=[jax.ShapeDtypeStruct((n_prompt_rows, nh * dv), BF16),
                   jax.ShapeDtypeStruct((nh, dk, dv), F32)],
        scratch_shapes=scratch(tp),
        compiler_params=_params("parallel", "arbitrary"),
        name="hgrn_prompt",
    )(*ops, out_norm, state)
    ns = _tile(nb, max(1, HGRN_GROUP * HGRN_BLOCK // seq))
    ts = ns * seq
    assert n_prompt_rows % ts == 0 and seq % HGRN_BLOCK == 0
    r0 = n_prompt_rows // ts
    o_s, s_s = pl.pallas_call(
        functools.partial(_hgrn_kernel, seq_blocks=seq // HGRN_BLOCK),
        grid=(nh, nb // ns),
        in_specs=specs(ts, lambda c: r0 + c) + [pl.BlockSpec((None, ns, None, dk, dv),
                                                              lambda h, c: (jl, c, h, 0, 0))],
        out_specs=[pl.BlockSpec((ts, dv), lambda h, c: (c, h)),
                   pl.BlockSpec((ns, None, dk, dv), lambda h, c: (c, h, 0, 0))],
        out_shape=[jax.ShapeDtypeStruct((nb * seq, nh * dv), BF16),
                   jax.ShapeDtypeStruct((nb, nh, dk, dv), F32)],
        scratch_shapes=scratch(ts),
        compiler_params=_params("parallel", "arbitrary"),
        name="hgrn_sample",
    )(*ops, out_norm, state)
    return (o_p, o_s), s_p, s_s


def _lanes(x, n):
    return x[:, :n] if n <= LANES else jnp.concatenate([x] * (n // LANES), axis=1)


def _scores(q, k):
    return lax.dot_general(q, k, (((1,), (1,)), ((), ())), preferred_element_type=F32)


def _softmax_step(c, s, m_ref, l_ref, mask=None, rows=slice(None)):
    if mask is not None:
        s = jnp.where(mask, s, -jnp.inf)
    m_prev = m_ref[c, rows]
    m_new = jnp.maximum(m_prev, jnp.max(s, axis=-1, keepdims=True))
    alpha = jnp.exp2(m_prev - m_new)
    p = jnp.exp2(s - _lanes(m_new, s.shape[1]))
    l_ref[c, rows] = alpha * l_ref[c, rows] + jnp.sum(p, axis=-1, keepdims=True)
    m_ref[c, rows] = m_new
    return alpha, p.astype(BF16)


def _pv_step(c, alpha, p, v, acc_ref, rows=slice(None)):
    acc_ref[c, rows] = (_lanes(alpha, v.shape[1]) * acc_ref[c, rows]
                        + jnp.dot(p, v, preferred_element_type=F32))


def _diff_finish(a0, l0, a1, l1, lam_refs, subln, lam_init):
    lq1, lk1, lq2, lk2 = [r[...] for r in lam_refs]
    lam = (jnp.exp(jnp.sum(lq1 * lk1, axis=-1, keepdims=True))
           - jnp.exp(jnp.sum(lq2 * lk2, axis=-1, keepdims=True)) + lam_init)
    o = a0 * _lanes(1.0 / l0, a0.shape[1]) - lam * (a1 * _lanes(1.0 / l1, a1.shape[1]))
    y = o * lax.rsqrt(jnp.mean(o * o, axis=-1, keepdims=True) + EPS)
    return (y * subln * (1.0 - lam_init)).astype(BF16)


def _attn_prompt_kernel(it_ref, jt_ref, q_ref, k_ref, v_ref, lq1, lk1, lq2, lk2, sub_ref, o_ref,
                        m_ref, l_ref, acc_ref, *, lam_init):
    p = pl.program_id(1)
    i, j = it_ref[p], jt_ref[p]
    bq, bk = q_ref.shape[0], k_ref.shape[0]
    dh = q_ref.shape[1] // 2

    @pl.when(j == 0)
    def _():
        m_ref[...] = jnp.full(m_ref.shape, -jnp.inf, F32)
        l_ref[...] = jnp.zeros(l_ref.shape, F32)
        acc_ref[...] = jnp.zeros(acc_ref.shape, F32)

    def steps(*parts):
        work = [(c, mask, rows, keys,
                 _scores(q_ref[rows, c * dh:(c + 1) * dh], k_ref[keys, c * dh:(c + 1) * dh]))
                for mask, rows, keys in parts for c in range(2)]
        for c, mask, rows, keys, s in work:
            v = v_ref[keys, :]
            alpha, p = _softmax_step(c, s, m_ref, l_ref, mask, rows)
            _pv_step(c, alpha, p, v, acc_ref, rows)

    @pl.when(j < i)
    def _():
        steps((None, slice(None), slice(None)))

    @pl.when(j == i)
    def _():
        half = bq // 2

        def chunk_mask(n_keys, first_row):
            qpos = first_row + lax.broadcasted_iota(jnp.int32, (half, n_keys), 0)
            kpos = lax.broadcasted_iota(jnp.int32, (half, n_keys), 1)
            return kpos < (qpos // CHUNK + 1) * CHUNK

        steps((chunk_mask(half, 0), slice(0, half), slice(0, half)),
              (chunk_mask(bk, half), slice(half, bq), slice(None)))
        o_ref[...] = _diff_finish(acc_ref[0], l_ref[0], acc_ref[1], l_ref[1],
                                  (lq1, lk1, lq2, lk2), sub_ref[...], lam_init)


def _attn_prompt(qb, kb, vb, lams, subln, jl, lam_init, n_rows, nh):
    dv = qb.shape[1] // nh
    bq = bk = _tile(n_rows, ATTN_TILE)
    assert (bq // 2) % CHUNK == 0
    nq = n_rows // bq
    pairs = [(i, j) for i in range(nq) for j in range(i + 1)]
    it = jnp.asarray([p[0] for p in pairs], jnp.int32)
    jt = jnp.asarray([p[1] for p in pairs], jnp.int32)
    lam_spec = pl.BlockSpec((None, 1, dv // 2), lambda h, p, it, jt: (jl, 0, 0))
    return pl.pallas_call(
        functools.partial(_attn_prompt_kernel, lam_init=lam_init),
        grid_spec=pltpu.PrefetchScalarGridSpec(
            num_scalar_prefetch=2,
            grid=(nh, len(pairs)),
            in_specs=[pl.BlockSpec((bq, dv), lambda h, p, it, jt: (it[p], h)),
                      pl.BlockSpec((bk, dv), lambda h, p, it, jt: (jt[p], h)),
                      pl.BlockSpec((bk, dv), lambda h, p, it, jt: (jt[p], h)),
                      lam_spec, lam_spec, lam_spec, lam_spec,
                      pl.BlockSpec((None, 1, dv), lambda h, p, it, jt: (jl, 0, 0))],
            out_specs=pl.BlockSpec((bq, dv), lambda h, p, it, jt: (it[p], h)),
            scratch_shapes=[pltpu.VMEM((2, bq, LANES), F32), pltpu.VMEM((2, bq, LANES), F32),
                            pltpu.VMEM((2, bq, dv), F32)]),
        out_shape=jax.ShapeDtypeStruct((n_rows, nh * dv), BF16),
        compiler_params=_params("parallel", "arbitrary"),
        name="attn_prompt",
    )(it, jt, qb, kb, vb, *[_rows3(a) for a in lams], _rows3(subln))


def _attn_sample_kernel(q_ref, ck_ref, cv_ref, kn_ref, vn_ref, lq1, lk1, lq2, lk2, sub_ref, o_ref,
                        m_ref, l_ref, acc_ref, *, lam_init, nh):
    j = pl.program_id(1)
    last = pl.num_programs(1) - 1
    dv = q_ref.shape[1] // nh
    dh = dv // 2

    @pl.when(j == 0)
    def _():
        m_ref[...] = jnp.full(m_ref.shape, -jnp.inf, F32)
        l_ref[...] = jnp.zeros(l_ref.shape, F32)
        acc_ref[...] = jnp.zeros(acc_ref.shape, F32)

    def step(k_of, v_of):
        for h in range(nh):
            v = v_of(h)
            for c in range(2):
                lo = h * dv + c * dh
                a, p = _softmax_step(2 * h + c, _scores(q_ref[:, lo:lo + dh], k_of(h, c)), m_ref, l_ref)
                _pv_step(2 * h + c, a, p, v, acc_ref)

    @pl.when(j < last)
    def _():
        bk = ck_ref.shape[0] // (2 * nh)

        def rows(ref, first):
            return ref[pl.ds(first, bk, stride=2 * nh), :].astype(BF16)

        step(lambda h, c: rows(ck_ref, 2 * h + c),
             lambda h: jnp.concatenate([rows(cv_ref, h), rows(cv_ref, nh + h)], axis=1))

    @pl.when(j == last)
    def _():
        step(lambda h, c: kn_ref[:, h * dv + c * dh:h * dv + (c + 1) * dh],
             lambda h: vn_ref[:, h * dv:(h + 1) * dv])
        for h in range(nh):
            o_ref[:, h * dv:(h + 1) * dv] = _diff_finish(
                acc_ref[2 * h], l_ref[2 * h], acc_ref[2 * h + 1], l_ref[2 * h + 1],
                (lq1, lk1, lq2, lk2), sub_ref[...], lam_init)


def _attn_sample(qb, kb, vb, cache_k, cache_v, lams, subln, jl, lam_init, n_prompt_rows, seq, nh):
    d = qb.shape[1]
    dv = d // nh
    nb, past = cache_k.shape[1], cache_k.shape[2]
    bk = _tile(past, 512)
    nkc = past // bk
    r0 = n_prompt_rows // seq
    new_spec = pl.BlockSpec((seq, d), lambda b, j: (r0 + b, 0))
    nl = cache_k.shape[0]
    ck = cache_k.reshape(nl, nb, past * nh * 2, dv // 2)
    cv = cache_v.reshape(nl, nb, past, nh, 2, dv // 2).transpose(0, 1, 2, 4, 3, 5).reshape(ck.shape)
    cache_spec = pl.BlockSpec((None, None, bk * nh * 2, dv // 2),
                              lambda b, j: (jl, b, jnp.minimum(j, nkc - 1), 0))
    lam_spec = pl.BlockSpec((None, 1, dv // 2), lambda b, j: (jl, 0, 0))
    return pl.pallas_call(
        functools.partial(_attn_sample_kernel, lam_init=lam_init, nh=nh),
        grid=(nb, nkc + 1),
        in_specs=[new_spec, cache_spec, cache_spec, new_spec, new_spec,
                  lam_spec, lam_spec, lam_spec, lam_spec,
                  pl.BlockSpec((None, 1, dv), lambda b, j: (jl, 0, 0))],
        out_specs=pl.BlockSpec((seq, d), lambda b, j: (b, 0)),
        out_shape=jax.ShapeDtypeStruct((nb * seq, d), BF16),
        scratch_shapes=[pltpu.VMEM((2 * nh, seq, LANES), F32), pltpu.VMEM((2 * nh, seq, LANES), F32),
                        pltpu.VMEM((2 * nh, seq, dv), F32)],
        compiler_params=_params("parallel", "arbitrary"),
        name="attn_sample",
    )(qb, ck, cv, kb, vb, *[_rows3(a) for a in lams], _rows3(subln))


def kernel(x_prompt, x_sample, state_hgrn, cache_k, cache_v, state_ffn_conv, norm_mix, norm_ffn, hgrn_lower_bounds, w_hgrn_in, w_hgrn_out, hgrn_out_norm, w_diff_in, w_diff_out, diff_q_norm, diff_k_norm, diff_lambda_q1, diff_lambda_k1, diff_lambda_q2, diff_lambda_k2, diff_subln, w_ffn_up, ffn_conv_w, ffn_conv_b, w_ffn_down):
    bp, seq_p, d = x_prompt.shape
    nb, seq_s, _ = x_sample.shape
    assert bp == 1 and seq_s == CHUNK
    depth = norm_mix.shape[0]
    n_mixers = 2
    n_p = bp * seq_p
    diff_heads, dh = cache_k.shape[3], cache_k.shape[5]
    x = (x_prompt.reshape(n_p, d), x_sample.reshape(nb * seq_s, d))

    hgrn_p, hgrn_s, kfs, vfs, tails = [], [], [], [], []
    for i in range(depth):
        jl = i // n_mixers
        h = _rms_norm_bf16(x, norm_mix, i)
        if i % n_mixers == 0:
            ops = _hgrn_project(h, w_hgrn_in, hgrn_lower_bounds, i, jl)
            o, s_p, s_s = _hgrn_scan(ops, hgrn_out_norm, state_hgrn, jl, n_p, seq_s)
            hgrn_p.append(s_p[None])
            hgrn_s.append(s_s)
            x = _matmul(o, w_hgrn_out, jl, n_p, res=x, name="hgrn_out")
        else:
            lam_init = 0.8 - 0.6 * math.exp(-0.3 * i)
            lams = (diff_lambda_q1, diff_lambda_k1, diff_lambda_q2, diff_lambda_k2)
            qb, kf, kb, vf, vb = _diff_project(h, w_diff_in, jl, diff_q_norm, diff_k_norm, dh, n_p)
            o_p = _attn_prompt(qb, kb, vb, lams, diff_subln, jl, lam_init, n_p, diff_heads)
            o_s = _attn_sample(qb, kb, vb, cache_k, cache_v, lams, diff_subln, jl, lam_init,
                               n_p, seq_s, diff_heads)
            kfs.append(kf)
            vfs.append(vf)
            x = _matmul((o_p, o_s), w_diff_out, jl, n_p, res=x, name="diff_out")
        h = _rms_norm_bf16(x, norm_ffn, i)
        act, tail = _ffn_up(h, w_ffn_up, ffn_conv_w, ffn_conv_b, state_ffn_conv, i, n_p, seq_s)
        tails.append(tail)
        x = _matmul(act, w_ffn_down, i, n_p, res=x, split_out=i == depth - 1, bm=512, bn=1024,
                    w_buffers=1, name="ffn_down")

    x_p, x_s = x
    tail = jnp.stack(tails)
    seg_p = n_p // seq_s
    stack = lambda pairs, k: jnp.stack([p[k] for p in pairs])
    return (x_p.reshape(bp, seq_p, d),
            x_s.reshape(nb, seq_s, d),
            jnp.stack(hgrn_p),
            jnp.stack(hgrn_s),
            stack(kfs, 0).reshape(-1, bp, seq_p, diff_heads, 2, dh),
            stack(vfs, 0).reshape(-1, bp, seq_p, diff_heads, 2 * dh),
            stack(kfs, 1).reshape(-1, nb, seq_s, diff_heads, 2, dh),
            stack(vfs, 1).reshape(-1, nb, seq_s, diff_heads, 2 * dh),
            tail[:, seg_p - 1][:, None],
            tail[:, seg_p:])
```

```python
import functools
import math

import jax
import jax.numpy as jnp
from jax import lax
from jax.experimental import pallas as pl
from jax.experimental.pallas import tpu as pltpu

EPS = 1e-6
LOG2E = math.log2(math.e)
CHUNK = 64
HGRN_BLOCK = 16
ATTN_TILE = 1024
HGRN_TILE = 2048
HGRN_GROUP = 16
CONV_W = 3
LANES = 128
V7X_VMEM_BYTES = 64 * 1024 * 1024
VMEM_LIMIT = V7X_VMEM_BYTES - 8 * 1024 * 1024

F32 = jnp.float32
BF16 = jnp.bfloat16


def _params(*sem):
    return pltpu.CompilerParams(dimension_semantics=sem, vmem_limit_bytes=VMEM_LIMIT)


def _rows3(a):
    return a.reshape(a.shape[0], 1, a.shape[1])


def _tile(n, pref):
    t = min(n, pref)
    while n % t:
        t //= 2
    return t


def _split_specs(n_p, n_s, bm, bn, col):
    npt, nst = n_p // bm, n_s // bm
    return [pl.BlockSpec((bm, bn), lambda j, i: (jnp.minimum(i, npt - 1), col(j))),
            pl.BlockSpec((bm, bn), lambda j, i: (jnp.clip(i - npt, 0, nst - 1), col(j)))]


def _on_rows(i, npt, split, fn, *ref_pairs):
    if not split:
        fn(*[p[0] for p in ref_pairs])
        return
    pl.when(i < npt)(lambda: fn(*[p[0] for p in ref_pairs]))
    pl.when(i >= npt)(lambda: fn(*[p[-1] for p in ref_pairs]))


def _norm_kernel(*refs, npt):
    *x_refs, g_ref, o_ref = refs

    def run(x_ref):
        x = x_ref[...]
        y = x * lax.rsqrt(jnp.mean(x * x, axis=-1, keepdims=True) + EPS)
        o_ref[...] = (y * g_ref[...]).astype(o_ref.dtype)

    _on_rows(pl.program_id(0), npt, len(x_refs) == 2, run, x_refs)


def _rms_norm_bf16(x, gains, layer):
    xs = x if isinstance(x, tuple) else (x,)
    d = xs[0].shape[1]
    m = sum(a.shape[0] for a in xs)
    bm = _tile(math.gcd(*[a.shape[0] for a in xs]), 512)
    npt = xs[0].shape[0] // bm
    if len(xs) == 2:
        nst = xs[1].shape[0] // bm
        x_specs = [pl.BlockSpec((bm, d), lambda i: (jnp.minimum(i, npt - 1), 0)),
                   pl.BlockSpec((bm, d), lambda i: (jnp.clip(i - npt, 0, nst - 1), 0))]
    else:
        x_specs = [pl.BlockSpec((bm, d), lambda i: (i, 0))]
    return pl.pallas_call(
        functools.partial(_norm_kernel, npt=npt),
        grid=(m // bm,),
        in_specs=x_specs + [pl.BlockSpec((None, 1, d), lambda i: (layer, 0, 0))],
        out_specs=pl.BlockSpec((bm, d), lambda i: (i, 0)),
        out_shape=jax.ShapeDtypeStruct((m, d), BF16),
        compiler_params=_params("parallel"),
        name="rms_norm",
    )(*xs, _rows3(gains))


def _mm_kernel(*refs, n_res, split_in, split_out, npt):
    refs = list(refs)
    h_refs = [refs.pop(0) for _ in range(2 if split_in else 1)]
    w_ref = refs.pop(0)
    r_refs = [refs.pop(0) for _ in range(n_res)] or [None]
    o_refs = [refs.pop(0) for _ in range(2 if split_out else 1)]
    wb_ref, = refs
    i = pl.program_id(1)

    @pl.when(i == 0)
    def _():
        wb_ref[...] = w_ref[...].astype(BF16)

    def run(h_ref, r_ref, o_ref):
        acc = jnp.dot(h_ref[...], wb_ref[...], preferred_element_type=F32)
        o_ref[...] = acc if r_ref is None else r_ref[...] + acc

    _on_rows(i, npt, split_in or split_out or n_res == 2, run, h_refs, r_refs, o_refs)


def _matmul(h, w, layer, n_p, *, res=None, split_out=False, bm=1024, bn=512, w_buffers=2, name="matmul"):
    split_in = isinstance(h, tuple)
    k, ncols = w.shape[1], w.shape[2]
    m = sum(a.shape[0] for a in h) if split_in else h.shape[0]
    n_s = m - n_p
    bm, bn = _tile(math.gcd(n_p, n_s), bm), _tile(ncols, bn)
    npt = n_p // bm
    in_specs = (_split_specs(n_p, n_s, bm, k, lambda j: 0) if split_in
                else [pl.BlockSpec((bm, k), lambda j, i: (i, 0))])
    in_specs.append(pl.BlockSpec((None, k, bn), lambda j, i: (layer, 0, j),
                                 pipeline_mode=pl.Buffered(w_buffers)))
    args = list(h) if split_in else [h]
    args.append(w)
    res = () if res is None else res if isinstance(res, tuple) else (res,)
    in_specs += (_split_specs(n_p, n_s, bm, bn, lambda j: j) if len(res) == 2
                 else [pl.BlockSpec((bm, bn), lambda j, i: (i, j))] * len(res))
    args += res
    if split_out:
        out_specs = _split_specs(n_p, n_s, bm, bn, lambda j: j)
        out_shape = [jax.ShapeDtypeStruct((n_p, ncols), F32), jax.ShapeDtypeStruct((n_s, ncols), F32)]
    else:
        out_specs = pl.BlockSpec((bm, bn), lambda j, i: (i, j))
        out_shape = jax.ShapeDtypeStruct((m, ncols), F32)
    return pl.pallas_call(
        functools.partial(_mm_kernel, n_res=len(res), split_in=split_in, split_out=split_out, npt=npt),
        grid=(ncols // bn, m // bm),
        in_specs=in_specs,
        out_specs=out_specs,
        out_shape=out_shape,
        scratch_shapes=[pltpu.VMEM((k, bn), BF16)],
        compiler_params=_params("parallel", "arbitrary"),
        name=name,
    )(*args)


def _headnorm(acc, g, scale):
    outs = []
    for c in range(acc.shape[1] // LANES):
        blk = acc[:, c * LANES:(c + 1) * LANES]
        y = blk * lax.rsqrt(jnp.mean(blk * blk, axis=-1, keepdims=True) + EPS)
        outs.append(y * g * scale if scale != 1.0 else y * g)
    return jnp.concatenate(outs, axis=1) if len(outs) > 1 else outs[0]


def _qkv_kernel(h_ref, wq_ref, wk_ref, wv_ref, gq_ref, gk_ref,
                qb_ref, kfp_ref, kfs_ref, kb_ref, vfp_ref, vfs_ref, vb_ref,
                wqb_ref, wkb_ref, wvb_ref, *, scale, npt):
    i = pl.program_id(1)

    @pl.when(i == 0)
    def _():
        wqb_ref[...] = wq_ref[...].astype(BF16)
        wkb_ref[...] = wk_ref[...].astype(BF16)
        wvb_ref[...] = wv_ref[...].astype(BF16)

    h = h_ref[...]
    q = jnp.dot(h, wqb_ref[...], preferred_element_type=F32)
    k = jnp.dot(h, wkb_ref[...], preferred_element_type=F32)
    v = jnp.dot(h, wvb_ref[...], preferred_element_type=F32)
    qb_ref[...] = _headnorm(q, gq_ref[...], scale).astype(BF16)
    kn = _headnorm(k, gk_ref[...], 1.0)
    kb_ref[...] = kn.astype(BF16)
    vb_ref[...] = v.astype(BF16)

    def put(kf_ref, vf_ref):
        for g in range(kf_ref.shape[1]):
            kf_ref[:, g, :] = kn[:, g * LANES:(g + 1) * LANES]
        vf_ref[...] = v

    _on_rows(i, npt, True, put, (kfp_ref, kfs_ref), (vfp_ref, vfs_ref))


def _diff_project(h, w, layer, q_g, k_g, dh, n_p):
    m, d = h.shape
    n_s = m - n_p
    bm, bn = _tile(math.gcd(n_p, n_s), 1024), _tile(d, 512)
    nj = d // bn
    f_specs = _split_specs(n_p, n_s, bm, bn, lambda j: j)
    f_shapes = [jax.ShapeDtypeStruct((n_p, d), F32), jax.ShapeDtypeStruct((n_s, d), F32)]
    npt, nst, ng = n_p // bm, n_s // bm, bn // dh
    k_specs = [pl.BlockSpec((bm, None, ng, dh), lambda j, i: (jnp.minimum(i, npt - 1), j, 0, 0)),
               pl.BlockSpec((bm, None, ng, dh), lambda j, i: (jnp.clip(i - npt, 0, nst - 1), j, 0, 0))]
    k_shapes = [jax.ShapeDtypeStruct((n_p, nj, ng, dh), F32), jax.ShapeDtypeStruct((n_s, nj, ng, dh), F32)]
    o_spec = pl.BlockSpec((bm, bn), lambda j, i: (i, j))
    b_shape = jax.ShapeDtypeStruct((m, d), BF16)
    g_spec = pl.BlockSpec((None, 1, dh), lambda j, i: (layer, 0, 0))

    def w_spec(seg):
        return pl.BlockSpec((None, d, bn), lambda j, i: (layer, 0, seg * nj + j),
                            pipeline_mode=pl.Buffered(1))

    qb, kf_p, kf_s, kb, vf_p, vf_s, vb = pl.pallas_call(
        functools.partial(_qkv_kernel, scale=dh ** -0.5 * LOG2E, npt=n_p // bm),
        grid=(nj, m // bm),
        in_specs=[pl.BlockSpec((bm, d), lambda j, i: (i, 0)), w_spec(0), w_spec(1), w_spec(2),
                  g_spec, g_spec],
        out_specs=[o_spec] + k_specs + [o_spec] + f_specs + [o_spec],
        out_shape=[b_shape] + k_shapes + [b_shape] + f_shapes + [b_shape],
        scratch_shapes=[pltpu.VMEM((d, bn), BF16)] * 3,
        compiler_params=_params("parallel", "arbitrary"),
        name="diff_qkv",
    )(h, w, w, w, _rows3(q_g), _rows3(k_g))
    return qb, (kf_p, kf_s), kb, (vf_p, vf_s), vb


def _ffn_up_kernel(h_ref, wg_ref, wu_ref, cw_ref, cb_ref, st_ref, a_ref, tail_ref,
                   wgb_ref, wub_ref, g_ref, *, n_prompt_tiles, seq):
    i = pl.program_id(1)
    bm, bn = a_ref.shape
    nseg = bm // seq

    @pl.when(i == 0)
    def _():
        wgb_ref[...] = wg_ref[...].astype(BF16)
        wub_ref[...] = wu_ref[...].astype(BF16)
        g_ref[0:8, :] = jnp.zeros((8, bn), F32)

    h = h_ref[...]
    g = jnp.dot(h, wgb_ref[...], preferred_element_type=F32)
    u = jnp.dot(h, wub_ref[...], preferred_element_type=F32)
    g_ref[8:8 + bm, :] = g
    g1 = g_ref[7:7 + bm, :]
    g2 = g_ref[6:6 + bm, :]
    st = st_ref[...]
    p2 = jnp.broadcast_to(st[:, 0:1, :], (nseg, seq, bn)).reshape(bm, bn)
    p1 = jnp.broadcast_to(st[:, 1:2, :], (nseg, seq, bn)).reshape(bm, bn)
    pos = (lax.broadcasted_iota(jnp.int32, (bm, bn), 0) % seq
           + jnp.where(i >= n_prompt_tiles, 0, seq))
    g1 = jnp.where(pos == 0, p1, g1)
    g2 = jnp.where(pos == 0, p2, jnp.where(pos == 1, p1, g2))
    cw = cw_ref[...]
    conv = cb_ref[...] + cw[2:3, :] * g + cw[1:2, :] * g1 + cw[0:1, :] * g2
    a_ref[...] = (conv * jax.nn.sigmoid(conv) * u).astype(BF16)

    for n in range(nseg):
        end = 8 + (n + 1) * seq
        tail_ref[n] = g_ref[end - (CONV_W - 1):end, :]
    g_ref[0:8, :] = g_ref[bm:bm + 8, :]


def _ffn_up(h, w_up, conv_w, conv_b, conv_state, layer, n_prompt_rows, seq):
    m, d = h.shape
    dff = conv_w.shape[-1]
    nb = conv_state.shape[1]
    bm = _tile(math.gcd(n_prompt_rows, nb * seq), 1024)
    bn = _tile(dff, 512)
    assert bm % seq == 0 and (m - n_prompt_rows) == nb * seq
    nseg = bm // seq
    npt = n_prompt_rows // bm
    nj = dff // bn
    return pl.pallas_call(
        functools.partial(_ffn_up_kernel, n_prompt_tiles=npt, seq=seq),
        grid=(nj, m // bm),
        in_specs=[pl.BlockSpec((bm, d), lambda j, i: (i, 0)),
                  pl.BlockSpec((None, d, bn), lambda j, i: (layer, 0, j)),
                  pl.BlockSpec((None, d, bn), lambda j, i: (layer, 0, nj + j)),
                  pl.BlockSpec((None, CONV_W, bn), lambda j, i: (layer, 0, j)),
                  pl.BlockSpec((None, 1, bn), lambda j, i: (layer, 0, j)),
                  pl.BlockSpec((None, nseg, CONV_W - 1, bn),
                               lambda j, i: (layer, jnp.maximum(i - npt, 0), 0, j))],
        out_specs=[pl.BlockSpec((bm, bn), lambda j, i: (i, j)),
                   pl.BlockSpec((nseg, CONV_W - 1, bn), lambda j, i: (i, 0, j))],
        out_shape=[jax.ShapeDtypeStruct((m, dff), BF16),
                   jax.ShapeDtypeStruct((m // seq, CONV_W - 1, dff), F32)],
        scratch_shapes=[pltpu.VMEM((d, bn), BF16), pltpu.VMEM((d, bn), BF16),
                        pltpu.VMEM((bm + 8, bn), F32)],
        compiler_params=_params("parallel", "arbitrary"),
        name="ffn_up",
    )(h, w_up, w_up, conv_w, _rows3(conv_b), conv_state)


def _hgrn_in_kernel(h_ref, wq_ref, wf_ref, wv_ref, wg_ref, lbp_ref,
                    q_ref, b2_ref, k_ref, qe_ref, kd_ref, v_ref, gate_ref,
                    wqb_ref, wfb_ref, wvb_ref, wgb_ref, *, layer):
    @pl.when(pl.program_id(1) == 0)
    def _():
        wqb_ref[...] = wq_ref[...].astype(BF16)
        wfb_ref[...] = wf_ref[...].astype(BF16)
        wvb_ref[...] = wv_ref[...].astype(BF16)
        wgb_ref[...] = wg_ref[...].astype(BF16)

    h = h_ref[...]
    rows, width = q_ref.shape
    q = jnp.dot(h, wqb_ref[...], preferred_element_type=F32)
    fz = jnp.dot(h, wfb_ref[...], preferred_element_type=F32)
    v_ref[...] = jnp.dot(h, wvb_ref[...], preferred_element_type=F32).astype(BF16)
    gate_ref[...] = jnp.dot(h, wgb_ref[...], preferred_element_type=F32)

    lbp = lbp_ref[...]
    e = jnp.exp(lbp - jnp.max(lbp, axis=0, keepdims=True))
    lb = jnp.sum(e[0:layer + 1], axis=0, keepdims=True) / jnp.sum(e, axis=0, keepdims=True)

    f = lb + (1.0 - lb) * jax.nn.sigmoid(fz)
    kk = 1.0 - f
    pos = lax.broadcasted_iota(jnp.int32, (rows, width), 0) % HGRN_BLOCK
    b = jnp.log(f)
    sh = 1
    while sh < HGRN_BLOCK:
        b = b + jnp.where(pos >= sh, pltpu.roll(b, sh, 0), 0.0)
        sh *= 2
    b3 = b.reshape(rows // HGRN_BLOCK, HGRN_BLOCK, width)
    b_last = jnp.broadcast_to(b3[:, HGRN_BLOCK - 1:, :], b3.shape).reshape(rows, width)
    q_ref[...] = q
    b2_ref[...] = b * LOG2E
    k_ref[...] = kk
    qe_ref[...] = (q * jnp.exp(b)).astype(BF16)
    kd_ref[...] = (kk * jnp.exp(b_last - b)).astype(BF16)


def _hgrn_project(h, w, lower_bounds, layer, jl):
    m, d = h.shape
    dk = w.shape[2] // 4
    bm, bn = _tile(m, 1024), _tile(dk, 256)
    nj = dk // bn

    def w_spec(seg):
        return pl.BlockSpec((None, d, bn), lambda j, i: (jl, 0, seg * nj + j))

    o_spec = pl.BlockSpec((bm, bn), lambda j, i: (i, j))
    f32_out, bf16_out = jax.ShapeDtypeStruct((m, dk), F32), jax.ShapeDtypeStruct((m, dk), BF16)
    return pl.pallas_call(
        functools.partial(_hgrn_in_kernel, layer=layer),
        grid=(nj, m // bm),
        in_specs=[pl.BlockSpec((bm, d), lambda j, i: (i, 0)), w_spec(0), w_spec(1), w_spec(2), w_spec(3),
                  pl.BlockSpec((lower_bounds.shape[0], bn), lambda j, i: (0, j))],
        out_specs=[o_spec] * 7,
        out_shape=[f32_out, f32_out, f32_out, bf16_out, bf16_out, bf16_out, f32_out],
        scratch_shapes=[pltpu.VMEM((d, bn), BF16)] * 4,
        compiler_params=_params("parallel", "arbitrary"),
        name="hgrn_in",
    )(h, w, w, w, w, lower_bounds)


def _hgrn_kernel(q_ref, b_ref, k_ref, qe_ref, kd_ref, v_ref, gate_ref, gout_ref, s0_ref, o_ref, sout_ref,
                 st_ref, oacc_ref, *, seq_blocks):
    c = pl.program_id(1)
    t_rows = q_ref.shape[0]
    nblk = t_rows // HGRN_BLOCK
    carry = seq_blocks is None

    if carry:
        @pl.when(c == 0)
        def _():
            st_ref[...] = jnp.zeros_like(st_ref)

    half = HGRN_BLOCK // 2
    row = lax.broadcasted_iota(jnp.int32, (half, LANES), 0)
    lane = lax.broadcasted_iota(jnp.int32, (half, LANES), 1)

    def scores(r0):
        lo, hi = pl.ds(r0, half), pl.ds(r0 + half, half)
        b_lo, b_hi, q_lo, q_hi = b_ref[lo, :], b_ref[hi, :], q_ref[lo, :], q_ref[hi, :]
        sc_lo = jnp.zeros((half, LANES), F32)
        sc_hi = jnp.zeros((half, LANES), F32)
        for s in range(HGRN_BLOCK):
            bs, ks = b_ref[pl.ds(r0 + s, 1), :], k_ref[pl.ds(r0 + s, 1), :]
            col_hi = jnp.sum(jnp.exp2(b_hi - bs) * (q_hi * ks), axis=-1, keepdims=True)
            sc_hi = jnp.where(lane == s, col_hi, sc_hi)
            if s < half:
                col_lo = jnp.sum(jnp.exp2(b_lo - bs) * (q_lo * ks), axis=-1, keepdims=True)
                sc_lo = jnp.where(lane == s, col_lo, sc_lo)
        sc = jnp.concatenate([jnp.where(row >= lane, sc_lo, 0.0),
                              jnp.where(row + half >= lane, sc_hi, 0.0)], axis=0)
        return sc[:, 0:HGRN_BLOCK].astype(BF16)

    group = min(nblk, HGRN_GROUP) if carry else nblk

    def blocks(jg, st):
        rows, sc, vb, decay, upd = [], [], [], [], []
        for g in range(group):
            r0 = pl.multiple_of((jg * group + g) * HGRN_BLOCK, HGRN_BLOCK)
            rows.append(pl.ds(r0, HGRN_BLOCK))
            vb.append(v_ref[rows[g], :])
            sc.append(scores(r0))
            decay.append(jnp.exp2(b_ref[pl.ds(r0 + HGRN_BLOCK - 1, 1), :]))
            upd.append(lax.dot_general(vb[g], kd_ref[rows[g], :], (((0,), (0,)), ((), ())),
                                       preferred_element_type=F32))
        states = []
        for g in range(group):
            if not carry and g % seq_blocks == 0:
                st = s0_ref[g // seq_blocks].T
            states.append(st.astype(BF16))
            st = st * decay[g] + upd[g]
            if not carry and (g + 1) % seq_blocks == 0:
                sout_ref[g // seq_blocks] = st.T
        for g in range(group):
            o = lax.dot_general(qe_ref[rows[g], :], states[g], (((1,), (1,)), ((), ())),
                                preferred_element_type=F32)
            oacc_ref[rows[g], :] = o + jnp.dot(sc[g], vb[g], preferred_element_type=F32)
        return st

    if carry:
        st_ref[...] = lax.fori_loop(0, nblk // group, blocks, st_ref[...])
    else:
        blocks(0, None)

    o = oacc_ref[...]
    y = o * lax.rsqrt(jnp.mean(o * o, axis=-1, keepdims=True) + EPS) * gout_ref[...]
    gate = gate_ref[...]
    o_ref[...] = (y * (gate * jax.nn.sigmoid(gate))).astype(BF16)

    if carry:
        @pl.when(c == pl.num_programs(1) - 1)
        def _():
            sout_ref[...] = st_ref[...].T


def _hgrn_scan(ops, out_norm, state, jl, n_prompt_rows, seq):
    nb, nh, dk, dv = state.shape[1:]
    assert dk == LANES and dv == LANES
    scratch = lambda t: [pltpu.VMEM((dv, dk), F32), pltpu.VMEM((t, dv), F32)]

    def specs(t, row_of):
        return ([pl.BlockSpec((t, LANES), lambda h, c: (row_of(c), h))] * len(ops)
                + [pl.BlockSpec((None, 1, dv), lambda h, c: (jl, 0, 0))])

    out_norm = _rows3(out_norm)
    tp = _tile(n_prompt_rows, HGRN_TILE)
    o_p, s_p = pl.pallas_call(
        functools.partial(_hgrn_kernel, seq_blocks=None),
        grid=(nh, n_prompt_rows // tp),
        in_specs=specs(tp, lambda c: c) + [pl.BlockSpec((None, None, None, dk, dv),
                                                        lambda h, c: (jl, 0, h, 0, 0))],
        out_specs=[pl.BlockSpec((tp, dv), lambda h, c: (c, h)),
                   pl.BlockSpec((None, dk, dv), lambda h, c: (h, 0, 0))],
        out_shape=[jax.ShapeDtypeStruct((n_prompt_rows, nh * dv), BF16),
                   jax.ShapeDtypeStruct((nh, dk, dv), F32)],
        scratch_shapes=scratch(tp),
        compiler_params=_params("parallel", "arbitrary"),
        name="hgrn_prompt",
    )(*ops, out_norm, state)
    ns = _tile(nb, max(1, HGRN_GROUP * HGRN_BLOCK // seq))
    ts = ns * seq
    assert n_prompt_rows % ts == 0 and seq % HGRN_BLOCK == 0
    r0 = n_prompt_rows // ts
    o_s, s_s = pl.pallas_call(
        functools.partial(_hgrn_kernel, seq_blocks=seq // HGRN_BLOCK),
        grid=(nh, nb // ns),
        in_specs=specs(ts, lambda c: r0 + c) + [pl.BlockSpec((None, ns, None, dk, dv),
                                                              lambda h, c: (jl, c, h, 0, 0))],
        out_specs=[pl.BlockSpec((ts, dv), lambda h, c: (c, h)),
                   pl.BlockSpec((ns, None, dk, dv), lambda h, c: (c, h, 0, 0))],
        out_shape=[jax.ShapeDtypeStruct((nb * seq, nh * dv), BF16),
                   jax.ShapeDtypeStruct((nb, nh, dk, dv), F32)],
        scratch_shapes=scratch(ts),
        compiler_params=_params("parallel", "arbitrary"),
        name="hgrn_sample",
    )(*ops, out_norm, state)
    return (o_p, o_s), s_p, s_s


def _lanes(x, n):
    return x[:, :n] if n <= LANES else jnp.concatenate([x] * (n // LANES), axis=1)


def _scores(q, k):
    return lax.dot_general(q, k, (((1,), (1,)), ((), ())), preferred_element_type=F32)


def _softmax_step(c, s, m_ref, l_ref, mask=None, rows=slice(None)):
    if mask is not None:
        s = jnp.where(mask, s, -jnp.inf)
    m_prev = m_ref[c, rows]
    m_new = jnp.maximum(m_prev, jnp.max(s, axis=-1, keepdims=True))
    alpha = jnp.exp2(m_prev - m_new)
    p = jnp.exp2(s - _lanes(m_new, s.shape[1]))
    l_ref[c, rows] = alpha * l_ref[c, rows] + jnp.sum(p, axis=-1, keepdims=True)
    m_ref[c, rows] = m_new
    return alpha, p.astype(BF16)


def _pv_step(c, alpha, p, v, acc_ref, rows=slice(None)):
    acc_ref[c, rows] = (_lanes(alpha, v.shape[1]) * acc_ref[c, rows]
                        + jnp.dot(p, v, preferred_element_type=F32))


def _diff_finish(a0, l0, a1, l1, lam_refs, subln, lam_init):
    lq1, lk1, lq2, lk2 = [r[...] for r in lam_refs]
    lam = (jnp.exp(jnp.sum(lq1 * lk1, axis=-1, keepdims=True))
           - jnp.exp(jnp.sum(lq2 * lk2, axis=-1, keepdims=True)) + lam_init)
    o = a0 * _lanes(1.0 / l0, a0.shape[1]) - lam * (a1 * _lanes(1.0 / l1, a1.shape[1]))
    y = o * lax.rsqrt(jnp.mean(o * o, axis=-1, keepdims=True) + EPS)
    return (y * subln * (1.0 - lam_init)).astype(BF16)


def _attn_prompt_kernel(it_ref, jt_ref, q_ref, k_ref, v_ref, lq1, lk1, lq2, lk2, sub_ref, o_ref,
                        m_ref, l_ref, acc_ref, *, lam_init):
    p = pl.program_id(1)
    i, j = it_ref[p], jt_ref[p]
    bq, bk = q_ref.shape[0], k_ref.shape[0]
    dh = q_ref.shape[1] // 2

    @pl.when(j == 0)
    def _():
        m_ref[...] = jnp.full(m_ref.shape, -jnp.inf, F32)
        l_ref[...] = jnp.zeros(l_ref.shape, F32)
        acc_ref[...] = jnp.zeros(acc_ref.shape, F32)

    def steps(*parts):
        work = [(c, mask, rows, keys,
                 _scores(q_ref[rows, c * dh:(c + 1) * dh], k_ref[keys, c * dh:(c + 1) * dh]))
                for mask, rows, keys in parts for c in range(2)]
        for c, mask, rows, keys, s in work:
            v = v_ref[keys, :]
            alpha, p = _softmax_step(c, s, m_ref, l_ref, mask, rows)
            _pv_step(c, alpha, p, v, acc_ref, rows)

    @pl.when(j < i)
    def _():
        steps((None, slice(None), slice(None)))

    @pl.when(j == i)
    def _():
        half = bq // 2

        def chunk_mask(n_keys, first_row):
            qpos = first_row + lax.broadcasted_iota(jnp.int32, (half, n_keys), 0)
            kpos = lax.broadcasted_iota(jnp.int32, (half, n_keys), 1)
            return kpos < (qpos // CHUNK + 1) * CHUNK

        steps((chunk_mask(half, 0), slice(0, half), slice(0, half)),
              (chunk_mask(bk, half), slice(half, bq), slice(None)))
        o_ref[...] = _diff_finish(acc_ref[0], l_ref[0], acc_ref[1], l_ref[1],
                                  (lq1, lk1, lq2, lk2), sub_ref[...], lam_init)


def _attn_prompt(qb, kb, vb, lams, subln, jl, lam_init, n_rows, nh):
    dv = qb.shape[1] // nh
    bq = bk = _tile(n_rows, ATTN_TILE)
    assert (bq // 2) % CHUNK == 0
    nq = n_rows // bq
    pairs = [(i, j) for i in range(nq) for j in range(i + 1)]
    it = jnp.asarray([p[0] for p in pairs], jnp.int32)
    jt = jnp.asarray([p[1] for p in pairs], jnp.int32)
    lam_spec = pl.BlockSpec((None, 1, dv // 2), lambda h, p, it, jt: (jl, 0, 0))
    return pl.pallas_call(
        functools.partial(_attn_prompt_kernel, lam_init=lam_init),
        grid_spec=pltpu.PrefetchScalarGridSpec(
            num_scalar_prefetch=2,
            grid=(nh, len(pairs)),
            in_specs=[pl.BlockSpec((bq, dv), lambda h, p, it, jt: (it[p], h)),
                      pl.BlockSpec((bk, dv), lambda h, p, it, jt: (jt[p], h)),
                      pl.BlockSpec((bk, dv), lambda h, p, it, jt: (jt[p], h)),
                      lam_spec, lam_spec, lam_spec, lam_spec,
                      pl.BlockSpec((None, 1, dv), lambda h, p, it, jt: (jl, 0, 0))],
            out_specs=pl.BlockSpec((bq, dv), lambda h, p, it, jt: (it[p], h)),
            scratch_shapes=[pltpu.VMEM((2, bq, LANES), F32), pltpu.VMEM((2, bq, LANES), F32),
                            pltpu.VMEM((2, bq, dv), F32)]),
        out_shape=jax.ShapeDtypeStruct((n_rows, nh * dv), BF16),
        compiler_params=_params("parallel", "arbitrary"),
        name="attn_prompt",
    )(it, jt, qb, kb, vb, *[_rows3(a) for a in lams], _rows3(subln))


def _attn_sample_kernel(q_ref, ck_ref, cv_ref, kn_ref, vn_ref, lq1, lk1, lq2, lk2, sub_ref, o_ref,
                        m_ref, l_ref, acc_ref, *, lam_init, nh):
    j = pl.program_id(1)
    last = pl.num_programs(1) - 1
    dv = q_ref.shape[1] // nh
    dh = dv // 2

    @pl.when(j == 0)
    def _():
        m_ref[...] = jnp.full(m_ref.shape, -jnp.inf, F32)
        l_ref[...] = jnp.zeros(l_ref.shape, F32)
        acc_ref[...] = jnp.zeros(acc_ref.shape, F32)

    def q_of(h, c):
        return q_ref[:, h * dv + c * dh:h * dv + (c + 1) * dh]

    def step(k_of, v_of, scores_first):
        ss = [_scores(q_of(h, c), k_of(h, c)) for h in range(nh) for c in range(2)] if scores_first else None
        for h in range(nh):
            v = v_of(h)
            for c in range(2):
                s = ss[2 * h + c] if scores_first else _scores(q_of(h, c), k_of(h, c))
                a, p = _softmax_step(2 * h + c, s, m_ref, l_ref)
                _pv_step(2 * h + c, a, p, v, acc_ref)

    @pl.when(j < last)
    def _():
        bk = ck_ref.shape[0] // (2 * nh)

        def rows(ref, first):
            return ref[pl.ds(first, bk, stride=2 * nh), :].astype(BF16)

        step(lambda h, c: rows(ck_ref, 2 * h + c),
             lambda h: jnp.concatenate([rows(cv_ref, h), rows(cv_ref, nh + h)], axis=1), False)

    @pl.when(j == last)
    def _():
        step(lambda h, c: kn_ref[:, h * dv + c * dh:h * dv + (c + 1) * dh],
             lambda h: vn_ref[:, h * dv:(h + 1) * dv], True)
        for h in range(nh):
            o_ref[:, h * dv:(h + 1) * dv] = _diff_finish(
                acc_ref[2 * h], l_ref[2 * h], acc_ref[2 * h + 1], l_ref[2 * h + 1],
                (lq1, lk1, lq2, lk2), sub_ref[...], lam_init)


def _attn_sample(qb, kb, vb, cache_k, cache_v, lams, subln, jl, lam_init, n_prompt_rows, seq, nh):
    d = qb.shape[1]
    dv = d // nh
    nb, past = cache_k.shape[1], cache_k.shape[2]
    bk = _tile(past, 1024)
    nkc = past // bk
    r0 = n_prompt_rows // seq
    new_spec = pl.BlockSpec((seq, d), lambda b, j: (r0 + b, 0))
    nl = cache_k.shape[0]
    ck = cache_k.reshape(nl, nb, past * nh * 2, dv // 2)
    cv = cache_v.reshape(nl, nb, past, nh, 2, dv // 2).transpose(0, 1, 2, 4, 3, 5).reshape(ck.shape)
    cache_spec = pl.BlockSpec((None, None, bk * nh * 2, dv // 2),
                              lambda b, j: (jl, b, jnp.minimum(j, nkc - 1), 0))
    lam_spec = pl.BlockSpec((None, 1, dv // 2), lambda b, j: (jl, 0, 0))
    return pl.pallas_call(
        functools.partial(_attn_sample_kernel, lam_init=lam_init, nh=nh),
        grid=(nb, nkc + 1),
        in_specs=[new_spec, cache_spec, cache_spec, new_spec, new_spec,
                  lam_spec, lam_spec, lam_spec, lam_spec,
                  pl.BlockSpec((None, 1, dv), lambda b, j: (jl, 0, 0))],
        out_specs=pl.BlockSpec((seq, d), lambda b, j: (b, 0)),
        out_shape=jax.ShapeDtypeStruct((nb * seq, d), BF16),
        scratch_shapes=[pltpu.VMEM((2 * nh, seq, LANES), F32), pltpu.VMEM((2 * nh, seq, LANES), F32),
                        pltpu.VMEM((2 * nh, seq, dv), F32)],
        compiler_params=_params("parallel", "arbitrary"),
        name="attn_sample",
    )(qb, ck, cv, kb, vb, *[_rows3(a) for a in lams], _rows3(subln))


def kernel(x_prompt, x_sample, state_hgrn, cache_k, cache_v, state_ffn_conv, norm_mix, norm_ffn, hgrn_lower_bounds, w_hgrn_in, w_hgrn_out, hgrn_out_norm, w_diff_in, w_diff_out, diff_q_norm, diff_k_norm, diff_lambda_q1, diff_lambda_k1, diff_lambda_q2, diff_lambda_k2, diff_subln, w_ffn_up, ffn_conv_w, ffn_conv_b, w_ffn_down):
    bp, seq_p, d = x_prompt.shape
    nb, seq_s, _ = x_sample.shape
    assert bp == 1 and seq_s == CHUNK
    depth = norm_mix.shape[0]
    n_mixers = 2
    n_p = bp * seq_p
    diff_heads, dh = cache_k.shape[3], cache_k.shape[5]
    x = (x_prompt.reshape(n_p, d), x_sample.reshape(nb * seq_s, d))

    hgrn_p, hgrn_s, kfs, vfs, tails = [], [], [], [], []
    for i in range(depth):
        jl = i // n_mixers
        h = _rms_norm_bf16(x, norm_mix, i)
        if i % n_mixers == 0:
            ops = _hgrn_project(h, w_hgrn_in, hgrn_lower_bounds, i, jl)
            o, s_p, s_s = _hgrn_scan(ops, hgrn_out_norm, state_hgrn, jl, n_p, seq_s)
            hgrn_p.append(s_p[None])
            hgrn_s.append(s_s)
            x = _matmul(o, w_hgrn_out, jl, n_p, res=x, name="hgrn_out")
        else:
            lam_init = 0.8 - 0.6 * math.exp(-0.3 * i)
            lams = (diff_lambda_q1, diff_lambda_k1, diff_lambda_q2, diff_lambda_k2)
            qb, kf, kb, vf, vb = _diff_project(h, w_diff_in, jl, diff_q_norm, diff_k_norm, dh, n_p)
            o_p = _attn_prompt(qb, kb, vb, lams, diff_subln, jl, lam_init, n_p, diff_heads)
            o_s = _attn_sample(qb, kb, vb, cache_k, cache_v, lams, diff_subln, jl, lam_init,
                               n_p, seq_s, diff_heads)
            kfs.append(kf)
            vfs.append(vf)
            x = _matmul((o_p, o_s), w_diff_out, jl, n_p, res=x, name="diff_out")
        h = _rms_norm_bf16(x, norm_ffn, i)
        act, tail = _ffn_up(h, w_ffn_up, ffn_conv_w, ffn_conv_b, state_ffn_conv, i, n_p, seq_s)
        tails.append(tail)
        x = _matmul(act, w_ffn_down, i, n_p, res=x, split_out=i == depth - 1, bm=512, bn=1024,
                    w_buffers=1, name="ffn_down")

    x_p, x_s = x
    tail = jnp.stack(tails)
    seg_p = n_p // seq_s
    stack = lambda pairs, k: jnp.stack([p[k] for p in pairs])
    return (x_p.reshape(bp, seq_p, d),
            x_s.reshape(nb, seq_s, d),
            jnp.stack(hgrn_p),
            jnp.stack(hgrn_s),
            stack(kfs, 0).reshape(-1, bp, seq_p, diff_heads, 2, dh),
            stack(vfs, 0).reshape(-1, bp, seq_p, diff_heads, 2 * dh),
            stack(kfs, 1).reshape(-1, nb, seq_s, diff_heads, 2, dh),
            stack(vfs, 1).reshape(-1, nb, seq_s, diff_heads, 2 * dh),
            tail[:, seg_p - 1][:, None],
            tail[:, seg_p:])
```

```python
import functools
import math

import jax
import jax.numpy as jnp
from jax import lax
from jax.experimental import pallas as pl
from jax.experimental.pallas import tpu as pltpu

EPS = 1e-6
LOG2E = math.log2(math.e)
CHUNK = 64
HGRN_BLOCK = 16
ATTN_TILE = 1024
HGRN_TILE = 2048
HGRN_GROUP = 32
CONV_W = 3
LANES = 128
V7X_VMEM_BYTES = 64 * 1024 * 1024
VMEM_LIMIT = V7X_VMEM_BYTES - 8 * 1024 * 1024

F32 = jnp.float32
BF16 = jnp.bfloat16


def _params(*sem):
    return pltpu.CompilerParams(dimension_semantics=sem, vmem_limit_bytes=VMEM_LIMIT)


def _rows3(a):
    return a.reshape(a.shape[0], 1, a.shape[1])


def _tile(n, pref):
    t = min(n, pref)
    while n % t:
        t //= 2
    return t


def _split_specs(n_p, n_s, bm, bn, col):
    npt, nst = n_p // bm, n_s // bm
    return [pl.BlockSpec((bm, bn), lambda j, i: (jnp.minimum(i, npt - 1), col(j))),
            pl.BlockSpec((bm, bn), lambda j, i: (jnp.clip(i - npt, 0, nst - 1), col(j)))]


def _on_rows(i, npt, split, fn, *ref_pairs):
    if not split:
        fn(*[p[0] for p in ref_pairs])
        return
    pl.when(i < npt)(lambda: fn(*[p[0] for p in ref_pairs]))
    pl.when(i >= npt)(lambda: fn(*[p[-1] for p in ref_pairs]))


def _norm_kernel(*refs, npt):
    *x_refs, g_ref, o_ref = refs

    def run(x_ref):
        x = x_ref[...]
        y = x * lax.rsqrt(jnp.mean(x * x, axis=-1, keepdims=True) + EPS)
        o_ref[...] = (y * g_ref[...]).astype(o_ref.dtype)

    _on_rows(pl.program_id(0), npt, len(x_refs) == 2, run, x_refs)


def _rms_norm_bf16(x, gains, layer):
    xs = x if isinstance(x, tuple) else (x,)
    d = xs[0].shape[1]
    m = sum(a.shape[0] for a in xs)
    bm = _tile(math.gcd(*[a.shape[0] for a in xs]), 512)
    npt = xs[0].shape[0] // bm
    if len(xs) == 2:
        nst = xs[1].shape[0] // bm
        x_specs = [pl.BlockSpec((bm, d), lambda i: (jnp.minimum(i, npt - 1), 0)),
                   pl.BlockSpec((bm, d), lambda i: (jnp.clip(i - npt, 0, nst - 1), 0))]
    else:
        x_specs = [pl.BlockSpec((bm, d), lambda i: (i, 0))]
    return pl.pallas_call(
        functools.partial(_norm_kernel, npt=npt),
        grid=(m // bm,),
        in_specs=x_specs + [pl.BlockSpec((None, 1, d), lambda i: (layer, 0, 0))],
        out_specs=pl.BlockSpec((bm, d), lambda i: (i, 0)),
        out_shape=jax.ShapeDtypeStruct((m, d), BF16),
        compiler_params=_params("parallel"),
        name="rms_norm",
    )(*xs, _rows3(gains))


def _mm_kernel(*refs, n_res, split_in, split_out, npt):
    refs = list(refs)
    h_refs = [refs.pop(0) for _ in range(2 if split_in else 1)]
    w_ref = refs.pop(0)
    r_refs = [refs.pop(0) for _ in range(n_res)] or [None]
    o_refs = [refs.pop(0) for _ in range(2 if split_out else 1)]
    wb_ref, = refs
    i = pl.program_id(1)

    @pl.when(i == 0)
    def _():
        wb_ref[...] = w_ref[...].astype(BF16)

    def run(h_ref, r_ref, o_ref):
        acc = jnp.dot(h_ref[...], wb_ref[...], preferred_element_type=F32)
        o_ref[...] = acc if r_ref is None else r_ref[...] + acc

    _on_rows(i, npt, split_in or split_out or n_res == 2, run, h_refs, r_refs, o_refs)


def _matmul(h, w, layer, n_p, *, res=None, split_out=False, bm=1024, bn=512, w_buffers=2, name="matmul"):
    split_in = isinstance(h, tuple)
    k, ncols = w.shape[1], w.shape[2]
    m = sum(a.shape[0] for a in h) if split_in else h.shape[0]
    n_s = m - n_p
    bm, bn = _tile(math.gcd(n_p, n_s), bm), _tile(ncols, bn)
    npt = n_p // bm
    in_specs = (_split_specs(n_p, n_s, bm, k, lambda j: 0) if split_in
                else [pl.BlockSpec((bm, k), lambda j, i: (i, 0))])
    in_specs.append(pl.BlockSpec((None, k, bn), lambda j, i: (layer, 0, j),
                                 pipeline_mode=pl.Buffered(w_buffers)))
    args = list(h) if split_in else [h]
    args.append(w)
    res = () if res is None else res if isinstance(res, tuple) else (res,)
    in_specs += (_split_specs(n_p, n_s, bm, bn, lambda j: j) if len(res) == 2
                 else [pl.BlockSpec((bm, bn), lambda j, i: (i, j))] * len(res))
    args += res
    if split_out:
        out_specs = _split_specs(n_p, n_s, bm, bn, lambda j: j)
        out_shape = [jax.ShapeDtypeStruct((n_p, ncols), F32), jax.ShapeDtypeStruct((n_s, ncols), F32)]
    else:
        out_specs = pl.BlockSpec((bm, bn), lambda j, i: (i, j))
        out_shape = jax.ShapeDtypeStruct((m, ncols), F32)
    return pl.pallas_call(
        functools.partial(_mm_kernel, n_res=len(res), split_in=split_in, split_out=split_out, npt=npt),
        grid=(ncols // bn, m // bm),
        in_specs=in_specs,
        out_specs=out_specs,
        out_shape=out_shape,
        scratch_shapes=[pltpu.VMEM((k, bn), BF16)],
        compiler_params=_params("parallel", "arbitrary"),
        name=name,
    )(*args)


def _headnorm(acc, g, scale):
    outs = []
    for c in range(acc.shape[1] // LANES):
        blk = acc[:, c * LANES:(c + 1) * LANES]
        y = blk * lax.rsqrt(jnp.mean(blk * blk, axis=-1, keepdims=True) + EPS)
        outs.append(y * g * scale if scale != 1.0 else y * g)
    return jnp.concatenate(outs, axis=1) if len(outs) > 1 else outs[0]


def _qkv_kernel(h_ref, wq_ref, wk_ref, wv_ref, gq_ref, gk_ref,
                qb_ref, kfp_ref, kfs_ref, kb_ref, vfp_ref, vfs_ref, vb_ref,
                wqb_ref, wkb_ref, wvb_ref, *, scale, npt):
    i = pl.program_id(1)

    @pl.when(i == 0)
    def _():
        wqb_ref[...] = wq_ref[...].astype(BF16)
        wkb_ref[...] = wk_ref[...].astype(BF16)
        wvb_ref[...] = wv_ref[...].astype(BF16)

    h = h_ref[...]
    q = jnp.dot(h, wqb_ref[...], preferred_element_type=F32)
    k = jnp.dot(h, wkb_ref[...], preferred_element_type=F32)
    v = jnp.dot(h, wvb_ref[...], preferred_element_type=F32)
    qb_ref[...] = _headnorm(q, gq_ref[...], scale).astype(BF16)
    kn = _headnorm(k, gk_ref[...], 1.0)
    kb_ref[...] = kn.astype(BF16)
    vb_ref[...] = v.astype(BF16)

    def put(kf_ref, vf_ref):
        for g in range(kf_ref.shape[1]):
            kf_ref[:, g, :] = kn[:, g * LANES:(g + 1) * LANES]
        vf_ref[...] = v

    _on_rows(i, npt, True, put, (kfp_ref, kfs_ref), (vfp_ref, vfs_ref))


def _diff_project(h, w, layer, q_g, k_g, dh, n_p):
    m, d = h.shape
    n_s = m - n_p
    bm, bn = _tile(math.gcd(n_p, n_s), 1024), _tile(d, 512)
    nj = d // bn
    f_specs = _split_specs(n_p, n_s, bm, bn, lambda j: j)
    f_shapes = [jax.ShapeDtypeStruct((n_p, d), F32), jax.ShapeDtypeStruct((n_s, d), F32)]
    npt, nst, ng = n_p // bm, n_s // bm, bn // dh
    k_specs = [pl.BlockSpec((bm, None, ng, dh), lambda j, i: (jnp.minimum(i, npt - 1), j, 0, 0)),
               pl.BlockSpec((bm, None, ng, dh), lambda j, i: (jnp.clip(i - npt, 0, nst - 1), j, 0, 0))]
    k_shapes = [jax.ShapeDtypeStruct((n_p, nj, ng, dh), F32), jax.ShapeDtypeStruct((n_s, nj, ng, dh), F32)]
    o_spec = pl.BlockSpec((bm, bn), lambda j, i: (i, j))
    b_shape = jax.ShapeDtypeStruct((m, d), BF16)
    g_spec = pl.BlockSpec((None, 1, dh), lambda j, i: (layer, 0, 0))

    def w_spec(seg):
        return pl.BlockSpec((None, d, bn), lambda j, i: (layer, 0, seg * nj + j),
                            pipeline_mode=pl.Buffered(1))

    qb, kf_p, kf_s, kb, vf_p, vf_s, vb = pl.pallas_call(
        functools.partial(_qkv_kernel, scale=dh ** -0.5 * LOG2E, npt=n_p // bm),
        grid=(nj, m // bm),
        in_specs=[pl.BlockSpec((bm, d), lambda j, i: (i, 0)), w_spec(0), w_spec(1), w_spec(2),
                  g_spec, g_spec],
        out_specs=[o_spec] + k_specs + [o_spec] + f_specs + [o_spec],
        out_shape=[b_shape] + k_shapes + [b_shape] + f_shapes + [b_shape],
        scratch_shapes=[pltpu.VMEM((d, bn), BF16)] * 3,
        compiler_params=_params("parallel", "arbitrary"),
        name="diff_qkv",
    )(h, w, w, w, _rows3(q_g), _rows3(k_g))
    return qb, (kf_p, kf_s), kb, (vf_p, vf_s), vb


def _ffn_up_kernel(h_ref, wg_ref, wu_ref, cw_ref, cb_ref, st_ref, a_ref, tail_ref,
                   wgb_ref, wub_ref, g_ref, *, n_prompt_tiles, seq):
    i = pl.program_id(1)
    bm, bn = a_ref.shape
    nseg = bm // seq

    @pl.when(i == 0)
    def _():
        wgb_ref[...] = wg_ref[...].astype(BF16)
        wub_ref[...] = wu_ref[...].astype(BF16)
        g_ref[0:8, :] = jnp.zeros((8, bn), F32)

    h = h_ref[...]
    g = jnp.dot(h, wgb_ref[...], preferred_element_type=F32)
    u = jnp.dot(h, wub_ref[...], preferred_element_type=F32)
    g_ref[8:8 + bm, :] = g
    g1 = g_ref[7:7 + bm, :]
    g2 = g_ref[6:6 + bm, :]
    st = st_ref[...]
    p2 = jnp.broadcast_to(st[:, 0:1, :], (nseg, seq, bn)).reshape(bm, bn)
    p1 = jnp.broadcast_to(st[:, 1:2, :], (nseg, seq, bn)).reshape(bm, bn)
    pos = (lax.broadcasted_iota(jnp.int32, (bm, bn), 0) % seq
           + jnp.where(i >= n_prompt_tiles, 0, seq))
    g1 = jnp.where(pos == 0, p1, g1)
    g2 = jnp.where(pos == 0, p2, jnp.where(pos == 1, p1, g2))
    cw = cw_ref[...]
    conv = cb_ref[...] + cw[2:3, :] * g + cw[1:2, :] * g1 + cw[0:1, :] * g2
    a_ref[...] = (conv * jax.nn.sigmoid(conv) * u).astype(BF16)

    for n in range(nseg):
        end = 8 + (n + 1) * seq
        tail_ref[n] = g_ref[end - (CONV_W - 1):end, :]
    g_ref[0:8, :] = g_ref[bm:bm + 8, :]


def _ffn_up(h, w_up, conv_w, conv_b, conv_state, layer, n_prompt_rows, seq):
    m, d = h.shape
    dff = conv_w.shape[-1]
    nb = conv_state.shape[1]
    bm = _tile(math.gcd(n_prompt_rows, nb * seq), 1024)
    bn = _tile(dff, 512)
    assert bm % seq == 0 and (m - n_prompt_rows) == nb * seq
    nseg = bm // seq
    npt = n_prompt_rows // bm
    nj = dff // bn
    return pl.pallas_call(
        functools.partial(_ffn_up_kernel, n_prompt_tiles=npt, seq=seq),
        grid=(nj, m // bm),
        in_specs=[pl.BlockSpec((bm, d), lambda j, i: (i, 0)),
                  pl.BlockSpec((None, d, bn), lambda j, i: (layer, 0, j)),
                  pl.BlockSpec((None, d, bn), lambda j, i: (layer, 0, nj + j)),
                  pl.BlockSpec((None, CONV_W, bn), lambda j, i: (layer, 0, j)),
                  pl.BlockSpec((None, 1, bn), lambda j, i: (layer, 0, j)),
                  pl.BlockSpec((None, nseg, CONV_W - 1, bn),
                               lambda j, i: (layer, jnp.maximum(i - npt, 0), 0, j))],
        out_specs=[pl.BlockSpec((bm, bn), lambda j, i: (i, j)),
                   pl.BlockSpec((nseg, CONV_W - 1, bn), lambda j, i: (i, 0, j))],
        out_shape=[jax.ShapeDtypeStruct((m, dff), BF16),
                   jax.ShapeDtypeStruct((m // seq, CONV_W - 1, dff), F32)],
        scratch_shapes=[pltpu.VMEM((d, bn), BF16), pltpu.VMEM((d, bn), BF16),
                        pltpu.VMEM((bm + 8, bn), F32)],
        compiler_params=_params("parallel", "arbitrary"),
        name="ffn_up",
    )(h, w_up, w_up, conv_w, _rows3(conv_b), conv_state)


def _hgrn_in_kernel(h_ref, wq_ref, wf_ref, wv_ref, wg_ref, lbp_ref,
                    q_ref, b2_ref, k_ref, qe_ref, kd_ref, v_ref, gate_ref,
                    wqb_ref, wfb_ref, wvb_ref, wgb_ref, *, layer):
    @pl.when(pl.program_id(1) == 0)
    def _():
        wqb_ref[...] = wq_ref[...].astype(BF16)
        wfb_ref[...] = wf_ref[...].astype(BF16)
        wvb_ref[...] = wv_ref[...].astype(BF16)
        wgb_ref[...] = wg_ref[...].astype(BF16)

    h = h_ref[...]
    rows, width = q_ref.shape
    q = jnp.dot(h, wqb_ref[...], preferred_element_type=F32)
    fz = jnp.dot(h, wfb_ref[...], preferred_element_type=F32)
    v_ref[...] = jnp.dot(h, wvb_ref[...], preferred_element_type=F32).astype(BF16)
    gate_ref[...] = jnp.dot(h, wgb_ref[...], preferred_element_type=F32)

    lbp = lbp_ref[...]
    e = jnp.exp(lbp - jnp.max(lbp, axis=0, keepdims=True))
    lb = jnp.sum(e[0:layer + 1], axis=0, keepdims=True) / jnp.sum(e, axis=0, keepdims=True)

    f = lb + (1.0 - lb) * jax.nn.sigmoid(fz)
    kk = 1.0 - f
    pos = lax.broadcasted_iota(jnp.int32, (rows, width), 0) % HGRN_BLOCK
    b = jnp.log(f)
    sh = 1
    while sh < HGRN_BLOCK:
        b = b + jnp.where(pos >= sh, pltpu.roll(b, sh, 0), 0.0)
        sh *= 2
    b3 = b.reshape(rows // HGRN_BLOCK, HGRN_BLOCK, width)
    b_last = jnp.broadcast_to(b3[:, HGRN_BLOCK - 1:, :], b3.shape).reshape(rows, width)
    q_ref[...] = q
    b2_ref[...] = b * LOG2E
    k_ref[...] = kk
    qe_ref[...] = (q * jnp.exp(b)).astype(BF16)
    kd_ref[...] = (kk * jnp.exp(b_last - b)).astype(BF16)


def _hgrn_project(h, w, lower_bounds, layer, jl):
    m, d = h.shape
    dk = w.shape[2] // 4
    bm, bn = _tile(m, 1024), _tile(dk, 256)
    nj = dk // bn

    def w_spec(seg):
        return pl.BlockSpec((None, d, bn), lambda j, i: (jl, 0, seg * nj + j))

    o_spec = pl.BlockSpec((bm, bn), lambda j, i: (i, j))
    f32_out, bf16_out = jax.ShapeDtypeStruct((m, dk), F32), jax.ShapeDtypeStruct((m, dk), BF16)
    return pl.pallas_call(
        functools.partial(_hgrn_in_kernel, layer=layer),
        grid=(nj, m // bm),
        in_specs=[pl.BlockSpec((bm, d), lambda j, i: (i, 0)), w_spec(0), w_spec(1), w_spec(2), w_spec(3),
                  pl.BlockSpec((lower_bounds.shape[0], bn), lambda j, i: (0, j))],
        out_specs=[o_spec] * 7,
        out_shape=[f32_out, f32_out, f32_out, bf16_out, bf16_out, bf16_out, f32_out],
        scratch_shapes=[pltpu.VMEM((d, bn), BF16)] * 4,
        compiler_params=_params("parallel", "arbitrary"),
        name="hgrn_in",
    )(h, w, w, w, w, lower_bounds)


def _hgrn_kernel(q_ref, b_ref, k_ref, qe_ref, kd_ref, v_ref, gate_ref, gout_ref, s0_ref, o_ref, sout_ref,
                 st_ref, oacc_ref, *, seq_blocks):
    c = pl.program_id(1)
    t_rows = q_ref.shape[0]
    nblk = t_rows // HGRN_BLOCK
    carry = seq_blocks is None

    if carry:
        @pl.when(c == 0)
        def _():
            st_ref[...] = jnp.zeros_like(st_ref)

    half = HGRN_BLOCK // 2
    row = lax.broadcasted_iota(jnp.int32, (half, LANES), 0)
    lane = lax.broadcasted_iota(jnp.int32, (half, LANES), 1)

    def scores(r0):
        lo, hi = pl.ds(r0, half), pl.ds(r0 + half, half)
        b_lo, b_hi, q_lo, q_hi = b_ref[lo, :], b_ref[hi, :], q_ref[lo, :], q_ref[hi, :]
        b_mid = b_ref[pl.ds(r0 + half - 1, 1), :]
        q_in = (q_hi * jnp.exp2(b_hi - b_mid)).astype(BF16)
        k_out = (k_ref[lo, :] * jnp.exp2(b_mid - b_lo)).astype(BF16)
        k_out = jnp.concatenate([k_out, jnp.zeros((LANES - half, LANES), BF16)], axis=0)
        sc_hi = lax.dot_general(q_in, k_out, (((1,), (1,)), ((), ())), preferred_element_type=F32)
        sc_lo = jnp.zeros((half, LANES), F32)
        for s in range(HGRN_BLOCK):
            bs, ks = b_ref[pl.ds(r0 + s, 1), :], k_ref[pl.ds(r0 + s, 1), :]
            if s < half:
                col_lo = jnp.sum(jnp.exp2(b_lo - bs) * (q_lo * ks), axis=-1, keepdims=True)
                sc_lo = jnp.where(lane == s, col_lo, sc_lo)
            else:
                col_hi = jnp.sum(jnp.exp2(b_hi - bs) * (q_hi * ks), axis=-1, keepdims=True)
                sc_hi = jnp.where(lane == s, col_hi, sc_hi)
        sc = jnp.concatenate([jnp.where(row >= lane, sc_lo, 0.0),
                              jnp.where(row + half >= lane, sc_hi, 0.0)], axis=0)
        return sc[:, 0:HGRN_BLOCK].astype(BF16)

    group = min(nblk, HGRN_GROUP) if carry else nblk

    def blocks(jg, st):
        rows, sc, vb, decay, upd = [], [], [], [], []
        for g in range(group):
            r0 = pl.multiple_of((jg * group + g) * HGRN_BLOCK, HGRN_BLOCK)
            rows.append(pl.ds(r0, HGRN_BLOCK))
            vb.append(v_ref[rows[g], :])
            sc.append(scores(r0))
            decay.append(jnp.exp2(b_ref[pl.ds(r0 + HGRN_BLOCK - 1, 1), :]))
            upd.append(lax.dot_general(vb[g], kd_ref[rows[g], :], (((0,), (0,)), ((), ())),
                                       preferred_element_type=F32))
        states = []
        for g in range(group):
            if not carry and g % seq_blocks == 0:
                st = s0_ref[g // seq_blocks].T
            states.append(st.astype(BF16))
            st = st * decay[g] + upd[g]
            if not carry and (g + 1) % seq_blocks == 0:
                sout_ref[g // seq_blocks] = st.T
        for g in range(group):
            o = lax.dot_general(qe_ref[rows[g], :], states[g], (((1,), (1,)), ((), ())),
                                preferred_element_type=F32)
            oacc_ref[rows[g], :] = o + jnp.dot(sc[g], vb[g], preferred_element_type=F32)
        return st

    if carry:
        st_ref[...] = lax.fori_loop(0, nblk // group, blocks, st_ref[...])
    else:
        blocks(0, None)

    o = oacc_ref[...]
    y = o * lax.rsqrt(jnp.mean(o * o, axis=-1, keepdims=True) + EPS) * gout_ref[...]
    gate = gate_ref[...]
    o_ref[...] = (y * (gate * jax.nn.sigmoid(gate))).astype(BF16)

    if carry:
        @pl.when(c == pl.num_programs(1) - 1)
        def _():
            sout_ref[...] = st_ref[...].T


def _hgrn_scan(ops, out_norm, state, jl, n_prompt_rows, seq):
    nb, nh, dk, dv = state.shape[1:]
    assert dk == LANES and dv == LANES
    scratch = lambda t: [pltpu.VMEM((dv, dk), F32), pltpu.VMEM((t, dv), F32)]

    def specs(t, row_of):
        return ([pl.BlockSpec((t, LANES), lambda h, c: (row_of(c), h))] * len(ops)
                + [pl.BlockSpec((None, 1, dv), lambda h, c: (jl, 0, 0))])

    out_norm = _rows3(out_norm)
    tp = _tile(n_prompt_rows, HGRN_TILE)
    o_p, s_p = pl.pallas_call(
        functools.partial(_hgrn_kernel, seq_blocks=None),
        grid=(nh, n_prompt_rows // tp),
        in_specs=specs(tp, lambda c: c) + [pl.BlockSpec((None, None, None, dk, dv),
                                                        lambda h, c: (jl, 0, h, 0, 0))],
        out_specs=[pl.BlockSpec((tp, dv), lambda h, c: (c, h)),
                   pl.BlockSpec((None, dk, dv), lambda h, c: (h, 0, 0))],
        out_shape=[jax.ShapeDtypeStruct((n_prompt_rows, nh * dv), BF16),
                   jax.ShapeDtypeStruct((nh, dk, dv), F32)],
        scratch_shapes=scratch(tp),
        compiler_params=_params("parallel", "arbitrary"),
        name="hgrn_prompt",
    )(*ops, out_norm, state)
    ns = _tile(nb, max(1, HGRN_GROUP * HGRN_BLOCK // seq))
    ts = ns * seq
    assert n_prompt_rows % ts == 0 and seq % HGRN_BLOCK == 0
    r0 = n_prompt_rows // ts
    o_s, s_s = pl.pallas_call(
        functools.partial(_hgrn_kernel, seq_blocks=seq // HGRN_BLOCK),
        grid=(nh, nb // ns),
        in_specs=specs(ts, lambda c: r0 + c) + [pl.BlockSpec((None, ns, None, dk, dv),
                                                              lambda h, c: (jl, c, h, 0, 0))],
        out_specs=[pl.BlockSpec((ts, dv), lambda h, c: (c, h)),
                   pl.BlockSpec((ns, None, dk, dv), lambda h, c: (c, h, 0, 0))],
        out_shape=[jax.ShapeDtypeStruct((nb * seq, nh * dv), BF16),
                   jax.ShapeDtypeStruct((nb, nh, dk, dv), F32)],
        scratch_shapes=scratch(ts),
        compiler_params=_params("parallel", "arbitrary"),
        name="hgrn_sample",
    )(*ops, out_norm, state)
    return (o_p, o_s), s_p, s_s


def _lanes(x, n):
    return x[:, :n] if n <= LANES else jnp.concatenate([x] * (n // LANES), axis=1)


def _scores(q, k):
    return lax.dot_general(q, k, (((1,), (1,)), ((), ())), preferred_element_type=F32)


def _softmax_step(c, s, m_ref, l_ref, mask=None, rows=slice(None)):
    if mask is not None:
        s = jnp.where(mask, s, -jnp.inf)
    m_prev = m_ref[c, rows]
    m_new = jnp.maximum(m_prev, jnp.max(s, axis=-1, keepdims=True))
    alpha = jnp.exp2(m_prev - m_new)
    p = jnp.exp2(s - _lanes(m_new, s.shape[1]))
    l_ref[c, rows] = alpha * l_ref[c, rows] + jnp.sum(p, axis=-1, keepdims=True)
    m_ref[c, rows] = m_new
    return alpha, p.astype(BF16)


def _pv_step(c, alpha, p, v, acc_ref, rows=slice(None)):
    acc_ref[c, rows] = (_lanes(alpha, v.shape[1]) * acc_ref[c, rows]
                        + jnp.dot(p, v, preferred_element_type=F32))


def _diff_finish(a0, l0, a1, l1, lam_refs, subln, lam_init):
    lq1, lk1, lq2, lk2 = [r[...] for r in lam_refs]
    lam = (jnp.exp(jnp.sum(lq1 * lk1, axis=-1, keepdims=True))
           - jnp.exp(jnp.sum(lq2 * lk2, axis=-1, keepdims=True)) + lam_init)
    o = a0 * _lanes(1.0 / l0, a0.shape[1]) - lam * (a1 * _lanes(1.0 / l1, a1.shape[1]))
    y = o * lax.rsqrt(jnp.mean(o * o, axis=-1, keepdims=True) + EPS)
    return (y * subln * (1.0 - lam_init)).astype(BF16)


def _attn_prompt_kernel(it_ref, jt_ref, q_ref, k_ref, v_ref, lq1, lk1, lq2, lk2, sub_ref, o_ref,
                        m_ref, l_ref, acc_ref, *, lam_init):
    p = pl.program_id(1)
    i, j = it_ref[p], jt_ref[p]
    bq, bk = q_ref.shape[0], k_ref.shape[0]
    dh = q_ref.shape[1] // 2

    @pl.when(j == 0)
    def _():
        m_ref[...] = jnp.full(m_ref.shape, -jnp.inf, F32)
        l_ref[...] = jnp.zeros(l_ref.shape, F32)
        acc_ref[...] = jnp.zeros(acc_ref.shape, F32)

    def steps(*parts):
        work = [(c, mask, rows, keys,
                 _scores(q_ref[rows, c * dh:(c + 1) * dh], k_ref[keys, c * dh:(c + 1) * dh]))
                for mask, rows, keys in parts for c in range(2)]
        for c, mask, rows, keys, s in work:
            v = v_ref[keys, :]
            alpha, p = _softmax_step(c, s, m_ref, l_ref, mask, rows)
            _pv_step(c, alpha, p, v, acc_ref, rows)

    @pl.when(j < i)
    def _():
        steps((None, slice(None), slice(None)))

    @pl.when(j == i)
    def _():
        half = bq // 2

        def chunk_mask(n_keys, first_row):
            qpos = first_row + lax.broadcasted_iota(jnp.int32, (half, n_keys), 0)
            kpos = lax.broadcasted_iota(jnp.int32, (half, n_keys), 1)
            return kpos < (qpos // CHUNK + 1) * CHUNK

        steps((chunk_mask(half, 0), slice(0, half), slice(0, half)),
              (chunk_mask(bk, half), slice(half, bq), slice(None)))
        o_ref[...] = _diff_finish(acc_ref[0], l_ref[0], acc_ref[1], l_ref[1],
                                  (lq1, lk1, lq2, lk2), sub_ref[...], lam_init)


def _attn_prompt(qb, kb, vb, lams, subln, jl, lam_init, n_rows, nh):
    dv = qb.shape[1] // nh
    bq = bk = _tile(n_rows, ATTN_TILE)
    assert (bq // 2) % CHUNK == 0
    nq = n_rows // bq
    pairs = [(i, j) for i in range(nq) for j in range(i + 1)]
    it = jnp.asarray([p[0] for p in pairs], jnp.int32)
    jt = jnp.asarray([p[1] for p in pairs], jnp.int32)
    lam_spec = pl.BlockSpec((None, 1, dv // 2), lambda h, p, it, jt: (jl, 0, 0))
    return pl.pallas_call(
        functools.partial(_attn_prompt_kernel, lam_init=lam_init),
        grid_spec=pltpu.PrefetchScalarGridSpec(
            num_scalar_prefetch=2,
            grid=(nh, len(pairs)),
            in_specs=[pl.BlockSpec((bq, dv), lambda h, p, it, jt: (it[p], h)),
                      pl.BlockSpec((bk, dv), lambda h, p, it, jt: (jt[p], h)),
                      pl.BlockSpec((bk, dv), lambda h, p, it, jt: (jt[p], h)),
                      lam_spec, lam_spec, lam_spec, lam_spec,
                      pl.BlockSpec((None, 1, dv), lambda h, p, it, jt: (jl, 0, 0))],
            out_specs=pl.BlockSpec((bq, dv), lambda h, p, it, jt: (it[p], h)),
            scratch_shapes=[pltpu.VMEM((2, bq, LANES), F32), pltpu.VMEM((2, bq, LANES), F32),
                            pltpu.VMEM((2, bq, dv), F32)]),
        out_shape=jax.ShapeDtypeStruct((n_rows, nh * dv), BF16),
        compiler_params=_params("parallel", "arbitrary"),
        name="attn_prompt",
    )(it, jt, qb, kb, vb, *[_rows3(a) for a in lams], _rows3(subln))


def _attn_sample_kernel(q_ref, ck_ref, cv_ref, kn_ref, vn_ref, lq1, lk1, lq2, lk2, sub_ref, o_ref,
                        m_ref, l_ref, acc_ref, *, lam_init, nh):
    j = pl.program_id(1)
    last = pl.num_programs(1) - 1
    dv = q_ref.shape[1] // nh
    dh = dv // 2

    @pl.when(j == 0)
    def _():
        m_ref[...] = jnp.full(m_ref.shape, -jnp.inf, F32)
        l_ref[...] = jnp.zeros(l_ref.shape, F32)
        acc_ref[...] = jnp.zeros(acc_ref.shape, F32)

    def q_of(h, c):
        return q_ref[:, h * dv + c * dh:h * dv + (c + 1) * dh]

    def step(k_of, v_of, scores_first):
        ss = [_scores(q_of(h, c), k_of(h, c)) for h in range(nh) for c in range(2)] if scores_first else None
        for h in range(nh):
            v = v_of(h)
            for c in range(2):
                s = ss[2 * h + c] if scores_first else _scores(q_of(h, c), k_of(h, c))
                a, p = _softmax_step(2 * h + c, s, m_ref, l_ref)
                _pv_step(2 * h + c, a, p, v, acc_ref)

    @pl.when(j < last)
    def _():
        bk = ck_ref.shape[0] // (2 * nh)

        def rows(ref, first):
            return ref[pl.ds(first, bk, stride=2 * nh), :].astype(BF16)

        step(lambda h, c: rows(ck_ref, 2 * h + c),
             lambda h: jnp.concatenate([rows(cv_ref, h), rows(cv_ref, nh + h)], axis=1), False)

    @pl.when(j == last)
    def _():
        step(lambda h, c: kn_ref[:, h * dv + c * dh:h * dv + (c + 1) * dh],
             lambda h: vn_ref[:, h * dv:(h + 1) * dv], True)
        for h in range(nh):
            o_ref[:, h * dv:(h + 1) * dv] = _diff_finish(
                acc_ref[2 * h], l_ref[2 * h], acc_ref[2 * h + 1], l_ref[2 * h + 1],
                (lq1, lk1, lq2, lk2), sub_ref[...], lam_init)


def _attn_sample(qb, kb, vb, cache_k, cache_v, lams, subln, jl, lam_init, n_prompt_rows, seq, nh):
    d = qb.shape[1]
    dv = d // nh
    nb, past = cache_k.shape[1], cache_k.shape[2]
    bk = _tile(past, 512)
    nkc = past // bk
    r0 = n_prompt_rows // seq
    new_spec = pl.BlockSpec((seq, d), lambda b, j: (r0 + b, 0))
    nl = cache_k.shape[0]
    ck = cache_k.reshape(nl, nb, past * nh * 2, dv // 2)
    cv = cache_v.reshape(nl, nb, past, nh, 2, dv // 2).transpose(0, 1, 2, 4, 3, 5).reshape(ck.shape)
    cache_spec = pl.BlockSpec((None, None, bk * nh * 2, dv // 2),
                              lambda b, j: (jl, b, jnp.minimum(j, nkc - 1), 0))
    lam_spec = pl.BlockSpec((None, 1, dv // 2), lambda b, j: (jl, 0, 0))
    return pl.pallas_call(
        functools.partial(_attn_sample_kernel, lam_init=lam_init, nh=nh),
        grid=(nb, nkc + 1),
        in_specs=[new_spec, cache_spec, cache_spec, new_spec, new_spec,
                  lam_spec, lam_spec, lam_spec, lam_spec,
                  pl.BlockSpec((None, 1, dv), lambda b, j: (jl, 0, 0))],
        out_specs=pl.BlockSpec((seq, d), lambda b, j: (b, 0)),
        out_shape=jax.ShapeDtypeStruct((nb * seq, d), BF16),
        scratch_shapes=[pltpu.VMEM((2 * nh, seq, LANES), F32), pltpu.VMEM((2 * nh, seq, LANES), F32),
                        pltpu.VMEM((2 * nh, seq, dv), F32)],
        compiler_params=_params("parallel", "arbitrary"),
        name="attn_sample",
    )(qb, ck, cv, kb, vb, *[_rows3(a) for a in lams], _rows3(subln))


def kernel(x_prompt, x_sample, state_hgrn, cache_k, cache_v, state_ffn_conv, norm_mix, norm_ffn, hgrn_lower_bounds, w_hgrn_in, w_hgrn_out, hgrn_out_norm, w_diff_in, w_diff_out, diff_q_norm, diff_k_norm, diff_lambda_q1, diff_lambda_k1, diff_lambda_q2, diff_lambda_k2, diff_subln, w_ffn_up, ffn_conv_w, ffn_conv_b, w_ffn_down):
    bp, seq_p, d = x_prompt.shape
    nb, seq_s, _ = x_sample.shape
    assert bp == 1 and seq_s == CHUNK
    depth = norm_mix.shape[0]
    n_mixers = 2
    n_p = bp * seq_p
    diff_heads, dh = cache_k.shape[3], cache_k.shape[5]
    x = (x_prompt.reshape(n_p, d), x_sample.reshape(nb * seq_s, d))

    hgrn_p, hgrn_s, kfs, vfs, tails = [], [], [], [], []
    for i in range(depth):
        jl = i // n_mixers
        h = _rms_norm_bf16(x, norm_mix, i)
        if i % n_mixers == 0:
            ops = _hgrn_project(h, w_hgrn_in, hgrn_lower_bounds, i, jl)
            o, s_p, s_s = _hgrn_scan(ops, hgrn_out_norm, state_hgrn, jl, n_p, seq_s)
            hgrn_p.append(s_p[None])
            hgrn_s.append(s_s)
            x = _matmul(o, w_hgrn_out, jl, n_p, res=x, name="hgrn_out")
        else:
            lam_init = 0.8 - 0.6 * math.exp(-0.3 * i)
            lams = (diff_lambda_q1, diff_lambda_k1, diff_lambda_q2, diff_lambda_k2)
            qb, kf, kb, vf, vb = _diff_project(h, w_diff_in, jl, diff_q_norm, diff_k_norm, dh, n_p)
            o_p = _attn_prompt(qb, kb, vb, lams, diff_subln, jl, lam_init, n_p, diff_heads)
            o_s = _attn_sample(qb, kb, vb, cache_k, cache_v, lams, diff_subln, jl, lam_init,
                               n_p, seq_s, diff_heads)
            kfs.append(kf)
            vfs.append(vf)
            x = _matmul((o_p, o_s), w_diff_out, jl, n_p, res=x, name="diff_out")
        h = _rms_norm_bf16(x, norm_ffn, i)
        act, tail = _ffn_up(h, w_ffn_up, ffn_conv_w, ffn_conv_b, state_ffn_conv, i, n_p, seq_s)
        tails.append(tail)
        x = _matmul(act, w_ffn_down, i, n_p, res=x, split_out=i == depth - 1, bm=512, bn=1024,
                    w_buffers=1, name="ffn_down")

    x_p, x_s = x
    tail = jnp.stack(tails)
    seg_p = n_p // seq_s
    stack = lambda pairs, k: jnp.stack([p[k] for p in pairs])
    return (x_p.reshape(bp, seq_p, d),
            x_s.reshape(nb, seq_s, d),
            jnp.stack(hgrn_p),
            jnp.stack(hgrn_s),
            stack(kfs, 0).reshape(-1, bp, seq_p, diff_heads, 2, dh),
            stack(vfs, 0).reshape(-1, bp, seq_p, diff_heads, 2 * dh),
            stack(kfs, 1).reshape(-1, nb, seq_s, diff_heads, 2, dh),
            stack(vfs, 1).reshape(-1, nb, seq_s, diff_heads, 2 * dh),
            tail[:, seg_p - 1][:, None],
            tail[:, seg_p:])
```

```python
import functools
import math

import jax
import jax.numpy as jnp
from jax import lax
from jax.experimental import pallas as pl
from jax.experimental.pallas import tpu as pltpu

EPS = 1e-6
LOG2E = math.log2(math.e)
CHUNK = 64
HGRN_BLOCK = 16
ATTN_TILE = 1024
HGRN_TILE = 2048
HGRN_GROUP = 32
CONV_W = 3
LANES = 128
V7X_VMEM_BYTES = 64 * 1024 * 1024
VMEM_LIMIT = V7X_VMEM_BYTES - 8 * 1024 * 1024

F32 = jnp.float32
BF16 = jnp.bfloat16


def _params(*sem):
    return pltpu.CompilerParams(dimension_semantics=sem, vmem_limit_bytes=VMEM_LIMIT)


def _rows3(a):
    return a.reshape(a.shape[0], 1, a.shape[1])


def _tile(n, pref):
    t = min(n, pref)
    while n % t:
        t //= 2
    return t


def _split_specs(n_p, n_s, bm, bn, col):
    npt, nst = n_p // bm, n_s // bm
    return [pl.BlockSpec((bm, bn), lambda j, i: (jnp.minimum(i, npt - 1), col(j))),
            pl.BlockSpec((bm, bn), lambda j, i: (jnp.clip(i - npt, 0, nst - 1), col(j)))]


def _on_rows(i, npt, split, fn, *ref_pairs):
    if not split:
        fn(*[p[0] for p in ref_pairs])
        return
    pl.when(i < npt)(lambda: fn(*[p[0] for p in ref_pairs]))
    pl.when(i >= npt)(lambda: fn(*[p[-1] for p in ref_pairs]))


def _norm_kernel(*refs, npt):
    *x_refs, g_ref, o_ref = refs

    def run(x_ref):
        x = x_ref[...]
        y = x * lax.rsqrt(jnp.mean(x * x, axis=-1, keepdims=True) + EPS)
        o_ref[...] = (y * g_ref[...]).astype(o_ref.dtype)

    _on_rows(pl.program_id(0), npt, len(x_refs) == 2, run, x_refs)


def _rms_norm_bf16(x, gains, layer):
    xs = x if isinstance(x, tuple) else (x,)
    d = xs[0].shape[1]
    m = sum(a.shape[0] for a in xs)
    bm = _tile(math.gcd(*[a.shape[0] for a in xs]), 512)
    npt = xs[0].shape[0] // bm
    if len(xs) == 2:
        nst = xs[1].shape[0] // bm
        x_specs = [pl.BlockSpec((bm, d), lambda i: (jnp.minimum(i, npt - 1), 0)),
                   pl.BlockSpec((bm, d), lambda i: (jnp.clip(i - npt, 0, nst - 1), 0))]
    else:
        x_specs = [pl.BlockSpec((bm, d), lambda i: (i, 0))]
    return pl.pallas_call(
        functools.partial(_norm_kernel, npt=npt),
        grid=(m // bm,),
        in_specs=x_specs + [pl.BlockSpec((None, 1, d), lambda i: (layer, 0, 0))],
        out_specs=pl.BlockSpec((bm, d), lambda i: (i, 0)),
        out_shape=jax.ShapeDtypeStruct((m, d), BF16),
        compiler_params=_params("parallel"),
        name="rms_norm",
    )(*xs, _rows3(gains))


def _mm_kernel(*refs, n_res, split_in, split_out, npt):
    refs = list(refs)
    h_refs = [refs.pop(0) for _ in range(2 if split_in else 1)]
    w_ref = refs.pop(0)
    r_refs = [refs.pop(0) for _ in range(n_res)] or [None]
    o_refs = [refs.pop(0) for _ in range(2 if split_out else 1)]
    wb_ref, = refs
    i = pl.program_id(1)

    @pl.when(i == 0)
    def _():
        wb_ref[...] = w_ref[...].astype(BF16)

    def run(h_ref, r_ref, o_ref):
        acc = jnp.dot(h_ref[...], wb_ref[...], preferred_element_type=F32)
        o_ref[...] = acc if r_ref is None else r_ref[...] + acc

    _on_rows(i, npt, split_in or split_out or n_res == 2, run, h_refs, r_refs, o_refs)


def _matmul(h, w, layer, n_p, *, res=None, split_out=False, bm=1024, bn=512, w_buffers=2, name="matmul"):
    split_in = isinstance(h, tuple)
    k, ncols = w.shape[1], w.shape[2]
    m = sum(a.shape[0] for a in h) if split_in else h.shape[0]
    n_s = m - n_p
    bm, bn = _tile(math.gcd(n_p, n_s), bm), _tile(ncols, bn)
    npt = n_p // bm
    in_specs = (_split_specs(n_p, n_s, bm, k, lambda j: 0) if split_in
                else [pl.BlockSpec((bm, k), lambda j, i: (i, 0))])
    in_specs.append(pl.BlockSpec((None, k, bn), lambda j, i: (layer, 0, j),
                                 pipeline_mode=pl.Buffered(w_buffers)))
    args = list(h) if split_in else [h]
    args.append(w)
    res = () if res is None else res if isinstance(res, tuple) else (res,)
    in_specs += (_split_specs(n_p, n_s, bm, bn, lambda j: j) if len(res) == 2
                 else [pl.BlockSpec((bm, bn), lambda j, i: (i, j))] * len(res))
    args += res
    if split_out:
        out_specs = _split_specs(n_p, n_s, bm, bn, lambda j: j)
        out_shape = [jax.ShapeDtypeStruct((n_p, ncols), F32), jax.ShapeDtypeStruct((n_s, ncols), F32)]
    else:
        out_specs = pl.BlockSpec((bm, bn), lambda j, i: (i, j))
        out_shape = jax.ShapeDtypeStruct((m, ncols), F32)
    return pl.pallas_call(
        functools.partial(_mm_kernel, n_res=len(res), split_in=split_in, split_out=split_out, npt=npt),
        grid=(ncols // bn, m // bm),
        in_specs=in_specs,
        out_specs=out_specs,
        out_shape=out_shape,
        scratch_shapes=[pltpu.VMEM((k, bn), BF16)],
        compiler_params=_params("parallel", "arbitrary"),
        name=name,
    )(*args)


def _headnorm(acc, g, scale):
    outs = []
    for c in range(acc.shape[1] // LANES):
        blk = acc[:, c * LANES:(c + 1) * LANES]
        y = blk * lax.rsqrt(jnp.mean(blk * blk, axis=-1, keepdims=True) + EPS)
        outs.append(y * g * scale if scale != 1.0 else y * g)
    return jnp.concatenate(outs, axis=1) if len(outs) > 1 else outs[0]


def _qkv_kernel(h_ref, wq_ref, wk_ref, wv_ref, gq_ref, gk_ref,
                qb_ref, kfp_ref, kfs_ref, kb_ref, vfp_ref, vfs_ref, vb_ref,
                wqb_ref, wkb_ref, wvb_ref, *, scale, npt):
    i = pl.program_id(1)

    @pl.when(i == 0)
    def _():
        wqb_ref[...] = wq_ref[...].astype(BF16)
        wkb_ref[...] = wk_ref[...].astype(BF16)
        wvb_ref[...] = wv_ref[...].astype(BF16)

    h = h_ref[...]
    q = jnp.dot(h, wqb_ref[...], preferred_element_type=F32)
    k = jnp.dot(h, wkb_ref[...], preferred_element_type=F32)
    v = jnp.dot(h, wvb_ref[...], preferred_element_type=F32)
    qb_ref[...] = _headnorm(q, gq_ref[...], scale).astype(BF16)
    kn = _headnorm(k, gk_ref[...], 1.0)
    kb_ref[...] = kn.astype(BF16)
    vb_ref[...] = v.astype(BF16)

    def put(kf_ref, vf_ref):
        for g in range(kf_ref.shape[1]):
            kf_ref[:, g, :] = kn[:, g * LANES:(g + 1) * LANES]
        vf_ref[...] = v

    _on_rows(i, npt, True, put, (kfp_ref, kfs_ref), (vfp_ref, vfs_ref))


def _diff_project(h, w, layer, q_g, k_g, dh, n_p):
    m, d = h.shape
    n_s = m - n_p
    bm, bn = _tile(math.gcd(n_p, n_s), 1024), _tile(d, 512)
    nj = d // bn
    f_specs = _split_specs(n_p, n_s, bm, bn, lambda j: j)
    f_shapes = [jax.ShapeDtypeStruct((n_p, d), F32), jax.ShapeDtypeStruct((n_s, d), F32)]
    npt, nst, ng = n_p // bm, n_s // bm, bn // dh
    k_specs = [pl.BlockSpec((bm, None, ng, dh), lambda j, i: (jnp.minimum(i, npt - 1), j, 0, 0)),
               pl.BlockSpec((bm, None, ng, dh), lambda j, i: (jnp.clip(i - npt, 0, nst - 1), j, 0, 0))]
    k_shapes = [jax.ShapeDtypeStruct((n_p, nj, ng, dh), F32), jax.ShapeDtypeStruct((n_s, nj, ng, dh), F32)]
    o_spec = pl.BlockSpec((bm, bn), lambda j, i: (i, j))
    b_shape = jax.ShapeDtypeStruct((m, d), BF16)
    g_spec = pl.BlockSpec((None, 1, dh), lambda j, i: (layer, 0, 0))

    def w_spec(seg):
        return pl.BlockSpec((None, d, bn), lambda j, i: (layer, 0, seg * nj + j),
                            pipeline_mode=pl.Buffered(1))

    qb, kf_p, kf_s, kb, vf_p, vf_s, vb = pl.pallas_call(
        functools.partial(_qkv_kernel, scale=dh ** -0.5 * LOG2E, npt=n_p // bm),
        grid=(nj, m // bm),
        in_specs=[pl.BlockSpec((bm, d), lambda j, i: (i, 0)), w_spec(0), w_spec(1), w_spec(2),
                  g_spec, g_spec],
        out_specs=[o_spec] + k_specs + [o_spec] + f_specs + [o_spec],
        out_shape=[b_shape] + k_shapes + [b_shape] + f_shapes + [b_shape],
        scratch_shapes=[pltpu.VMEM((d, bn), BF16)] * 3,
        compiler_params=_params("parallel", "arbitrary"),
        name="diff_qkv",
    )(h, w, w, w, _rows3(q_g), _rows3(k_g))
    return qb, (kf_p, kf_s), kb, (vf_p, vf_s), vb


def _ffn_up_kernel(h_ref, wg_ref, wu_ref, cw_ref, cb_ref, st_ref, a_ref, tail_ref,
                   wgb_ref, wub_ref, g_ref, *, n_prompt_tiles, seq):
    i = pl.program_id(1)
    bm, bn = a_ref.shape
    nseg = bm // seq

    @pl.when(i == 0)
    def _():
        wgb_ref[...] = wg_ref[...].astype(BF16)
        wub_ref[...] = wu_ref[...].astype(BF16)
        g_ref[0:8, :] = jnp.zeros((8, bn), F32)

    h = h_ref[...]
    g = jnp.dot(h, wgb_ref[...], preferred_element_type=F32)
    u = jnp.dot(h, wub_ref[...], preferred_element_type=F32)
    g_ref[8:8 + bm, :] = g
    g1 = g_ref[7:7 + bm, :]
    g2 = g_ref[6:6 + bm, :]
    st = st_ref[...]
    p2 = jnp.broadcast_to(st[:, 0:1, :], (nseg, seq, bn)).reshape(bm, bn)
    p1 = jnp.broadcast_to(st[:, 1:2, :], (nseg, seq, bn)).reshape(bm, bn)
    pos = (lax.broadcasted_iota(jnp.int32, (bm, bn), 0) % seq
           + jnp.where(i >= n_prompt_tiles, 0, seq))
    g1 = jnp.where(pos == 0, p1, g1)
    g2 = jnp.where(pos == 0, p2, jnp.where(pos == 1, p1, g2))
    cw = cw_ref[...]
    conv = cb_ref[...] + cw[2:3, :] * g + cw[1:2, :] * g1 + cw[0:1, :] * g2
    a_ref[...] = (conv * jax.nn.sigmoid(conv) * u).astype(BF16)

    for n in range(nseg):
        end = 8 + (n + 1) * seq
        tail_ref[n] = g_ref[end - (CONV_W - 1):end, :]
    g_ref[0:8, :] = g_ref[bm:bm + 8, :]


def _ffn_up(h, w_up, conv_w, conv_b, conv_state, layer, n_prompt_rows, seq):
    m, d = h.shape
    dff = conv_w.shape[-1]
    nb = conv_state.shape[1]
    bm = _tile(math.gcd(n_prompt_rows, nb * seq), 1024)
    bn = _tile(dff, 512)
    assert bm % seq == 0 and (m - n_prompt_rows) == nb * seq
    nseg = bm // seq
    npt = n_prompt_rows // bm
    nj = dff // bn
    return pl.pallas_call(
        functools.partial(_ffn_up_kernel, n_prompt_tiles=npt, seq=seq),
        grid=(nj, m // bm),
        in_specs=[pl.BlockSpec((bm, d), lambda j, i: (i, 0)),
                  pl.BlockSpec((None, d, bn), lambda j, i: (layer, 0, j)),
                  pl.BlockSpec((None, d, bn), lambda j, i: (layer, 0, nj + j)),
                  pl.BlockSpec((None, CONV_W, bn), lambda j, i: (layer, 0, j)),
                  pl.BlockSpec((None, 1, bn), lambda j, i: (layer, 0, j)),
                  pl.BlockSpec((None, nseg, CONV_W - 1, bn),
                               lambda j, i: (layer, jnp.maximum(i - npt, 0), 0, j))],
        out_specs=[pl.BlockSpec((bm, bn), lambda j, i: (i, j)),
                   pl.BlockSpec((nseg, CONV_W - 1, bn), lambda j, i: (i, 0, j))],
        out_shape=[jax.ShapeDtypeStruct((m, dff), BF16),
                   jax.ShapeDtypeStruct((m // seq, CONV_W - 1, dff), F32)],
        scratch_shapes=[pltpu.VMEM((d, bn), BF16), pltpu.VMEM((d, bn), BF16),
                        pltpu.VMEM((bm + 8, bn), F32)],
        compiler_params=_params("parallel", "arbitrary"),
        name="ffn_up",
    )(h, w_up, w_up, conv_w, _rows3(conv_b), conv_state)


def _hgrn_in_kernel(h_ref, wq_ref, wf_ref, wv_ref, wg_ref, lbp_ref,
                    q_ref, b2_ref, k_ref, qe_ref, kd_ref, v_ref, gate_ref,
                    wqb_ref, wfb_ref, wvb_ref, wgb_ref, *, layer):
    @pl.when(pl.program_id(1) == 0)
    def _():
        wqb_ref[...] = wq_ref[...].astype(BF16)
        wfb_ref[...] = wf_ref[...].astype(BF16)
        wvb_ref[...] = wv_ref[...].astype(BF16)
        wgb_ref[...] = wg_ref[...].astype(BF16)

    h = h_ref[...]
    rows, width = q_ref.shape
    q = jnp.dot(h, wqb_ref[...], preferred_element_type=F32)
    fz = jnp.dot(h, wfb_ref[...], preferred_element_type=F32)
    v_ref[...] = jnp.dot(h, wvb_ref[...], preferred_element_type=F32).astype(BF16)
    gate_ref[...] = jnp.dot(h, wgb_ref[...], preferred_element_type=F32)

    lbp = lbp_ref[...]
    e = jnp.exp(lbp - jnp.max(lbp, axis=0, keepdims=True))
    lb = jnp.sum(e[0:layer + 1], axis=0, keepdims=True) / jnp.sum(e, axis=0, keepdims=True)

    f = lb + (1.0 - lb) * jax.nn.sigmoid(fz)
    kk = 1.0 - f
    pos = lax.broadcasted_iota(jnp.int32, (rows, width), 0) % HGRN_BLOCK
    b = jnp.log(f)
    sh = 1
    while sh < HGRN_BLOCK:
        b = b + jnp.where(pos >= sh, pltpu.roll(b, sh, 0), 0.0)
        sh *= 2
    b3 = b.reshape(rows // HGRN_BLOCK, HGRN_BLOCK, width)
    b_last = jnp.broadcast_to(b3[:, HGRN_BLOCK - 1:, :], b3.shape).reshape(rows, width)
    q_ref[...] = q
    b2_ref[...] = b * LOG2E
    k_ref[...] = kk
    qe_ref[...] = (q * jnp.exp(b)).astype(BF16)
    kd_ref[...] = (kk * jnp.exp(b_last - b)).astype(BF16)


def _hgrn_project(h, w, lower_bounds, layer, jl):
    m, d = h.shape
    dk = w.shape[2] // 4
    bm, bn = _tile(m, 1024), _tile(dk, 256)
    nj = dk // bn

    def w_spec(seg):
        return pl.BlockSpec((None, d, bn), lambda j, i: (jl, 0, seg * nj + j))

    o_spec = pl.BlockSpec((bm, bn), lambda j, i: (i, j))
    f32_out, bf16_out = jax.ShapeDtypeStruct((m, dk), F32), jax.ShapeDtypeStruct((m, dk), BF16)
    return pl.pallas_call(
        functools.partial(_hgrn_in_kernel, layer=layer),
        grid=(nj, m // bm),
        in_specs=[pl.BlockSpec((bm, d), lambda j, i: (i, 0)), w_spec(0), w_spec(1), w_spec(2), w_spec(3),
                  pl.BlockSpec((lower_bounds.shape[0], bn), lambda j, i: (0, j))],
        out_specs=[o_spec] * 7,
        out_shape=[f32_out, f32_out, f32_out, bf16_out, bf16_out, bf16_out, f32_out],
        scratch_shapes=[pltpu.VMEM((d, bn), BF16)] * 4,
        compiler_params=_params("parallel", "arbitrary"),
        name="hgrn_in",
    )(h, w, w, w, w, lower_bounds)


def _hgrn_kernel(q_ref, b_ref, k_ref, qe_ref, kd_ref, v_ref, gate_ref, gout_ref, s0_ref, o_ref, sout_ref,
                 st_ref, oacc_ref, *, seq_blocks):
    c = pl.program_id(1)
    t_rows = q_ref.shape[0]
    nblk = t_rows // HGRN_BLOCK
    carry = seq_blocks is None

    if carry:
        @pl.when(c == 0)
        def _():
            st_ref[...] = jnp.zeros_like(st_ref)

    half = HGRN_BLOCK // 2
    row = lax.broadcasted_iota(jnp.int32, (half, LANES), 0)
    lane = lax.broadcasted_iota(jnp.int32, (half, LANES), 1)

    def scores(r0):
        lo, hi = pl.ds(r0, half), pl.ds(r0 + half, half)
        b_lo, b_hi, q_lo, q_hi = b_ref[lo, :], b_ref[hi, :], q_ref[lo, :], q_ref[hi, :]
        b_mid = b_ref[pl.ds(r0 + half - 1, 1), :]
        q_in = (q_hi * jnp.exp2(b_hi - b_mid)).astype(BF16)
        k_out = (k_ref[lo, :] * jnp.exp2(b_mid - b_lo)).astype(BF16)
        k_out = jnp.concatenate([k_out, jnp.zeros((LANES - half, LANES), BF16)], axis=0)
        sc_hi = lax.dot_general(q_in, k_out, (((1,), (1,)), ((), ())), preferred_element_type=F32)
        sc_lo = jnp.zeros((half, LANES), F32)
        for s in range(HGRN_BLOCK):
            bs, ks = b_ref[pl.ds(r0 + s, 1), :], k_ref[pl.ds(r0 + s, 1), :]
            if s < half:
                col_lo = jnp.sum(jnp.exp2(b_lo - bs) * (q_lo * ks), axis=-1, keepdims=True)
                sc_lo = jnp.where(lane == s, col_lo, sc_lo)
            else:
                col_hi = jnp.sum(jnp.exp2(b_hi - bs) * (q_hi * ks), axis=-1, keepdims=True)
                sc_hi = jnp.where(lane == s, col_hi, sc_hi)
        sc = jnp.concatenate([jnp.where(row >= lane, sc_lo, 0.0),
                              jnp.where(row + half >= lane, sc_hi, 0.0)], axis=0)
        return sc[:, 0:HGRN_BLOCK].astype(BF16)

    group = min(nblk, HGRN_GROUP) if carry else nblk

    def blocks(jg, st):
        rows, sc, vb, decay, upd = [], [], [], [], []
        for g in range(group):
            r0 = pl.multiple_of((jg * group + g) * HGRN_BLOCK, HGRN_BLOCK)
            rows.append(pl.ds(r0, HGRN_BLOCK))
            vb.append(v_ref[rows[g], :])
            sc.append(scores(r0))
            decay.append(jnp.exp2(b_ref[pl.ds(r0 + HGRN_BLOCK - 1, 1), :]))
            upd.append(lax.dot_general(vb[g], kd_ref[rows[g], :], (((0,), (0,)), ((), ())),
                                       preferred_element_type=F32))
        states = []
        for g in range(group):
            if not carry and g % seq_blocks == 0:
                st = s0_ref[g // seq_blocks].T
            states.append(st.astype(BF16))
            st = st * decay[g] + upd[g]
            if not carry and (g + 1) % seq_blocks == 0:
                sout_ref[g // seq_blocks] = st.T
        for g in range(group):
            o = lax.dot_general(qe_ref[rows[g], :], states[g], (((1,), (1,)), ((), ())),
                                preferred_element_type=F32)
            oacc_ref[rows[g], :] = o + jnp.dot(sc[g], vb[g], preferred_element_type=F32)
        return st

    if carry:
        st_ref[...] = lax.fori_loop(0, nblk // group, blocks, st_ref[...])
    else:
        blocks(0, None)

    o = oacc_ref[...]
    y = o * lax.rsqrt(jnp.mean(o * o, axis=-1, keepdims=True) + EPS) * gout_ref[...]
    gate = gate_ref[...]
    o_ref[...] = (y * (gate * jax.nn.sigmoid(gate))).astype(BF16)

    if carry:
        @pl.when(c == pl.num_programs(1) - 1)
        def _():
            sout_ref[...] = st_ref[...].T


def _hgrn_scan(ops, out_norm, state, jl, n_prompt_rows, seq):
    nb, nh, dk, dv = state.shape[1:]
    assert dk == LANES and dv == LANES
    scratch = lambda t: [pltpu.VMEM((dv, dk), F32), pltpu.VMEM((t, dv), F32)]

    def specs(t, row_of):
        return ([pl.BlockSpec((t, LANES), lambda h, c: (row_of(c), h))] * len(ops)
                + [pl.BlockSpec((None, 1, dv), lambda h, c: (jl, 0, 0))])

    out_norm = _rows3(out_norm)
    tp = _tile(n_prompt_rows, HGRN_TILE)
    o_p, s_p = pl.pallas_call(
        functools.partial(_hgrn_kernel, seq_blocks=None),
        grid=(nh, n_prompt_rows // tp),
        in_specs=specs(tp, lambda c: c) + [pl.BlockSpec((None, None, None, dk, dv),
                                                        lambda h, c: (jl, 0, h, 0, 0))],
        out_specs=[pl.BlockSpec((tp, dv), lambda h, c: (c, h)),
                   pl.BlockSpec((None, dk, dv), lambda h, c: (h, 0, 0))],
        out_shape=[jax.ShapeDtypeStruct((n_prompt_rows, nh * dv), BF16),
                   jax.ShapeDtypeStruct((nh, dk, dv), F32)],
        scratch_shapes=scratch(tp),
        compiler_params=_params("parallel", "arbitrary"),
        name="hgrn_prompt",
    )(*ops, out_norm, state)
    ns = _tile(nb, max(1, HGRN_GROUP * HGRN_BLOCK // seq))
    ts = ns * seq
    assert n_prompt_rows % ts == 0 and seq % HGRN_BLOCK == 0
    r0 = n_prompt_rows // ts
    o_s, s_s = pl.pallas_call(
        functools.partial(_hgrn_kernel, seq_blocks=seq // HGRN_BLOCK),
        grid=(nh, nb // ns),
        in_specs=specs(ts, lambda c: r0 + c) + [pl.BlockSpec((None, ns, None, dk, dv),
                                                              lambda h, c: (jl, c, h, 0, 0))],
        out_specs=[pl.BlockSpec((ts, dv), lambda h, c: (c, h)),
                   pl.BlockSpec((ns, None, dk, dv), lambda h, c: (c, h, 0, 0))],
        out_shape=[jax.ShapeDtypeStruct((nb * seq, nh * dv), BF16),
                   jax.ShapeDtypeStruct((nb, nh, dk, dv), F32)],
        scratch_shapes=scratch(ts),
        compiler_params=_params("parallel", "arbitrary"),
        name="hgrn_sample",
    )(*ops, out_norm, state)
    return (o_p, o_s), s_p, s_s


def _lanes(x, n):
    return x[:, :n] if n <= LANES else jnp.concatenate([x] * (n // LANES), axis=1)


def _scores(q, k):
    return lax.dot_general(q, k, (((1,), (1,)), ((), ())), preferred_element_type=F32)


def _softmax_step(c, s, m_ref, l_ref, mask=None, rows=slice(None)):
    if mask is not None:
        s = jnp.where(mask, s, -jnp.inf)
    m_prev = m_ref[c, rows]
    m_new = jnp.maximum(m_prev, jnp.max(s, axis=-1, keepdims=True))
    alpha = jnp.exp2(m_prev - m_new)
    p = jnp.exp2(s - _lanes(m_new, s.shape[1]))
    l_ref[c, rows] = alpha * l_ref[c, rows] + jnp.sum(p, axis=-1, keepdims=True)
    m_ref[c, rows] = m_new
    return alpha, p.astype(BF16)


def _pv_step(c, alpha, p, v, acc_ref, rows=slice(None)):
    acc_ref[c, rows] = (_lanes(alpha, v.shape[1]) * acc_ref[c, rows]
                        + jnp.dot(p, v, preferred_element_type=F32))


def _diff_finish(a0, l0, a1, l1, lam_refs, subln, lam_init):
    lq1, lk1, lq2, lk2 = [r[...] for r in lam_refs]
    lam = (jnp.exp(jnp.sum(lq1 * lk1, axis=-1, keepdims=True))
           - jnp.exp(jnp.sum(lq2 * lk2, axis=-1, keepdims=True)) + lam_init)
    o = a0 * _lanes(1.0 / l0, a0.shape[1]) - lam * (a1 * _lanes(1.0 / l1, a1.shape[1]))
    y = o * lax.rsqrt(jnp.mean(o * o, axis=-1, keepdims=True) + EPS)
    return (y * subln * (1.0 - lam_init)).astype(BF16)


def _attn_prompt_kernel(it_ref, jt_ref, q_ref, k_ref, v_ref, lq1, lk1, lq2, lk2, sub_ref, o_ref,
                        m_ref, l_ref, acc_ref, *, lam_init):
    p = pl.program_id(1)
    i, j = it_ref[p], jt_ref[p]
    bq, bk = q_ref.shape[0], k_ref.shape[0]
    dh = q_ref.shape[1] // 2

    @pl.when(j == 0)
    def _():
        m_ref[...] = jnp.full(m_ref.shape, -jnp.inf, F32)
        l_ref[...] = jnp.zeros(l_ref.shape, F32)
        acc_ref[...] = jnp.zeros(acc_ref.shape, F32)

    def steps(*parts):
        work = [(c, mask, rows, keys,
                 _scores(q_ref[rows, c * dh:(c + 1) * dh], k_ref[keys, c * dh:(c + 1) * dh]))
                for mask, rows, keys in parts for c in range(2)]
        for c, mask, rows, keys, s in work:
            v = v_ref[keys, :]
            alpha, p = _softmax_step(c, s, m_ref, l_ref, mask, rows)
            _pv_step(c, alpha, p, v, acc_ref, rows)

    @pl.when(j < i)
    def _():
        steps((None, slice(None), slice(None)))

    @pl.when(j == i)
    def _():
        half = bq // 2

        def chunk_mask(n_keys, first_row):
            qpos = first_row + lax.broadcasted_iota(jnp.int32, (half, n_keys), 0)
            kpos = lax.broadcasted_iota(jnp.int32, (half, n_keys), 1)
            return kpos < (qpos // CHUNK + 1) * CHUNK

        steps((chunk_mask(half, 0), slice(0, half), slice(0, half)),
              (chunk_mask(bk, half), slice(half, bq), slice(None)))
        o_ref[...] = _diff_finish(acc_ref[0], l_ref[0], acc_ref[1], l_ref[1],
                                  (lq1, lk1, lq2, lk2), sub_ref[...], lam_init)


def _attn_prompt(qb, kb, vb, lams, subln, jl, lam_init, n_rows, nh):
    dv = qb.shape[1] // nh
    bq = bk = _tile(n_rows, ATTN_TILE)
    assert (bq // 2) % CHUNK == 0
    nq = n_rows // bq
    pairs = [(i, j) for i in range(nq) for j in range(i + 1)]
    it = jnp.asarray([p[0] for p in pairs], jnp.int32)
    jt = jnp.asarray([p[1] for p in pairs], jnp.int32)
    lam_spec = pl.BlockSpec((None, 1, dv // 2), lambda h, p, it, jt: (jl, 0, 0))
    return pl.pallas_call(
        functools.partial(_attn_prompt_kernel, lam_init=lam_init),
        grid_spec=pltpu.PrefetchScalarGridSpec(
            num_scalar_prefetch=2,
            grid=(nh, len(pairs)),
            in_specs=[pl.BlockSpec((bq, dv), lambda h, p, it, jt: (it[p], h)),
                      pl.BlockSpec((bk, dv), lambda h, p, it, jt: (jt[p], h)),
                      pl.BlockSpec((bk, dv), lambda h, p, it, jt: (jt[p], h)),
                      lam_spec, lam_spec, lam_spec, lam_spec,
                      pl.BlockSpec((None, 1, dv), lambda h, p, it, jt: (jl, 0, 0))],
            out_specs=pl.BlockSpec((bq, dv), lambda h, p, it, jt: (it[p], h)),
            scratch_shapes=[pltpu.VMEM((2, bq, LANES), F32), pltpu.VMEM((2, bq, LANES), F32),
                            pltpu.VMEM((2, bq, dv), F32)]),
        out_shape=jax.ShapeDtypeStruct((n_rows, nh * dv), BF16),
        compiler_params=_params("parallel", "arbitrary"),
        name="attn_prompt",
    )(it, jt, qb, kb, vb, *[_rows3(a) for a in lams], _rows3(subln))


def _attn_sample_kernel(q_ref, ck_ref, cv_ref, kn_ref, vn_ref, lq1, lk1, lq2, lk2, sub_ref, o_ref,
                        m_ref, l_ref, acc_ref, *, lam_init, nh):
    j = pl.program_id(1)
    last = pl.num_programs(1) - 1
    dv = q_ref.shape[1] // nh
    dh = dv // 2

    @pl.when(j == 0)
    def _():
        m_ref[...] = jnp.full(m_ref.shape, -jnp.inf, F32)
        l_ref[...] = jnp.zeros(l_ref.shape, F32)
        acc_ref[...] = jnp.zeros(acc_ref.shape, F32)

    def q_of(h, c):
        return q_ref[:, h * dv + c * dh:h * dv + (c + 1) * dh]

    def step(k_of, v_of, scores_first):
        ss = [_scores(q_of(h, c), k_of(h, c)) for h in range(nh) for c in range(2)] if scores_first else None
        for h in range(nh):
            v = v_of(h)
            for c in range(2):
                s = ss[2 * h + c] if scores_first else _scores(q_of(h, c), k_of(h, c))
                a, p = _softmax_step(2 * h + c, s, m_ref, l_ref)
                _pv_step(2 * h + c, a, p, v, acc_ref)

    @pl.when(j < last)
    def _():
        ng = 2 * nh
        tok = LANES // ng
        i_out = lax.broadcasted_iota(jnp.int32, (LANES, LANES), 0)
        i_in = lax.broadcasted_iota(jnp.int32, (LANES, LANES), 1)
        perm = jnp.where(i_in == (i_out % tok) * ng + i_out // tok, 1.0, 0.0).astype(BF16)

        def regroup(src_ref):
            n_slab = src_ref.shape[0] // LANES
            wide = jnp.concatenate([src_ref[n * LANES:(n + 1) * LANES, :].astype(BF16)
                                    for n in range(n_slab)], axis=1)
            y = jnp.dot(perm, wide, preferred_element_type=F32)
            return [jnp.concatenate([y[g * tok:(g + 1) * tok, n * LANES:(n + 1) * LANES]
                                     for n in range(n_slab)], axis=0).astype(BF16) for g in range(ng)]

        kg, vg = regroup(ck_ref), regroup(cv_ref)
        step(lambda h, c: kg[2 * h + c],
             lambda h: jnp.concatenate([vg[h], vg[nh + h]], axis=1), False)

    @pl.when(j == last)
    def _():
        step(lambda h, c: kn_ref[:, h * dv + c * dh:h * dv + (c + 1) * dh],
             lambda h: vn_ref[:, h * dv:(h + 1) * dv], True)
        for h in range(nh):
            o_ref[:, h * dv:(h + 1) * dv] = _diff_finish(
                acc_ref[2 * h], l_ref[2 * h], acc_ref[2 * h + 1], l_ref[2 * h + 1],
                (lq1, lk1, lq2, lk2), sub_ref[...], lam_init)


def _attn_sample(qb, kb, vb, cache_k, cache_v, lams, subln, jl, lam_init, n_prompt_rows, seq, nh):
    d = qb.shape[1]
    dv = d // nh
    nb, past = cache_k.shape[1], cache_k.shape[2]
    bk = _tile(past, 512)
    nkc = past // bk
    r0 = n_prompt_rows // seq
    new_spec = pl.BlockSpec((seq, d), lambda b, j: (r0 + b, 0))
    nl = cache_k.shape[0]
    ck = cache_k.reshape(nl, nb, past * nh * 2, dv // 2)
    cv = cache_v.reshape(nl, nb, past, nh, 2, dv // 2).transpose(0, 1, 2, 4, 3, 5).reshape(ck.shape)
    cache_spec = pl.BlockSpec((None, None, bk * nh * 2, dv // 2),
                              lambda b, j: (jl, b, jnp.minimum(j, nkc - 1), 0))
    lam_spec = pl.BlockSpec((None, 1, dv // 2), lambda b, j: (jl, 0, 0))
    return pl.pallas_call(
        functools.partial(_attn_sample_kernel, lam_init=lam_init, nh=nh),
        grid=(nb, nkc + 1),
        in_specs=[new_spec, cache_spec, cache_spec, new_spec, new_spec,
                  lam_spec, lam_spec, lam_spec, lam_spec,
                  pl.BlockSpec((None, 1, dv), lambda b, j: (jl, 0, 0))],
        out_specs=pl.BlockSpec((seq, d), lambda b, j: (b, 0)),
        out_shape=jax.ShapeDtypeStruct((nb * seq, d), BF16),
        scratch_shapes=[pltpu.VMEM((2 * nh, seq, LANES), F32), pltpu.VMEM((2 * nh, seq, LANES), F32),
                        pltpu.VMEM((2 * nh, seq, dv), F32)],
        compiler_params=_params("parallel", "arbitrary"),
        name="attn_sample",
    )(qb, ck, cv, kb, vb, *[_rows3(a) for a in lams], _rows3(subln))


def kernel(x_prompt, x_sample, state_hgrn, cache_k, cache_v, state_ffn_conv, norm_mix, norm_ffn, hgrn_lower_bounds, w_hgrn_in, w_hgrn_out, hgrn_out_norm, w_diff_in, w_diff_out, diff_q_norm, diff_k_norm, diff_lambda_q1, diff_lambda_k1, diff_lambda_q2, diff_lambda_k2, diff_subln, w_ffn_up, ffn_conv_w, ffn_conv_b, w_ffn_down):
    bp, seq_p, d = x_prompt.shape
    nb, seq_s, _ = x_sample.shape
    assert bp == 1 and seq_s == CHUNK
    depth = norm_mix.shape[0]
    n_mixers = 2
    n_p = bp * seq_p
    diff_heads, dh = cache_k.shape[3], cache_k.shape[5]
    x = (x_prompt.reshape(n_p, d), x_sample.reshape(nb * seq_s, d))

    hgrn_p, hgrn_s, kfs, vfs, tails = [], [], [], [], []
    for i in range(depth):
        jl = i // n_mixers
        h = _rms_norm_bf16(x, norm_mix, i)
        if i % n_mixers == 0:
            ops = _hgrn_project(h, w_hgrn_in, hgrn_lower_bounds, i, jl)
            o, s_p, s_s = _hgrn_scan(ops, hgrn_out_norm, state_hgrn, jl, n_p, seq_s)
            hgrn_p.append(s_p[None])
            hgrn_s.append(s_s)
            x = _matmul(o, w_hgrn_out, jl, n_p, res=x, name="hgrn_out")
        else:
            lam_init = 0.8 - 0.6 * math.exp(-0.3 * i)
            lams = (diff_lambda_q1, diff_lambda_k1, diff_lambda_q2, diff_lambda_k2)
            qb, kf, kb, vf, vb = _diff_project(h, w_diff_in, jl, diff_q_norm, diff_k_norm, dh, n_p)
            o_p = _attn_prompt(qb, kb, vb, lams, diff_subln, jl, lam_init, n_p, diff_heads)
            o_s = _attn_sample(qb, kb, vb, cache_k, cache_v, lams, diff_subln, jl, lam_init,
                               n_p, seq_s, diff_heads)
            kfs.append(kf)
            vfs.append(vf)
            x = _matmul((o_p, o_s), w_diff_out, jl, n_p, res=x, name="diff_out")
        h = _rms_norm_bf16(x, norm_ffn, i)
        act, tail = _ffn_up(h, w_ffn_up, ffn_conv_w, ffn_conv_b, state_ffn_conv, i, n_p, seq_s)
        tails.append(tail)
        x = _matmul(act, w_ffn_down, i, n_p, res=x, split_out=i == depth - 1, bm=512, bn=1024,
                    w_buffers=1, name="ffn_down")

    x_p, x_s = x
    tail = jnp.stack(tails)
    seg_p = n_p // seq_s
    stack = lambda pairs, k: jnp.stack([p[k] for p in pairs])
    return (x_p.reshape(bp, seq_p, d),
            x_s.reshape(nb, seq_s, d),
            jnp.stack(hgrn_p),
            jnp.stack(hgrn_s),
            stack(kfs, 0).reshape(-1, bp, seq_p, diff_heads, 2, dh),
            stack(vfs, 0).reshape(-1, bp, seq_p, diff_heads, 2 * dh),
            stack(kfs, 1).reshape(-1, nb, seq_s, diff_heads, 2, dh),
            stack(vfs, 1).reshape(-1, nb, seq_s, diff_heads, 2 * dh),
            tail[:, seg_p - 1][:, None],
            tail[:, seg_p:])
```

```python
import functools
import math

import jax
import jax.numpy as jnp
from jax import lax
from jax.experimental import pallas as pl
from jax.experimental.pallas import tpu as pltpu

EPS = 1e-6
LOG2E = math.log2(math.e)
CHUNK = 64
HGRN_BLOCK = 16
ATTN_TILE = 1024
HGRN_TILE = 2048
HGRN_GROUP = 32
CONV_W = 3
LANES = 128
V7X_VMEM_BYTES = 64 * 1024 * 1024
VMEM_LIMIT = V7X_VMEM_BYTES - 8 * 1024 * 1024

F32 = jnp.float32
BF16 = jnp.bfloat16


def _params(*sem):
    return pltpu.CompilerParams(dimension_semantics=sem, vmem_limit_bytes=VMEM_LIMIT)


def _rows3(a):
    return a.reshape(a.shape[0], 1, a.shape[1])


def _tile(n, pref):
    t = min(n, pref)
    while n % t:
        t //= 2
    return t


def _split_specs(n_p, n_s, bm, bn, col):
    npt, nst = n_p // bm, n_s // bm
    return [pl.BlockSpec((bm, bn), lambda j, i: (jnp.minimum(i, npt - 1), col(j))),
            pl.BlockSpec((bm, bn), lambda j, i: (jnp.clip(i - npt, 0, nst - 1), col(j)))]


def _on_rows(i, npt, split, fn, *ref_pairs):
    if not split:
        fn(*[p[0] for p in ref_pairs])
        return
    pl.when(i < npt)(lambda: fn(*[p[0] for p in ref_pairs]))
    pl.when(i >= npt)(lambda: fn(*[p[-1] for p in ref_pairs]))


def _norm_kernel(*refs, npt):
    *x_refs, g_ref, o_ref = refs

    def run(x_ref):
        x = x_ref[...]
        y = x * lax.rsqrt(jnp.mean(x * x, axis=-1, keepdims=True) + EPS)
        o_ref[...] = (y * g_ref[...]).astype(o_ref.dtype)

    _on_rows(pl.program_id(0), npt, len(x_refs) == 2, run, x_refs)


def _rms_norm_bf16(x, gains, layer):
    xs = x if isinstance(x, tuple) else (x,)
    d = xs[0].shape[1]
    m = sum(a.shape[0] for a in xs)
    bm = _tile(math.gcd(*[a.shape[0] for a in xs]), 512)
    npt = xs[0].shape[0] // bm
    if len(xs) == 2:
        nst = xs[1].shape[0] // bm
        x_specs = [pl.BlockSpec((bm, d), lambda i: (jnp.minimum(i, npt - 1), 0)),
                   pl.BlockSpec((bm, d), lambda i: (jnp.clip(i - npt, 0, nst - 1), 0))]
    else:
        x_specs = [pl.BlockSpec((bm, d), lambda i: (i, 0))]
    return pl.pallas_call(
        functools.partial(_norm_kernel, npt=npt),
        grid=(m // bm,),
        in_specs=x_specs + [pl.BlockSpec((None, 1, d), lambda i: (layer, 0, 0))],
        out_specs=pl.BlockSpec((bm, d), lambda i: (i, 0)),
        out_shape=jax.ShapeDtypeStruct((m, d), BF16),
        compiler_params=_params("parallel"),
        name="rms_norm",
    )(*xs, _rows3(gains))


def _mm_kernel(*refs, n_res, split_in, split_out, npt):
    refs = list(refs)
    h_refs = [refs.pop(0) for _ in range(2 if split_in else 1)]
    w_ref = refs.pop(0)
    r_refs = [refs.pop(0) for _ in range(n_res)] or [None]
    o_refs = [refs.pop(0) for _ in range(2 if split_out else 1)]
    wb_ref, = refs
    i = pl.program_id(1)

    @pl.when(i == 0)
    def _():
        wb_ref[...] = w_ref[...].astype(BF16)

    def run(h_ref, r_ref, o_ref):
        acc = jnp.dot(h_ref[...], wb_ref[...], preferred_element_type=F32)
        o_ref[...] = acc if r_ref is None else r_ref[...] + acc

    _on_rows(i, npt, split_in or split_out or n_res == 2, run, h_refs, r_refs, o_refs)


def _matmul(h, w, layer, n_p, *, res=None, split_out=False, bm=1024, bn=512, w_buffers=2, name="matmul"):
    split_in = isinstance(h, tuple)
    k, ncols = w.shape[1], w.shape[2]
    m = sum(a.shape[0] for a in h) if split_in else h.shape[0]
    n_s = m - n_p
    bm, bn = _tile(math.gcd(n_p, n_s), bm), _tile(ncols, bn)
    npt = n_p // bm
    in_specs = (_split_specs(n_p, n_s, bm, k, lambda j: 0) if split_in
                else [pl.BlockSpec((bm, k), lambda j, i: (i, 0))])
    in_specs.append(pl.BlockSpec((None, k, bn), lambda j, i: (layer, 0, j),
                                 pipeline_mode=pl.Buffered(w_buffers)))
    args = list(h) if split_in else [h]
    args.append(w)
    res = () if res is None else res if isinstance(res, tuple) else (res,)
    in_specs += (_split_specs(n_p, n_s, bm, bn, lambda j: j) if len(res) == 2
                 else [pl.BlockSpec((bm, bn), lambda j, i: (i, j))] * len(res))
    args += res
    if split_out:
        out_specs = _split_specs(n_p, n_s, bm, bn, lambda j: j)
        out_shape = [jax.ShapeDtypeStruct((n_p, ncols), F32), jax.ShapeDtypeStruct((n_s, ncols), F32)]
    else:
        out_specs = pl.BlockSpec((bm, bn), lambda j, i: (i, j))
        out_shape = jax.ShapeDtypeStruct((m, ncols), F32)
    return pl.pallas_call(
        functools.partial(_mm_kernel, n_res=len(res), split_in=split_in, split_out=split_out, npt=npt),
        grid=(ncols // bn, m // bm),
        in_specs=in_specs,
        out_specs=out_specs,
        out_shape=out_shape,
        scratch_shapes=[pltpu.VMEM((k, bn), BF16)],
        compiler_params=_params("parallel", "arbitrary"),
        name=name,
    )(*args)


def _headnorm(acc, g, scale):
    outs = []
    for c in range(acc.shape[1] // LANES):
        blk = acc[:, c * LANES:(c + 1) * LANES]
        y = blk * lax.rsqrt(jnp.mean(blk * blk, axis=-1, keepdims=True) + EPS)
        outs.append(y * g * scale if scale != 1.0 else y * g)
    return jnp.concatenate(outs, axis=1) if len(outs) > 1 else outs[0]


def _qkv_kernel(h_ref, wq_ref, wk_ref, wv_ref, gq_ref, gk_ref,
                qb_ref, kfp_ref, kfs_ref, kb_ref, vfp_ref, vfs_ref, vb_ref,
                wqb_ref, wkb_ref, wvb_ref, *, scale, npt):
    i = pl.program_id(1)

    @pl.when(i == 0)
    def _():
        wqb_ref[...] = wq_ref[...].astype(BF16)
        wkb_ref[...] = wk_ref[...].astype(BF16)
        wvb_ref[...] = wv_ref[...].astype(BF16)

    h = h_ref[...]
    q = jnp.dot(h, wqb_ref[...], preferred_element_type=F32)
    k = jnp.dot(h, wkb_ref[...], preferred_element_type=F32)
    v = jnp.dot(h, wvb_ref[...], preferred_element_type=F32)
    qb_ref[...] = _headnorm(q, gq_ref[...], scale).astype(BF16)
    kn = _headnorm(k, gk_ref[...], 1.0)
    kb_ref[...] = kn.astype(BF16)
    vb_ref[...] = v.astype(BF16)

    def put(kf_ref, vf_ref):
        for g in range(kf_ref.shape[1]):
            kf_ref[:, g, :] = kn[:, g * LANES:(g + 1) * LANES]
        vf_ref[...] = v

    _on_rows(i, npt, True, put, (kfp_ref, kfs_ref), (vfp_ref, vfs_ref))


def _diff_project(h, w, layer, q_g, k_g, dh, n_p):
    m, d = h.shape
    n_s = m - n_p
    bm, bn = _tile(math.gcd(n_p, n_s), 1024), _tile(d, 512)
    nj = d // bn
    f_specs = _split_specs(n_p, n_s, bm, bn, lambda j: j)
    f_shapes = [jax.ShapeDtypeStruct((n_p, d), F32), jax.ShapeDtypeStruct((n_s, d), F32)]
    npt, nst, ng = n_p // bm, n_s // bm, bn // dh
    k_specs = [pl.BlockSpec((bm, None, ng, dh), lambda j, i: (jnp.minimum(i, npt - 1), j, 0, 0)),
               pl.BlockSpec((bm, None, ng, dh), lambda j, i: (jnp.clip(i - npt, 0, nst - 1), j, 0, 0))]
    k_shapes = [jax.ShapeDtypeStruct((n_p, nj, ng, dh), F32), jax.ShapeDtypeStruct((n_s, nj, ng, dh), F32)]
    o_spec = pl.BlockSpec((bm, bn), lambda j, i: (i, j))
    b_shape = jax.ShapeDtypeStruct((m, d), BF16)
    g_spec = pl.BlockSpec((None, 1, dh), lambda j, i: (layer, 0, 0))

    def w_spec(seg):
        return pl.BlockSpec((None, d, bn), lambda j, i: (layer, 0, seg * nj + j),
                            pipeline_mode=pl.Buffered(1))

    qb, kf_p, kf_s, kb, vf_p, vf_s, vb = pl.pallas_call(
        functools.partial(_qkv_kernel, scale=dh ** -0.5 * LOG2E, npt=n_p // bm),
        grid=(nj, m // bm),
        in_specs=[pl.BlockSpec((bm, d), lambda j, i: (i, 0)), w_spec(0), w_spec(1), w_spec(2),
                  g_spec, g_spec],
        out_specs=[o_spec] + k_specs + [o_spec] + f_specs + [o_spec],
        out_shape=[b_shape] + k_shapes + [b_shape] + f_shapes + [b_shape],
        scratch_shapes=[pltpu.VMEM((d, bn), BF16)] * 3,
        compiler_params=_params("parallel", "arbitrary"),
        name="diff_qkv",
    )(h, w, w, w, _rows3(q_g), _rows3(k_g))
    return qb, (kf_p, kf_s), kb, (vf_p, vf_s), vb


def _ffn_up_kernel(h_ref, wg_ref, wu_ref, cw_ref, cb_ref, st_ref, a_ref, tail_ref,
                   wgb_ref, wub_ref, g_ref, *, n_prompt_tiles, seq):
    i = pl.program_id(1)
    bm, bn = a_ref.shape
    nseg = bm // seq

    @pl.when(i == 0)
    def _():
        wgb_ref[...] = wg_ref[...].astype(BF16)
        wub_ref[...] = wu_ref[...].astype(BF16)
        g_ref[0:8, :] = jnp.zeros((8, bn), F32)

    h = h_ref[...]
    g = jnp.dot(h, wgb_ref[...], preferred_element_type=F32)
    u = jnp.dot(h, wub_ref[...], preferred_element_type=F32)
    g_ref[8:8 + bm, :] = g
    g1 = g_ref[7:7 + bm, :]
    g2 = g_ref[6:6 + bm, :]
    st = st_ref[...]
    p2 = jnp.broadcast_to(st[:, 0:1, :], (nseg, seq, bn)).reshape(bm, bn)
    p1 = jnp.broadcast_to(st[:, 1:2, :], (nseg, seq, bn)).reshape(bm, bn)
    pos = (lax.broadcasted_iota(jnp.int32, (bm, bn), 0) % seq
           + jnp.where(i >= n_prompt_tiles, 0, seq))
    g1 = jnp.where(pos == 0, p1, g1)
    g2 = jnp.where(pos == 0, p2, jnp.where(pos == 1, p1, g2))
    cw = cw_ref[...]
    conv = cb_ref[...] + cw[2:3, :] * g + cw[1:2, :] * g1 + cw[0:1, :] * g2
    a_ref[...] = (conv * jax.nn.sigmoid(conv) * u).astype(BF16)

    for n in range(nseg):
        end = 8 + (n + 1) * seq
        tail_ref[n] = g_ref[end - (CONV_W - 1):end, :]
    g_ref[0:8, :] = g_ref[bm:bm + 8, :]


def _ffn_up(h, w_up, conv_w, conv_b, conv_state, layer, n_prompt_rows, seq):
    m, d = h.shape
    dff = conv_w.shape[-1]
    nb = conv_state.shape[1]
    bm = _tile(math.gcd(n_prompt_rows, nb * seq), 1024)
    bn = _tile(dff, 512)
    assert bm % seq == 0 and (m - n_prompt_rows) == nb * seq
    nseg = bm // seq
    npt = n_prompt_rows // bm
    nj = dff // bn
    return pl.pallas_call(
        functools.partial(_ffn_up_kernel, n_prompt_tiles=npt, seq=seq),
        grid=(nj, m // bm),
        in_specs=[pl.BlockSpec((bm, d), lambda j, i: (i, 0)),
                  pl.BlockSpec((None, d, bn), lambda j, i: (layer, 0, j)),
                  pl.BlockSpec((None, d, bn), lambda j, i: (layer, 0, nj + j)),
                  pl.BlockSpec((None, CONV_W, bn), lambda j, i: (layer, 0, j)),
                  pl.BlockSpec((None, 1, bn), lambda j, i: (layer, 0, j)),
                  pl.BlockSpec((None, nseg, CONV_W - 1, bn),
                               lambda j, i: (layer, jnp.maximum(i - npt, 0), 0, j))],
        out_specs=[pl.BlockSpec((bm, bn), lambda j, i: (i, j)),
                   pl.BlockSpec((nseg, CONV_W - 1, bn), lambda j, i: (i, 0, j))],
        out_shape=[jax.ShapeDtypeStruct((m, dff), BF16),
                   jax.ShapeDtypeStruct((m // seq, CONV_W - 1, dff), F32)],
        scratch_shapes=[pltpu.VMEM((d, bn), BF16), pltpu.VMEM((d, bn), BF16),
                        pltpu.VMEM((bm + 8, bn), F32)],
        compiler_params=_params("parallel", "arbitrary"),
        name="ffn_up",
    )(h, w_up, w_up, conv_w, _rows3(conv_b), conv_state)


def _hgrn_in_kernel(h_ref, wq_ref, wf_ref, wv_ref, wg_ref, lbp_ref,
                    q_ref, b2_ref, k_ref, qe_ref, kd_ref, v_ref, gate_ref,
                    wqb_ref, wfb_ref, wvb_ref, wgb_ref, *, layer):
    @pl.when(pl.program_id(1) == 0)
    def _():
        wqb_ref[...] = wq_ref[...].astype(BF16)
        wfb_ref[...] = wf_ref[...].astype(BF16)
        wvb_ref[...] = wv_ref[...].astype(BF16)
        wgb_ref[...] = wg_ref[...].astype(BF16)

    h = h_ref[...]
    rows, width = q_ref.shape
    q = jnp.dot(h, wqb_ref[...], preferred_element_type=F32)
    fz = jnp.dot(h, wfb_ref[...], preferred_element_type=F32)
    v_ref[...] = jnp.dot(h, wvb_ref[...], preferred_element_type=F32).astype(BF16)
    gate = jnp.dot(h, wgb_ref[...], preferred_element_type=F32)
    gate_ref[...] = (gate * jax.nn.sigmoid(gate)).astype(BF16)

    lbp = lbp_ref[...]
    e = jnp.exp(lbp - jnp.max(lbp, axis=0, keepdims=True))
    lb = jnp.sum(e[0:layer + 1], axis=0, keepdims=True) / jnp.sum(e, axis=0, keepdims=True)

    f = lb + (1.0 - lb) * jax.nn.sigmoid(fz)
    kk = 1.0 - f
    pos = lax.broadcasted_iota(jnp.int32, (rows, width), 0) % HGRN_BLOCK
    b = jnp.log(f)
    sh = 1
    while sh < HGRN_BLOCK:
        b = b + jnp.where(pos >= sh, pltpu.roll(b, sh, 0), 0.0)
        sh *= 2
    b3 = b.reshape(rows // HGRN_BLOCK, HGRN_BLOCK, width)
    b_last = jnp.broadcast_to(b3[:, HGRN_BLOCK - 1:, :], b3.shape).reshape(rows, width)
    q_ref[...] = q
    b2_ref[...] = b * LOG2E
    k_ref[...] = kk
    qe_ref[...] = (q * jnp.exp(b)).astype(BF16)
    kd_ref[...] = (kk * jnp.exp(b_last - b)).astype(BF16)


def _hgrn_project(h, w, lower_bounds, layer, jl):
    m, d = h.shape
    dk = w.shape[2] // 4
    bm, bn = _tile(m, 1024), _tile(dk, 256)
    nj = dk // bn

    def w_spec(seg):
        return pl.BlockSpec((None, d, bn), lambda j, i: (jl, 0, seg * nj + j))

    o_spec = pl.BlockSpec((bm, bn), lambda j, i: (i, j))
    f32_out, bf16_out = jax.ShapeDtypeStruct((m, dk), F32), jax.ShapeDtypeStruct((m, dk), BF16)
    return pl.pallas_call(
        functools.partial(_hgrn_in_kernel, layer=layer),
        grid=(nj, m // bm),
        in_specs=[pl.BlockSpec((bm, d), lambda j, i: (i, 0)), w_spec(0), w_spec(1), w_spec(2), w_spec(3),
                  pl.BlockSpec((lower_bounds.shape[0], bn), lambda j, i: (0, j))],
        out_specs=[o_spec] * 7,
        out_shape=[f32_out, f32_out, f32_out, bf16_out, bf16_out, bf16_out, bf16_out],
        scratch_shapes=[pltpu.VMEM((d, bn), BF16)] * 4,
        compiler_params=_params("parallel", "arbitrary"),
        name="hgrn_in",
    )(h, w, w, w, w, lower_bounds)


def _hgrn_kernel(q_ref, b_ref, k_ref, qe_ref, kd_ref, v_ref, gate_ref, gout_ref, s0_ref, o_ref, sout_ref,
                 st_ref, oacc_ref, *, seq_blocks):
    c = pl.program_id(1)
    t_rows = q_ref.shape[0]
    nblk = t_rows // HGRN_BLOCK
    carry = seq_blocks is None

    if carry:
        @pl.when(c == 0)
        def _():
            st_ref[...] = jnp.zeros_like(st_ref)

    half = HGRN_BLOCK // 2
    row = lax.broadcasted_iota(jnp.int32, (half, LANES), 0)
    lane = lax.broadcasted_iota(jnp.int32, (half, LANES), 1)

    def scores(r0):
        lo, hi = pl.ds(r0, half), pl.ds(r0 + half, half)
        b_lo, b_hi, q_lo, q_hi = b_ref[lo, :], b_ref[hi, :], q_ref[lo, :], q_ref[hi, :]
        b_mid = b_ref[pl.ds(r0 + half - 1, 1), :]
        q_in = (q_hi * jnp.exp2(b_hi - b_mid)).astype(BF16)
        k_out = (k_ref[lo, :] * jnp.exp2(b_mid - b_lo)).astype(BF16)
        k_out = jnp.concatenate([k_out, jnp.zeros((LANES - half, LANES), BF16)], axis=0)
        sc_hi = lax.dot_general(q_in, k_out, (((1,), (1,)), ((), ())), preferred_element_type=F32)
        sc_lo = jnp.zeros((half, LANES), F32)
        for s in range(HGRN_BLOCK):
            bs, ks = b_ref[pl.ds(r0 + s, 1), :], k_ref[pl.ds(r0 + s, 1), :]
            if s < half:
                col_lo = jnp.sum(jnp.exp2(b_lo - bs) * (q_lo * ks), axis=-1, keepdims=True)
                sc_lo = jnp.where(lane == s, col_lo, sc_lo)
            else:
                col_hi = jnp.sum(jnp.exp2(b_hi - bs) * (q_hi * ks), axis=-1, keepdims=True)
                sc_hi = jnp.where(lane == s, col_hi, sc_hi)
        sc = jnp.concatenate([jnp.where(row >= lane, sc_lo, 0.0),
                              jnp.where(row + half >= lane, sc_hi, 0.0)], axis=0)
        return sc[:, 0:HGRN_BLOCK].astype(BF16)

    group = min(nblk, HGRN_GROUP) if carry else nblk

    def blocks(jg, st):
        rows, sc, vb, decay, upd = [], [], [], [], []
        for g in range(group):
            r0 = pl.multiple_of((jg * group + g) * HGRN_BLOCK, HGRN_BLOCK)
            rows.append(pl.ds(r0, HGRN_BLOCK))
            vb.append(v_ref[rows[g], :])
            sc.append(scores(r0))
            decay.append(jnp.exp2(b_ref[pl.ds(r0 + HGRN_BLOCK - 1, 1), :]))
            upd.append(lax.dot_general(vb[g], kd_ref[rows[g], :], (((0,), (0,)), ((), ())),
                                       preferred_element_type=F32))
        states = []
        for g in range(group):
            if not carry and g % seq_blocks == 0:
                st = s0_ref[g // seq_blocks].T
            states.append(st.astype(BF16))
            st = st * decay[g] + upd[g]
            if not carry and (g + 1) % seq_blocks == 0:
                sout_ref[g // seq_blocks] = st.T
        for g in range(group):
            o = lax.dot_general(qe_ref[rows[g], :], states[g], (((1,), (1,)), ((), ())),
                                preferred_element_type=F32)
            oacc_ref[rows[g], :] = o + jnp.dot(sc[g], vb[g], preferred_element_type=F32)
        return st

    if carry:
        st_ref[...] = lax.fori_loop(0, nblk // group, blocks, st_ref[...])
    else:
        blocks(0, None)

    o = oacc_ref[...]
    y = o * lax.rsqrt(jnp.mean(o * o, axis=-1, keepdims=True) + EPS) * gout_ref[...]
    o_ref[...] = (y * gate_ref[...]).astype(BF16)

    if carry:
        @pl.when(c == pl.num_programs(1) - 1)
        def _():
            sout_ref[...] = st_ref[...].T


def _hgrn_scan(ops, out_norm, state, jl, n_prompt_rows, seq):
    nb, nh, dk, dv = state.shape[1:]
    assert dk == LANES and dv == LANES
    scratch = lambda t: [pltpu.VMEM((dv, dk), F32), pltpu.VMEM((t, dv), F32)]

    def specs(t, row_of):
        return ([pl.BlockSpec((t, LANES), lambda h, c: (row_of(c), h))] * len(ops)
                + [pl.BlockSpec((None, 1, dv), lambda h, c: (jl, 0, 0))])

    out_norm = _rows3(out_norm)
    tp = _tile(n_prompt_rows, HGRN_TILE)
    o_p, s_p = pl.pallas_call(
        functools.partial(_hgrn_kernel, seq_blocks=None),
        grid=(nh, n_prompt_rows // tp),
        in_specs=specs(tp, lambda c: c) + [pl.BlockSpec((None, None, None, dk, dv),
                                                        lambda h, c: (jl, 0, h, 0, 0))],
        out_specs=[pl.BlockSpec((tp, dv), lambda h, c: (c, h)),
                   pl.BlockSpec((None, dk, dv), lambda h, c: (h, 0, 0))],
        out_shape=[jax.ShapeDtypeStruct((n_prompt_rows, nh * dv), BF16),
                   jax.ShapeDtypeStruct((nh, dk, dv), F32)],
        scratch_shapes=scratch(tp),
        compiler_params=_params("parallel", "arbitrary"),
        name="hgrn_prompt",
    )(*ops, out_norm, state)
    ns = _tile(nb, max(1, HGRN_GROUP * HGRN_BLOCK // seq))
    ts = ns * seq
    assert n_prompt_rows % ts == 0 and seq % HGRN_BLOCK == 0
    r0 = n_prompt_rows // ts
    o_s, s_s = pl.pallas_call(
        functools.partial(_hgrn_kernel, seq_blocks=seq // HGRN_BLOCK),
        grid=(nh, nb // ns),
        in_specs=specs(ts, lambda c: r0 + c) + [pl.BlockSpec((None, ns, None, dk, dv),
                                                              lambda h, c: (jl, c, h, 0, 0))],
        out_specs=[pl.BlockSpec((ts, dv), lambda h, c: (c, h)),
                   pl.BlockSpec((ns, None, dk, dv), lambda h, c: (c, h, 0, 0))],
        out_shape=[jax.ShapeDtypeStruct((nb * seq, nh * dv), BF16),
                   jax.ShapeDtypeStruct((nb, nh, dk, dv), F32)],
        scratch_shapes=scratch(ts),
        compiler_params=_params("parallel", "arbitrary"),
        name="hgrn_sample",
    )(*ops, out_norm, state)
    return (o_p, o_s), s_p, s_s


def _lanes(x, n):
    return x[:, :n] if n <= LANES else jnp.concatenate([x] * (n // LANES), axis=1)


def _scores(q, k):
    return lax.dot_general(q, k, (((1,), (1,)), ((), ())), preferred_element_type=F32)


def _softmax_step(c, s, m_ref, l_ref, mask=None, rows=slice(None)):
    if mask is not None:
        s = jnp.where(mask, s, -jnp.inf)
    m_prev = m_ref[c, rows]
    m_new = jnp.maximum(m_prev, jnp.max(s, axis=-1, keepdims=True))
    alpha = jnp.exp2(m_prev - m_new)
    p = jnp.exp2(s - _lanes(m_new, s.shape[1]))
    l_ref[c, rows] = alpha * l_ref[c, rows] + jnp.sum(p, axis=-1, keepdims=True)
    m_ref[c, rows] = m_new
    return alpha, p.astype(BF16)


def _pv_step(c, alpha, p, v, acc_ref, rows=slice(None)):
    acc_ref[c, rows] = (_lanes(alpha, v.shape[1]) * acc_ref[c, rows]
                        + jnp.dot(p, v, preferred_element_type=F32))


def _diff_finish(a0, l0, a1, l1, lam_refs, subln, lam_init):
    lq1, lk1, lq2, lk2 = [r[...] for r in lam_refs]
    lam = (jnp.exp(jnp.sum(lq1 * lk1, axis=-1, keepdims=True))
           - jnp.exp(jnp.sum(lq2 * lk2, axis=-1, keepdims=True)) + lam_init)
    o = a0 * _lanes(1.0 / l0, a0.shape[1]) - lam * (a1 * _lanes(1.0 / l1, a1.shape[1]))
    y = o * lax.rsqrt(jnp.mean(o * o, axis=-1, keepdims=True) + EPS)
    return (y * subln * (1.0 - lam_init)).astype(BF16)


def _attn_prompt_kernel(it_ref, jt_ref, q_ref, k_ref, v_ref, lq1, lk1, lq2, lk2, sub_ref, o_ref,
                        m_ref, l_ref, acc_ref, *, lam_init):
    p = pl.program_id(1)
    i, j = it_ref[p], jt_ref[p]
    bq, bk = q_ref.shape[0], k_ref.shape[0]
    dh = q_ref.shape[1] // 2

    @pl.when(j == 0)
    def _():
        m_ref[...] = jnp.full(m_ref.shape, -jnp.inf, F32)
        l_ref[...] = jnp.zeros(l_ref.shape, F32)
        acc_ref[...] = jnp.zeros(acc_ref.shape, F32)

    def steps(*parts):
        work = [(c, mask, rows, keys,
                 _scores(q_ref[rows, c * dh:(c + 1) * dh], k_ref[keys, c * dh:(c + 1) * dh]))
                for mask, rows, keys in parts for c in range(2)]
        for c, mask, rows, keys, s in work:
            v = v_ref[keys, :]
            alpha, p = _softmax_step(c, s, m_ref, l_ref, mask, rows)
            _pv_step(c, alpha, p, v, acc_ref, rows)

    @pl.when(j < i)
    def _():
        steps((None, slice(None), slice(None)))

    @pl.when(j == i)
    def _():
        half = bq // 2

        def chunk_mask(n_keys, first_row):
            qpos = first_row + lax.broadcasted_iota(jnp.int32, (half, n_keys), 0)
            kpos = lax.broadcasted_iota(jnp.int32, (half, n_keys), 1)
            return kpos < (qpos // CHUNK + 1) * CHUNK

        steps((chunk_mask(half, 0), slice(0, half), slice(0, half)),
              (chunk_mask(bk, half), slice(half, bq), slice(None)))
        o_ref[...] = _diff_finish(acc_ref[0], l_ref[0], acc_ref[1], l_ref[1],
                                  (lq1, lk1, lq2, lk2), sub_ref[...], lam_init)


def _attn_prompt(qb, kb, vb, lams, subln, jl, lam_init, n_rows, nh):
    dv = qb.shape[1] // nh
    bq = bk = _tile(n_rows, ATTN_TILE)
    assert (bq // 2) % CHUNK == 0
    nq = n_rows // bq
    pairs = [(i, j) for i in range(nq) for j in range(i + 1)]
    it = jnp.asarray([p[0] for p in pairs], jnp.int32)
    jt = jnp.asarray([p[1] for p in pairs], jnp.int32)
    lam_spec = pl.BlockSpec((None, 1, dv // 2), lambda h, p, it, jt: (jl, 0, 0))
    return pl.pallas_call(
        functools.partial(_attn_prompt_kernel, lam_init=lam_init),
        grid_spec=pltpu.PrefetchScalarGridSpec(
            num_scalar_prefetch=2,
            grid=(nh, len(pairs)),
            in_specs=[pl.BlockSpec((bq, dv), lambda h, p, it, jt: (it[p], h)),
                      pl.BlockSpec((bk, dv), lambda h, p, it, jt: (jt[p], h)),
                      pl.BlockSpec((bk, dv), lambda h, p, it, jt: (jt[p], h)),
                      lam_spec, lam_spec, lam_spec, lam_spec,
                      pl.BlockSpec((None, 1, dv), lambda h, p, it, jt: (jl, 0, 0))],
            out_specs=pl.BlockSpec((bq, dv), lambda h, p, it, jt: (it[p], h)),
            scratch_shapes=[pltpu.VMEM((2, bq, LANES), F32), pltpu.VMEM((2, bq, LANES), F32),
                            pltpu.VMEM((2, bq, dv), F32)]),
        out_shape=jax.ShapeDtypeStruct((n_rows, nh * dv), BF16),
        compiler_params=_params("parallel", "arbitrary"),
        name="attn_prompt",
    )(it, jt, qb, kb, vb, *[_rows3(a) for a in lams], _rows3(subln))


def _attn_sample_kernel(q_ref, ck_ref, cv_ref, kn_ref, vn_ref, lq1, lk1, lq2, lk2, sub_ref, o_ref,
                        m_ref, l_ref, acc_ref, *, lam_init, nh):
    j = pl.program_id(1)
    last = pl.num_programs(1) - 1
    dv = q_ref.shape[1] // nh
    dh = dv // 2

    @pl.when(j == 0)
    def _():
        m_ref[...] = jnp.full(m_ref.shape, -jnp.inf, F32)
        l_ref[...] = jnp.zeros(l_ref.shape, F32)
        acc_ref[...] = jnp.zeros(acc_ref.shape, F32)

    def q_of(h, c):
        return q_ref[:, h * dv + c * dh:h * dv + (c + 1) * dh]

    def step(k_of, v_of, scores_first):
        ss = [_scores(q_of(h, c), k_of(h, c)) for h in range(nh) for c in range(2)] if scores_first else None
        for h in range(nh):
            v = v_of(h)
            for c in range(2):
                s = ss[2 * h + c] if scores_first else _scores(q_of(h, c), k_of(h, c))
                a, p = _softmax_step(2 * h + c, s, m_ref, l_ref)
                _pv_step(2 * h + c, a, p, v, acc_ref)

    @pl.when(j < last)
    def _():
        ng = 2 * nh
        tok = LANES // ng
        i_out = lax.broadcasted_iota(jnp.int32, (LANES, LANES), 0)
        i_in = lax.broadcasted_iota(jnp.int32, (LANES, LANES), 1)
        perm = jnp.where(i_in == (i_out % tok) * ng + i_out // tok, 1.0, 0.0).astype(BF16)

        def regroup(src_ref):
            n_slab = src_ref.shape[0] // LANES
            wide = jnp.concatenate([src_ref[n * LANES:(n + 1) * LANES, :].astype(BF16)
                                    for n in range(n_slab)], axis=1)
            y = jnp.dot(perm, wide, preferred_element_type=F32)
            return [jnp.concatenate([y[g * tok:(g + 1) * tok, n * LANES:(n + 1) * LANES]
                                     for n in range(n_slab)], axis=0).astype(BF16) for g in range(ng)]

        kg, vg = regroup(ck_ref), regroup(cv_ref)
        step(lambda h, c: kg[2 * h + c],
             lambda h: jnp.concatenate([vg[h], vg[nh + h]], axis=1), False)

    @pl.when(j == last)
    def _():
        step(lambda h, c: kn_ref[:, h * dv + c * dh:h * dv + (c + 1) * dh],
             lambda h: vn_ref[:, h * dv:(h + 1) * dv], True)
        for h in range(nh):
            o_ref[:, h * dv:(h + 1) * dv] = _diff_finish(
                acc_ref[2 * h], l_ref[2 * h], acc_ref[2 * h + 1], l_ref[2 * h + 1],
                (lq1, lk1, lq2, lk2), sub_ref[...], lam_init)


def _attn_sample(qb, kb, vb, cache_k, cache_v, lams, subln, jl, lam_init, n_prompt_rows, seq, nh):
    d = qb.shape[1]
    dv = d // nh
    nb, past = cache_k.shape[1], cache_k.shape[2]
    bk = _tile(past, 512)
    nkc = past // bk
    r0 = n_prompt_rows // seq
    new_spec = pl.BlockSpec((seq, d), lambda b, j: (r0 + b, 0))
    nl = cache_k.shape[0]
    ck = cache_k.reshape(nl, nb, past * nh * 2, dv // 2)
    cv = cache_v.reshape(nl, nb, past, nh, 2, dv // 2).transpose(0, 1, 2, 4, 3, 5).reshape(ck.shape)
    cache_spec = pl.BlockSpec((None, None, bk * nh * 2, dv // 2),
                              lambda b, j: (jl, b, jnp.minimum(j, nkc - 1), 0))
    lam_spec = pl.BlockSpec((None, 1, dv // 2), lambda b, j: (jl, 0, 0))
    return pl.pallas_call(
        functools.partial(_attn_sample_kernel, lam_init=lam_init, nh=nh),
        grid=(nb, nkc + 1),
        in_specs=[new_spec, cache_spec, cache_spec, new_spec, new_spec,
                  lam_spec, lam_spec, lam_spec, lam_spec,
                  pl.BlockSpec((None, 1, dv), lambda b, j: (jl, 0, 0))],
        out_specs=pl.BlockSpec((seq, d), lambda b, j: (b, 0)),
        out_shape=jax.ShapeDtypeStruct((nb * seq, d), BF16),
        scratch_shapes=[pltpu.VMEM((2 * nh, seq, LANES), F32), pltpu.VMEM((2 * nh, seq, LANES), F32),
                        pltpu.VMEM((2 * nh, seq, dv), F32)],
        compiler_params=_params("parallel", "arbitrary"),
        name="attn_sample",
    )(qb, ck, cv, kb, vb, *[_rows3(a) for a in lams], _rows3(subln))


def kernel(x_prompt, x_sample, state_hgrn, cache_k, cache_v, state_ffn_conv, norm_mix, norm_ffn, hgrn_lower_bounds, w_hgrn_in, w_hgrn_out, hgrn_out_norm, w_diff_in, w_diff_out, diff_q_norm, diff_k_norm, diff_lambda_q1, diff_lambda_k1, diff_lambda_q2, diff_lambda_k2, diff_subln, w_ffn_up, ffn_conv_w, ffn_conv_b, w_ffn_down):
    bp, seq_p, d = x_prompt.shape
    nb, seq_s, _ = x_sample.shape
    assert bp == 1 and seq_s == CHUNK
    depth = norm_mix.shape[0]
    n_mixers = 2
    n_p = bp * seq_p
    diff_heads, dh = cache_k.shape[3], cache_k.shape[5]
    x = (x_prompt.reshape(n_p, d), x_sample.reshape(nb * seq_s, d))

    hgrn_p, hgrn_s, kfs, vfs, tails = [], [], [], [], []
    for i in range(depth):
        jl = i // n_mixers
        h = _rms_norm_bf16(x, norm_mix, i)
        if i % n_mixers == 0:
            ops = _hgrn_project(h, w_hgrn_in, hgrn_lower_bounds, i, jl)
            o, s_p, s_s = _hgrn_scan(ops, hgrn_out_norm, state_hgrn, jl, n_p, seq_s)
            hgrn_p.append(s_p[None])
            hgrn_s.append(s_s)
            x = _matmul(o, w_hgrn_out, jl, n_p, res=x, bm=512, bn=1024, w_buffers=1, name="hgrn_out")
        else:
            lam_init = 0.8 - 0.6 * math.exp(-0.3 * i)
            lams = (diff_lambda_q1, diff_lambda_k1, diff_lambda_q2, diff_lambda_k2)
            qb, kf, kb, vf, vb = _diff_project(h, w_diff_in, jl, diff_q_norm, diff_k_norm, dh, n_p)
            o_p = _attn_prompt(qb, kb, vb, lams, diff_subln, jl, lam_init, n_p, diff_heads)
            o_s = _attn_sample(qb, kb, vb, cache_k, cache_v, lams, diff_subln, jl, lam_init,
                               n_p, seq_s, diff_heads)
            kfs.append(kf)
            vfs.append(vf)
            x = _matmul((o_p, o_s), w_diff_out, jl, n_p, res=x, bm=512, bn=1024, w_buffers=1,
                        name="diff_out")
        h = _rms_norm_bf16(x, norm_ffn, i)
        act, tail = _ffn_up(h, w_ffn_up, ffn_conv_w, ffn_conv_b, state_ffn_conv, i, n_p, seq_s)
        tails.append(tail)
        x = _matmul(act, w_ffn_down, i, n_p, res=x, split_out=i == depth - 1, bm=512, bn=1024,
                    w_buffers=1, name="ffn_down")

    x_p, x_s = x
    tail = jnp.stack(tails)
    seg_p = n_p // seq_s
    stack = lambda pairs, k: jnp.stack([p[k] for p in pairs])
    return (x_p.reshape(bp, seq_p, d),
            x_s.reshape(nb, seq_s, d),
            jnp.stack(hgrn_p),
            jnp.stack(hgrn_s),
            stack(kfs, 0).reshape(-1, bp, seq_p, diff_heads, 2, dh),
            stack(vfs, 0).reshape(-1, bp, seq_p, diff_heads, 2 * dh),
            stack(kfs, 1).reshape(-1, nb, seq_s, diff_heads, 2, dh),
            stack(vfs, 1).reshape(-1, nb, seq_s, diff_heads, 2 * dh),
            tail[:, seg_p - 1][:, None],
            tail[:, seg_p:])
```

```python
import functools
import math

import jax
import jax.numpy as jnp
from jax import lax
from jax.experimental import pallas as pl
from jax.experimental.pallas import tpu as pltpu

EPS = 1e-6
LOG2E = math.log2(math.e)
CHUNK = 64
HGRN_BLOCK = 16
ATTN_TILE = 1024
ATTN_HEADS = 2
HGRN_TILE = 2048
HGRN_GROUP = 32
CONV_W = 3
LANES = 128
V7X_VMEM_BYTES = 64 * 1024 * 1024
VMEM_LIMIT = V7X_VMEM_BYTES - 8 * 1024 * 1024

F32 = jnp.float32
BF16 = jnp.bfloat16


def _params(*sem):
    return pltpu.CompilerParams(dimension_semantics=sem, vmem_limit_bytes=VMEM_LIMIT)


def _rows3(a):
    return a.reshape(a.shape[0], 1, a.shape[1])


def _tile(n, pref):
    t = min(n, pref)
    while n % t:
        t //= 2
    return t


def _split_specs(n_p, n_s, bm, bn, col):
    npt, nst = n_p // bm, n_s // bm
    return [pl.BlockSpec((bm, bn), lambda j, i: (jnp.minimum(i, npt - 1), col(j))),
            pl.BlockSpec((bm, bn), lambda j, i: (jnp.clip(i - npt, 0, nst - 1), col(j)))]


def _on_rows(i, npt, split, fn, *ref_pairs):
    if not split:
        fn(*[p[0] for p in ref_pairs])
        return
    pl.when(i < npt)(lambda: fn(*[p[0] for p in ref_pairs]))
    pl.when(i >= npt)(lambda: fn(*[p[-1] for p in ref_pairs]))


def _norm_kernel(*refs, npt):
    *x_refs, g_ref, o_ref = refs

    def run(x_ref):
        x = x_ref[...]
        y = x * lax.rsqrt(jnp.mean(x * x, axis=-1, keepdims=True) + EPS)
        o_ref[...] = (y * g_ref[...]).astype(o_ref.dtype)

    _on_rows(pl.program_id(0), npt, len(x_refs) == 2, run, x_refs)


def _rms_norm_bf16(x, gains, layer):
    xs = x if isinstance(x, tuple) else (x,)
    d = xs[0].shape[1]
    m = sum(a.shape[0] for a in xs)
    bm = _tile(math.gcd(*[a.shape[0] for a in xs]), 512)
    npt = xs[0].shape[0] // bm
    if len(xs) == 2:
        nst = xs[1].shape[0] // bm
        x_specs = [pl.BlockSpec((bm, d), lambda i: (jnp.minimum(i, npt - 1), 0)),
                   pl.BlockSpec((bm, d), lambda i: (jnp.clip(i - npt, 0, nst - 1), 0))]
    else:
        x_specs = [pl.BlockSpec((bm, d), lambda i: (i, 0))]
    return pl.pallas_call(
        functools.partial(_norm_kernel, npt=npt),
        grid=(m // bm,),
        in_specs=x_specs + [pl.BlockSpec((None, 1, d), lambda i: (layer, 0, 0))],
        out_specs=pl.BlockSpec((bm, d), lambda i: (i, 0)),
        out_shape=jax.ShapeDtypeStruct((m, d), BF16),
        compiler_params=_params("parallel"),
        name="rms_norm",
    )(*xs, _rows3(gains))


def _mm_kernel(*refs, n_res, split_in, split_out, npt):
    refs = list(refs)
    h_refs = [refs.pop(0) for _ in range(2 if split_in else 1)]
    w_ref = refs.pop(0)
    r_refs = [refs.pop(0) for _ in range(n_res)] or [None]
    o_refs = [refs.pop(0) for _ in range(2 if split_out else 1)]
    wb_ref, = refs
    i = pl.program_id(1)

    @pl.when(i == 0)
    def _():
        wb_ref[...] = w_ref[...].astype(BF16)

    def run(h_ref, r_ref, o_ref):
        acc = jnp.dot(h_ref[...], wb_ref[...], preferred_element_type=F32)
        o_ref[...] = acc if r_ref is None else r_ref[...] + acc

    _on_rows(i, npt, split_in or split_out or n_res == 2, run, h_refs, r_refs, o_refs)


def _matmul(h, w, layer, n_p, *, res=None, split_out=False, bm=1024, bn=512, w_buffers=2, name="matmul"):
    split_in = isinstance(h, tuple)
    k, ncols = w.shape[1], w.shape[2]
    m = sum(a.shape[0] for a in h) if split_in else h.shape[0]
    n_s = m - n_p
    bm, bn = _tile(math.gcd(n_p, n_s), bm), _tile(ncols, bn)
    npt = n_p // bm
    in_specs = (_split_specs(n_p, n_s, bm, k, lambda j: 0) if split_in
                else [pl.BlockSpec((bm, k), lambda j, i: (i, 0))])
    in_specs.append(pl.BlockSpec((None, k, bn), lambda j, i: (layer, 0, j),
                                 pipeline_mode=pl.Buffered(w_buffers)))
    args = list(h) if split_in else [h]
    args.append(w)
    res = () if res is None else res if isinstance(res, tuple) else (res,)
    in_specs += (_split_specs(n_p, n_s, bm, bn, lambda j: j) if len(res) == 2
                 else [pl.BlockSpec((bm, bn), lambda j, i: (i, j))] * len(res))
    args += res
    if split_out:
        out_specs = _split_specs(n_p, n_s, bm, bn, lambda j: j)
        out_shape = [jax.ShapeDtypeStruct((n_p, ncols), F32), jax.ShapeDtypeStruct((n_s, ncols), F32)]
    else:
        out_specs = pl.BlockSpec((bm, bn), lambda j, i: (i, j))
        out_shape = jax.ShapeDtypeStruct((m, ncols), F32)
    return pl.pallas_call(
        functools.partial(_mm_kernel, n_res=len(res), split_in=split_in, split_out=split_out, npt=npt),
        grid=(ncols // bn, m // bm),
        in_specs=in_specs,
        out_specs=out_specs,
        out_shape=out_shape,
        scratch_shapes=[pltpu.VMEM((k, bn), BF16)],
        compiler_params=_params("parallel", "arbitrary"),
        name=name,
    )(*args)


def _headnorm(acc, g, scale):
    outs = []
    for c in range(acc.shape[1] // LANES):
        blk = acc[:, c * LANES:(c + 1) * LANES]
        y = blk * lax.rsqrt(jnp.mean(blk * blk, axis=-1, keepdims=True) + EPS)
        outs.append(y * g * scale if scale != 1.0 else y * g)
    return jnp.concatenate(outs, axis=1) if len(outs) > 1 else outs[0]


def _qkv_kernel(h_ref, wq_ref, wk_ref, wv_ref, gq_ref, gk_ref,
                qb_ref, kfp_ref, kfs_ref, kb_ref, vfp_ref, vfs_ref, vb_ref,
                wqb_ref, wkb_ref, wvb_ref, *, scale, npt):
    i = pl.program_id(1)

    @pl.when(i == 0)
    def _():
        wqb_ref[...] = wq_ref[...].astype(BF16)
        wkb_ref[...] = wk_ref[...].astype(BF16)
        wvb_ref[...] = wv_ref[...].astype(BF16)

    h = h_ref[...]
    q = jnp.dot(h, wqb_ref[...], preferred_element_type=F32)
    k = jnp.dot(h, wkb_ref[...], preferred_element_type=F32)
    v = jnp.dot(h, wvb_ref[...], preferred_element_type=F32)
    qb_ref[...] = _headnorm(q, gq_ref[...], scale).astype(BF16)
    kn = _headnorm(k, gk_ref[...], 1.0)
    kb_ref[...] = kn.astype(BF16)
    vb_ref[...] = v.astype(BF16)

    def put(kf_ref, vf_ref):
        for g in range(kf_ref.shape[1]):
            kf_ref[:, g, :] = kn[:, g * LANES:(g + 1) * LANES]
        vf_ref[...] = v

    _on_rows(i, npt, True, put, (kfp_ref, kfs_ref), (vfp_ref, vfs_ref))


def _diff_project(h, w, layer, q_g, k_g, dh, n_p):
    m, d = h.shape
    n_s = m - n_p
    bm, bn = _tile(math.gcd(n_p, n_s), 1024), _tile(d, 512)
    nj = d // bn
    f_specs = _split_specs(n_p, n_s, bm, bn, lambda j: j)
    f_shapes = [jax.ShapeDtypeStruct((n_p, d), F32), jax.ShapeDtypeStruct((n_s, d), F32)]
    npt, nst, ng = n_p // bm, n_s // bm, bn // dh
    k_specs = [pl.BlockSpec((bm, None, ng, dh), lambda j, i: (jnp.minimum(i, npt - 1), j, 0, 0)),
               pl.BlockSpec((bm, None, ng, dh), lambda j, i: (jnp.clip(i - npt, 0, nst - 1), j, 0, 0))]
    k_shapes = [jax.ShapeDtypeStruct((n_p, nj, ng, dh), F32), jax.ShapeDtypeStruct((n_s, nj, ng, dh), F32)]
    o_spec = pl.BlockSpec((bm, bn), lambda j, i: (i, j))
    b_shape = jax.ShapeDtypeStruct((m, d), BF16)
    g_spec = pl.BlockSpec((None, 1, dh), lambda j, i: (layer, 0, 0))

    def w_spec(seg):
        return pl.BlockSpec((None, d, bn), lambda j, i: (layer, 0, seg * nj + j),
                            pipeline_mode=pl.Buffered(1))

    qb, kf_p, kf_s, kb, vf_p, vf_s, vb = pl.pallas_call(
        functools.partial(_qkv_kernel, scale=dh ** -0.5 * LOG2E, npt=n_p // bm),
        grid=(nj, m // bm),
        in_specs=[pl.BlockSpec((bm, d), lambda j, i: (i, 0)), w_spec(0), w_spec(1), w_spec(2),
                  g_spec, g_spec],
        out_specs=[o_spec] + k_specs + [o_spec] + f_specs + [o_spec],
        out_shape=[b_shape] + k_shapes + [b_shape] + f_shapes + [b_shape],
        scratch_shapes=[pltpu.VMEM((d, bn), BF16)] * 3,
        compiler_params=_params("parallel", "arbitrary"),
        name="diff_qkv",
    )(h, w, w, w, _rows3(q_g), _rows3(k_g))
    return qb, (kf_p, kf_s), kb, (vf_p, vf_s), vb


def _ffn_up_kernel(h_ref, wg_ref, wu_ref, cw_ref, cb_ref, st_ref, a_ref, tail_ref,
                   wgb_ref, wub_ref, g_ref, *, n_prompt_tiles, seq):
    i = pl.program_id(1)
    bm, bn = a_ref.shape
    nseg = bm // seq

    @pl.when(i == 0)
    def _():
        wgb_ref[...] = wg_ref[...].astype(BF16)
        wub_ref[...] = wu_ref[...].astype(BF16)
        g_ref[0:8, :] = jnp.zeros((8, bn), F32)

    h = h_ref[...]
    g = jnp.dot(h, wgb_ref[...], preferred_element_type=F32)
    u = jnp.dot(h, wub_ref[...], preferred_element_type=F32)
    g_ref[8:8 + bm, :] = g
    g1 = g_ref[7:7 + bm, :]
    g2 = g_ref[6:6 + bm, :]
    st = st_ref[...]
    p2 = jnp.broadcast_to(st[:, 0:1, :], (nseg, seq, bn)).reshape(bm, bn)
    p1 = jnp.broadcast_to(st[:, 1:2, :], (nseg, seq, bn)).reshape(bm, bn)
    pos = (lax.broadcasted_iota(jnp.int32, (bm, bn), 0) % seq
           + jnp.where(i >= n_prompt_tiles, 0, seq))
    g1 = jnp.where(pos == 0, p1, g1)
    g2 = jnp.where(pos == 0, p2, jnp.where(pos == 1, p1, g2))
    cw = cw_ref[...]
    conv = cb_ref[...] + cw[2:3, :] * g + cw[1:2, :] * g1 + cw[0:1, :] * g2
    a_ref[...] = (conv * jax.nn.sigmoid(conv) * u).astype(BF16)

    for n in range(nseg):
        end = 8 + (n + 1) * seq
        tail_ref[n] = g_ref[end - (CONV_W - 1):end, :]
    g_ref[0:8, :] = g_ref[bm:bm + 8, :]


def _ffn_up(h, w_up, conv_w, conv_b, conv_state, layer, n_prompt_rows, seq):
    m, d = h.shape
    dff = conv_w.shape[-1]
    nb = conv_state.shape[1]
    bm = _tile(math.gcd(n_prompt_rows, nb * seq), 1024)
    bn = _tile(dff, 512)
    assert bm % seq == 0 and (m - n_prompt_rows) == nb * seq
    nseg = bm // seq
    npt = n_prompt_rows // bm
    nj = dff // bn
    return pl.pallas_call(
        functools.partial(_ffn_up_kernel, n_prompt_tiles=npt, seq=seq),
        grid=(nj, m // bm),
        in_specs=[pl.BlockSpec((bm, d), lambda j, i: (i, 0)),
                  pl.BlockSpec((None, d, bn), lambda j, i: (layer, 0, j)),
                  pl.BlockSpec((None, d, bn), lambda j, i: (layer, 0, nj + j)),
                  pl.BlockSpec((None, CONV_W, bn), lambda j, i: (layer, 0, j)),
                  pl.BlockSpec((None, 1, bn), lambda j, i: (layer, 0, j)),
                  pl.BlockSpec((None, nseg, CONV_W - 1, bn),
                               lambda j, i: (layer, jnp.maximum(i - npt, 0), 0, j))],
        out_specs=[pl.BlockSpec((bm, bn), lambda j, i: (i, j)),
                   pl.BlockSpec((nseg, CONV_W - 1, bn), lambda j, i: (i, 0, j))],
        out_shape=[jax.ShapeDtypeStruct((m, dff), BF16),
                   jax.ShapeDtypeStruct((m // seq, CONV_W - 1, dff), F32)],
        scratch_shapes=[pltpu.VMEM((d, bn), BF16), pltpu.VMEM((d, bn), BF16),
                        pltpu.VMEM((bm + 8, bn), F32)],
        compiler_params=_params("parallel", "arbitrary"),
        name="ffn_up",
    )(h, w_up, w_up, conv_w, _rows3(conv_b), conv_state)


def _hgrn_in_kernel(h_ref, wq_ref, wf_ref, wv_ref, wg_ref, lbp_ref,
                    q_ref, b2_ref, k_ref, qe_ref, kd_ref, v_ref, gate_ref,
                    wqb_ref, wfb_ref, wvb_ref, wgb_ref, *, layer):
    @pl.when(pl.program_id(1) == 0)
    def _():
        wqb_ref[...] = wq_ref[...].astype(BF16)
        wfb_ref[...] = wf_ref[...].astype(BF16)
        wvb_ref[...] = wv_ref[...].astype(BF16)
        wgb_ref[...] = wg_ref[...].astype(BF16)

    h = h_ref[...]
    rows, width = q_ref.shape
    q = jnp.dot(h, wqb_ref[...], preferred_element_type=F32)
    fz = jnp.dot(h, wfb_ref[...], preferred_element_type=F32)
    v_ref[...] = jnp.dot(h, wvb_ref[...], preferred_element_type=F32).astype(BF16)
    gate = jnp.dot(h, wgb_ref[...], preferred_element_type=F32)
    gate_ref[...] = (gate * jax.nn.sigmoid(gate)).astype(BF16)

    lbp = lbp_ref[...]
    e = jnp.exp(lbp - jnp.max(lbp, axis=0, keepdims=True))
    lb = jnp.sum(e[0:layer + 1], axis=0, keepdims=True) / jnp.sum(e, axis=0, keepdims=True)

    f = lb + (1.0 - lb) * jax.nn.sigmoid(fz)
    kk = 1.0 - f
    pos = lax.broadcasted_iota(jnp.int32, (rows, width), 0) % HGRN_BLOCK
    b = jnp.log(f)
    sh = 1
    while sh < HGRN_BLOCK:
        b = b + jnp.where(pos >= sh, pltpu.roll(b, sh, 0), 0.0)
        sh *= 2
    b3 = b.reshape(rows // HGRN_BLOCK, HGRN_BLOCK, width)
    b_last = jnp.broadcast_to(b3[:, HGRN_BLOCK - 1:, :], b3.shape).reshape(rows, width)
    q_ref[...] = q
    b2_ref[...] = b * LOG2E
    k_ref[...] = kk
    qe_ref[...] = (q * jnp.exp(b)).astype(BF16)
    kd_ref[...] = (kk * jnp.exp(b_last - b)).astype(BF16)


def _hgrn_project(h, w, lower_bounds, layer, jl):
    m, d = h.shape
    dk = w.shape[2] // 4
    bm, bn = _tile(m, 1024), _tile(dk, 256)
    nj = dk // bn

    def w_spec(seg):
        return pl.BlockSpec((None, d, bn), lambda j, i: (jl, 0, seg * nj + j))

    o_spec = pl.BlockSpec((bm, bn), lambda j, i: (i, j))
    f32_out, bf16_out = jax.ShapeDtypeStruct((m, dk), F32), jax.ShapeDtypeStruct((m, dk), BF16)
    return pl.pallas_call(
        functools.partial(_hgrn_in_kernel, layer=layer),
        grid=(nj, m // bm),
        in_specs=[pl.BlockSpec((bm, d), lambda j, i: (i, 0)), w_spec(0), w_spec(1), w_spec(2), w_spec(3),
                  pl.BlockSpec((lower_bounds.shape[0], bn), lambda j, i: (0, j))],
        out_specs=[o_spec] * 7,
        out_shape=[f32_out, f32_out, f32_out, bf16_out, bf16_out, bf16_out, bf16_out],
        scratch_shapes=[pltpu.VMEM((d, bn), BF16)] * 4,
        compiler_params=_params("parallel", "arbitrary"),
        name="hgrn_in",
    )(h, w, w, w, w, lower_bounds)


def _hgrn_kernel(q_ref, b_ref, k_ref, qe_ref, kd_ref, v_ref, gate_ref, gout_ref, s0_ref, o_ref, sout_ref,
                 st_ref, oacc_ref, *, seq_blocks):
    c = pl.program_id(1)
    t_rows = q_ref.shape[0]
    nblk = t_rows // HGRN_BLOCK
    carry = seq_blocks is None

    if carry:
        @pl.when(c == 0)
        def _():
            st_ref[...] = jnp.zeros_like(st_ref)

    half = HGRN_BLOCK // 2
    row = lax.broadcasted_iota(jnp.int32, (half, LANES), 0)
    lane = lax.broadcasted_iota(jnp.int32, (half, LANES), 1)

    def scores(r0):
        lo, hi = pl.ds(r0, half), pl.ds(r0 + half, half)
        b_lo, b_hi, q_lo, q_hi = b_ref[lo, :], b_ref[hi, :], q_ref[lo, :], q_ref[hi, :]
        b_mid = b_ref[pl.ds(r0 + half - 1, 1), :]
        q_in = (q_hi * jnp.exp2(b_hi - b_mid)).astype(BF16)
        k_out = (k_ref[lo, :] * jnp.exp2(b_mid - b_lo)).astype(BF16)
        k_out = jnp.concatenate([k_out, jnp.zeros((LANES - half, LANES), BF16)], axis=0)
        sc_hi = lax.dot_general(q_in, k_out, (((1,), (1,)), ((), ())), preferred_element_type=F32)
        sc_lo = jnp.zeros((half, LANES), F32)
        for s in range(HGRN_BLOCK):
            bs, ks = b_ref[pl.ds(r0 + s, 1), :], k_ref[pl.ds(r0 + s, 1), :]
            if s < half:
                col_lo = jnp.sum(jnp.exp2(b_lo - bs) * (q_lo * ks), axis=-1, keepdims=True)
                sc_lo = jnp.where(lane == s, col_lo, sc_lo)
            else:
                col_hi = jnp.sum(jnp.exp2(b_hi - bs) * (q_hi * ks), axis=-1, keepdims=True)
                sc_hi = jnp.where(lane == s, col_hi, sc_hi)
        sc = jnp.concatenate([jnp.where(row >= lane, sc_lo, 0.0),
                              jnp.where(row + half >= lane, sc_hi, 0.0)], axis=0)
        return sc[:, 0:HGRN_BLOCK].astype(BF16)

    group = min(nblk, HGRN_GROUP) if carry else nblk

    def blocks(jg, st):
        rows, sc, vb, decay, upd = [], [], [], [], []
        for g in range(group):
            r0 = pl.multiple_of((jg * group + g) * HGRN_BLOCK, HGRN_BLOCK)
            rows.append(pl.ds(r0, HGRN_BLOCK))
            vb.append(v_ref[rows[g], :])
            sc.append(scores(r0))
            decay.append(jnp.exp2(b_ref[pl.ds(r0 + HGRN_BLOCK - 1, 1), :]))
            upd.append(lax.dot_general(vb[g], kd_ref[rows[g], :], (((0,), (0,)), ((), ())),
                                       preferred_element_type=F32))
        states = []
        for g in range(group):
            if not carry and g % seq_blocks == 0:
                st = s0_ref[g // seq_blocks].T
            states.append(st.astype(BF16))
            st = st * decay[g] + upd[g]
            if not carry and (g + 1) % seq_blocks == 0:
                sout_ref[g // seq_blocks] = st.T
        for g in range(group):
            o = lax.dot_general(qe_ref[rows[g], :], states[g], (((1,), (1,)), ((), ())),
                                preferred_element_type=F32)
            oacc_ref[rows[g], :] = o + jnp.dot(sc[g], vb[g], preferred_element_type=F32)
        return st

    if carry:
        st_ref[...] = lax.fori_loop(0, nblk // group, blocks, st_ref[...])
    else:
        blocks(0, None)

    o = oacc_ref[...]
    y = o * lax.rsqrt(jnp.mean(o * o, axis=-1, keepdims=True) + EPS) * gout_ref[...]
    o_ref[...] = (y * gate_ref[...]).astype(BF16)

    if carry:
        @pl.when(c == pl.num_programs(1) - 1)
        def _():
            sout_ref[...] = st_ref[...].T


def _hgrn_scan(ops, out_norm, state, jl, n_prompt_rows, seq):
    nb, nh, dk, dv = state.shape[1:]
    assert dk == LANES and dv == LANES
    scratch = lambda t: [pltpu.VMEM((dv, dk), F32), pltpu.VMEM((t, dv), F32)]

    def specs(t, row_of):
        return ([pl.BlockSpec((t, LANES), lambda h, c: (row_of(c), h))] * len(ops)
                + [pl.BlockSpec((None, 1, dv), lambda h, c: (jl, 0, 0))])

    out_norm = _rows3(out_norm)
    tp = _tile(n_prompt_rows, HGRN_TILE)
    o_p, s_p = pl.pallas_call(
        functools.partial(_hgrn_kernel, seq_blocks=None),
        grid=(nh, n_prompt_rows // tp),
        in_specs=specs(tp, lambda c: c) + [pl.BlockSpec((None, None, None, dk, dv),
                                                        lambda h, c: (jl, 0, h, 0, 0))],
        out_specs=[pl.BlockSpec((tp, dv), lambda h, c: (c, h)),
                   pl.BlockSpec((None, dk, dv), lambda h, c: (h, 0, 0))],
        out_shape=[jax.ShapeDtypeStruct((n_prompt_rows, nh * dv), BF16),
                   jax.ShapeDtypeStruct((nh, dk, dv), F32)],
        scratch_shapes=scratch(tp),
        compiler_params=_params("parallel", "arbitrary"),
        name="hgrn_prompt",
    )(*ops, out_norm, state)
    ns = _tile(nb, max(1, HGRN_GROUP * HGRN_BLOCK // seq))
    ts = ns * seq
    assert n_prompt_rows % ts == 0 and seq % HGRN_BLOCK == 0
    r0 = n_prompt_rows // ts
    o_s, s_s = pl.pallas_call(
        functools.partial(_hgrn_kernel, seq_blocks=seq // HGRN_BLOCK),
        grid=(nh, nb // ns),
        in_specs=specs(ts, lambda c: r0 + c) + [pl.BlockSpec((None, ns, None, dk, dv),
                                                              lambda h, c: (jl, c, h, 0, 0))],
        out_specs=[pl.BlockSpec((ts, dv), lambda h, c: (c, h)),
                   pl.BlockSpec((ns, None, dk, dv), lambda h, c: (c, h, 0, 0))],
        out_shape=[jax.ShapeDtypeStruct((nb * seq, nh * dv), BF16),
                   jax.ShapeDtypeStruct((nb, nh, dk, dv), F32)],
        scratch_shapes=scratch(ts),
        compiler_params=_params("parallel", "arbitrary"),
        name="hgrn_sample",
    )(*ops, out_norm, state)
    return (o_p, o_s), s_p, s_s


def _lanes(x, n):
    return x[:, :n] if n <= LANES else jnp.concatenate([x] * (n // LANES), axis=1)


def _scores(q, k):
    return lax.dot_general(q, k, (((1,), (1,)), ((), ())), preferred_element_type=F32)


def _softmax_step(c, s, m_ref, l_ref, mask=None, rows=slice(None)):
    if mask is not None:
        s = jnp.where(mask, s, -jnp.inf)
    m_prev = m_ref[c, rows]
    m_new = jnp.maximum(m_prev, jnp.max(s, axis=-1, keepdims=True))
    alpha = jnp.exp2(m_prev - m_new)
    p = jnp.exp2(s - _lanes(m_new, s.shape[1]))
    l_ref[c, rows] = alpha * l_ref[c, rows] + jnp.sum(p, axis=-1, keepdims=True)
    m_ref[c, rows] = m_new
    return alpha, p.astype(BF16)


def _pv_step(c, alpha, p, v, acc_ref, rows=slice(None)):
    acc_ref[c, rows] = (_lanes(alpha, v.shape[1]) * acc_ref[c, rows]
                        + jnp.dot(p, v, preferred_element_type=F32))


def _diff_finish(a0, l0, a1, l1, lam_refs, subln, lam_init):
    lq1, lk1, lq2, lk2 = [r[...] for r in lam_refs]
    lam = (jnp.exp(jnp.sum(lq1 * lk1, axis=-1, keepdims=True))
           - jnp.exp(jnp.sum(lq2 * lk2, axis=-1, keepdims=True)) + lam_init)
    o = a0 * _lanes(1.0 / l0, a0.shape[1]) - lam * (a1 * _lanes(1.0 / l1, a1.shape[1]))
    y = o * lax.rsqrt(jnp.mean(o * o, axis=-1, keepdims=True) + EPS)
    return (y * subln * (1.0 - lam_init)).astype(BF16)


def _attn_prompt_kernel(it_ref, jt_ref, q_ref, k_ref, v_ref, lq1, lk1, lq2, lk2, sub_ref, o_ref,
                        m_ref, l_ref, acc_ref, *, lam_init, heads):
    p = pl.program_id(1)
    i, j = it_ref[p], jt_ref[p]
    bq, bk = q_ref.shape[0], k_ref.shape[0]
    dv = q_ref.shape[1] // heads
    dh = dv // 2

    @pl.when(j == 0)
    def _():
        m_ref[...] = jnp.full(m_ref.shape, -jnp.inf, F32)
        l_ref[...] = jnp.zeros(l_ref.shape, F32)
        acc_ref[...] = jnp.zeros(acc_ref.shape, F32)

    def steps(*parts):
        work = [(c, mask, rows, keys,
                 _scores(q_ref[rows, c * dh:(c + 1) * dh], k_ref[keys, c * dh:(c + 1) * dh]))
                for mask, rows, keys in parts for c in range(2 * heads)]
        for c, mask, rows, keys, s in work:
            v = v_ref[keys, (c // 2) * dv:(c // 2 + 1) * dv]
            alpha, p = _softmax_step(c, s, m_ref, l_ref, mask, rows)
            _pv_step(c, alpha, p, v, acc_ref, rows)

    @pl.when(j < i)
    def _():
        steps((None, slice(None), slice(None)))

    @pl.when(j == i)
    def _():
        half = bq // 2

        def chunk_mask(n_keys, first_row):
            qpos = first_row + lax.broadcasted_iota(jnp.int32, (half, n_keys), 0)
            kpos = lax.broadcasted_iota(jnp.int32, (half, n_keys), 1)
            return kpos < (qpos // CHUNK + 1) * CHUNK

        steps((chunk_mask(half, 0), slice(0, half), slice(0, half)),
              (chunk_mask(bk, half), slice(half, bq), slice(None)))
        for h in range(heads):
            o_ref[:, h * dv:(h + 1) * dv] = _diff_finish(
                acc_ref[2 * h], l_ref[2 * h], acc_ref[2 * h + 1], l_ref[2 * h + 1],
                (lq1, lk1, lq2, lk2), sub_ref[...], lam_init)


def _attn_prompt(qb, kb, vb, lams, subln, jl, lam_init, n_rows, nh):
    dv = qb.shape[1] // nh
    bq = bk = _tile(n_rows, ATTN_TILE)
    assert (bq // 2) % CHUNK == 0
    nq = n_rows // bq
    pairs = [(i, j) for i in range(nq) for j in range(i + 1)]
    it = jnp.asarray([p[0] for p in pairs], jnp.int32)
    jt = jnp.asarray([p[1] for p in pairs], jnp.int32)
    lam_spec = pl.BlockSpec((None, 1, dv // 2), lambda h, p, it, jt: (jl, 0, 0))
    hs = ATTN_HEADS if nh % ATTN_HEADS == 0 else 1
    return pl.pallas_call(
        functools.partial(_attn_prompt_kernel, lam_init=lam_init, heads=hs),
        grid_spec=pltpu.PrefetchScalarGridSpec(
            num_scalar_prefetch=2,
            grid=(nh // hs, len(pairs)),
            in_specs=[pl.BlockSpec((bq, hs * dv), lambda h, p, it, jt: (it[p], h)),
                      pl.BlockSpec((bk, hs * dv), lambda h, p, it, jt: (jt[p], h)),
                      pl.BlockSpec((bk, hs * dv), lambda h, p, it, jt: (jt[p], h)),
                      lam_spec, lam_spec, lam_spec, lam_spec,
                      pl.BlockSpec((None, 1, dv), lambda h, p, it, jt: (jl, 0, 0))],
            out_specs=pl.BlockSpec((bq, hs * dv), lambda h, p, it, jt: (it[p], h)),
            scratch_shapes=[pltpu.VMEM((2 * hs, bq, LANES), F32), pltpu.VMEM((2 * hs, bq, LANES), F32),
                            pltpu.VMEM((2 * hs, bq, dv), F32)]),
        out_shape=jax.ShapeDtypeStruct((n_rows, nh * dv), BF16),
        compiler_params=_params("parallel", "arbitrary"),
        name="attn_prompt",
    )(it, jt, qb, kb, vb, *[_rows3(a) for a in lams], _rows3(subln))


def _attn_sample_kernel(q_ref, ck_ref, cv_ref, kn_ref, vn_ref, lq1, lk1, lq2, lk2, sub_ref, o_ref,
                        m_ref, l_ref, acc_ref, *, lam_init, nh):
    j = pl.program_id(1)
    last = pl.num_programs(1) - 1
    dv = q_ref.shape[1] // nh
    dh = dv // 2

    @pl.when(j == 0)
    def _():
        m_ref[...] = jnp.full(m_ref.shape, -jnp.inf, F32)
        l_ref[...] = jnp.zeros(l_ref.shape, F32)
        acc_ref[...] = jnp.zeros(acc_ref.shape, F32)

    def q_of(h, c):
        return q_ref[:, h * dv + c * dh:h * dv + (c + 1) * dh]

    def step(k_of, v_of, scores_first):
        ss = [_scores(q_of(h, c), k_of(h, c)) for h in range(nh) for c in range(2)] if scores_first else None
        for h in range(nh):
            v = v_of(h)
            for c in range(2):
                s = ss[2 * h + c] if scores_first else _scores(q_of(h, c), k_of(h, c))
                a, p = _softmax_step(2 * h + c, s, m_ref, l_ref)
                _pv_step(2 * h + c, a, p, v, acc_ref)

    @pl.when(j < last)
    def _():
        ng = 2 * nh
        tok = LANES // ng
        i_out = lax.broadcasted_iota(jnp.int32, (LANES, LANES), 0)
        i_in = lax.broadcasted_iota(jnp.int32, (LANES, LANES), 1)
        perm = jnp.where(i_in == (i_out % tok) * ng + i_out // tok, 1.0, 0.0).astype(BF16)

        def regroup(src_ref):
            n_slab = src_ref.shape[0] // LANES
            wide = jnp.concatenate([src_ref[n * LANES:(n + 1) * LANES, :].astype(BF16)
                                    for n in range(n_slab)], axis=1)
            y = jnp.dot(perm, wide, preferred_element_type=F32)
            return [jnp.concatenate([y[g * tok:(g + 1) * tok, n * LANES:(n + 1) * LANES]
                                     for n in range(n_slab)], axis=0).astype(BF16) for g in range(ng)]

        kg, vg = regroup(ck_ref), regroup(cv_ref)
        step(lambda h, c: kg[2 * h + c],
             lambda h: jnp.concatenate([vg[h], vg[nh + h]], axis=1), False)

    @pl.when(j == last)
    def _():
        step(lambda h, c: kn_ref[:, h * dv + c * dh:h * dv + (c + 1) * dh],
             lambda h: vn_ref[:, h * dv:(h + 1) * dv], True)
        for h in range(nh):
            o_ref[:, h * dv:(h + 1) * dv] = _diff_finish(
                acc_ref[2 * h], l_ref[2 * h], acc_ref[2 * h + 1], l_ref[2 * h + 1],
                (lq1, lk1, lq2, lk2), sub_ref[...], lam_init)


def _attn_sample(qb, kb, vb, cache_k, cache_v, lams, subln, jl, lam_init, n_prompt_rows, seq, nh):
    d = qb.shape[1]
    dv = d // nh
    nb, past = cache_k.shape[1], cache_k.shape[2]
    bk = _tile(past, 512)
    nkc = past // bk
    r0 = n_prompt_rows // seq
    new_spec = pl.BlockSpec((seq, d), lambda b, j: (r0 + b, 0))
    nl = cache_k.shape[0]
    ck = cache_k.reshape(nl, nb, past * nh * 2, dv // 2)
    cv = cache_v.reshape(nl, nb, past, nh, 2, dv // 2).transpose(0, 1, 2, 4, 3, 5).reshape(ck.shape)
    cache_spec = pl.BlockSpec((None, None, bk * nh * 2, dv // 2),
                              lambda b, j: (jl, b, jnp.minimum(j, nkc - 1), 0))
    lam_spec = pl.BlockSpec((None, 1, dv // 2), lambda b, j: (jl, 0, 0))
    return pl.pallas_call(
        functools.partial(_attn_sample_kernel, lam_init=lam_init, nh=nh),
        grid=(nb, nkc + 1),
        in_specs=[new_spec, cache_spec, cache_spec, new_spec, new_spec,
                  lam_spec, lam_spec, lam_spec, lam_spec,
                  pl.BlockSpec((None, 1, dv), lambda b, j: (jl, 0, 0))],
        out_specs=pl.BlockSpec((seq, d), lambda b, j: (b, 0)),
        out_shape=jax.ShapeDtypeStruct((nb * seq, d), BF16),
        scratch_shapes=[pltpu.VMEM((2 * nh, seq, LANES), F32), pltpu.VMEM((2 * nh, seq, LANES), F32),
                        pltpu.VMEM((2 * nh, seq, dv), F32)],
        compiler_params=_params("parallel", "arbitrary"),
        name="attn_sample",
    )(qb, ck, cv, kb, vb, *[_rows3(a) for a in lams], _rows3(subln))


def kernel(x_prompt, x_sample, state_hgrn, cache_k, cache_v, state_ffn_conv, norm_mix, norm_ffn, hgrn_lower_bounds, w_hgrn_in, w_hgrn_out, hgrn_out_norm, w_diff_in, w_diff_out, diff_q_norm, diff_k_norm, diff_lambda_q1, diff_lambda_k1, diff_lambda_q2, diff_lambda_k2, diff_subln, w_ffn_up, ffn_conv_w, ffn_conv_b, w_ffn_down):
    bp, seq_p, d = x_prompt.shape
    nb, seq_s, _ = x_sample.shape
    assert bp == 1 and seq_s == CHUNK
    depth = norm_mix.shape[0]
    n_mixers = 2
    n_p = bp * seq_p
    diff_heads, dh = cache_k.shape[3], cache_k.shape[5]
    x = (x_prompt.reshape(n_p, d), x_sample.reshape(nb * seq_s, d))

    hgrn_p, hgrn_s, kfs, vfs, tails = [], [], [], [], []
    for i in range(depth):
        jl = i // n_mixers
        h = _rms_norm_bf16(x, norm_mix, i)
        if i % n_mixers == 0:
            ops = _hgrn_project(h, w_hgrn_in, hgrn_lower_bounds, i, jl)
            o, s_p, s_s = _hgrn_scan(ops, hgrn_out_norm, state_hgrn, jl, n_p, seq_s)
            hgrn_p.append(s_p[None])
            hgrn_s.append(s_s)
            x = _matmul(o, w_hgrn_out, jl, n_p, res=x, bm=512, bn=1024, w_buffers=1, name="hgrn_out")
        else:
            lam_init = 0.8 - 0.6 * math.exp(-0.3 * i)
            lams = (diff_lambda_q1, diff_lambda_k1, diff_lambda_q2, diff_lambda_k2)
            qb, kf, kb, vf, vb = _diff_project(h, w_diff_in, jl, diff_q_norm, diff_k_norm, dh, n_p)
            o_p = _attn_prompt(qb, kb, vb, lams, diff_subln, jl, lam_init, n_p, diff_heads)
            o_s = _attn_sample(qb, kb, vb, cache_k, cache_v, lams, diff_subln, jl, lam_init,
                               n_p, seq_s, diff_heads)
            kfs.append(kf)
            vfs.append(vf)
            x = _matmul((o_p, o_s), w_diff_out, jl, n_p, res=x, bm=512, bn=1024, w_buffers=1,
                        name="diff_out")
        h = _rms_norm_bf16(x, norm_ffn, i)
        act, tail = _ffn_up(h, w_ffn_up, ffn_conv_w, ffn_conv_b, state_ffn_conv, i, n_p, seq_s)
        tails.append(tail)
        x = _matmul(act, w_ffn_down, i, n_p, res=x, split_out=i == depth - 1, bm=512, bn=1024,
                    w_buffers=1, name="ffn_down")

    x_p, x_s = x
    tail = jnp.stack(tails)
    seg_p = n_p // seq_s
    stack = lambda pairs, k: jnp.stack([p[k] for p in pairs])
    return (x_p.reshape(bp, seq_p, d),
            x_s.reshape(nb, seq_s, d),
            jnp.stack(hgrn_p),
            jnp.stack(hgrn_s),
            stack(kfs, 0).reshape(-1, bp, seq_p, diff_heads, 2, dh),
            stack(vfs, 0).reshape(-1, bp, seq_p, diff_heads, 2 * dh),
            stack(kfs, 1).reshape(-1, nb, seq_s, diff_heads, 2, dh),
            stack(vfs, 1).reshape(-1, nb, seq_s, diff_heads, 2 * dh),
            tail[:, seg_p - 1][:, None],
            tail[:, seg_p:])
```

```python
import functools
import math

import jax
import jax.numpy as jnp
from jax import lax
from jax.experimental import pallas as pl
from jax.experimental.pallas import tpu as pltpu

EPS = 1e-6
LOG2E = math.log2(math.e)
CHUNK = 64
HGRN_BLOCK = 16
ATTN_TILE = 1024
ATTN_HEADS = 2
HGRN_TILE = 2048
HGRN_GROUP = 32
CONV_W = 3
LANES = 128
SUBLANES = 8
V7X_VMEM_BYTES = 64 * 1024 * 1024
VMEM_RESERVE_BYTES = 8 * 1024 * 1024
VMEM_LIMIT = V7X_VMEM_BYTES - VMEM_RESERVE_BYTES

F32 = jnp.float32
BF16 = jnp.bfloat16


def _params(*sem):
    return pltpu.CompilerParams(dimension_semantics=sem, vmem_limit_bytes=VMEM_LIMIT)


def _rows3(a):
    return a.reshape(a.shape[0], 1, a.shape[1])


def _tile(n, pref):
    t = min(n, pref)
    while n % t:
        t //= 2
    return t


def _split_specs(n_p, n_s, bm, bn, col):
    npt, nst = n_p // bm, n_s // bm
    return [pl.BlockSpec((bm, bn), lambda j, i: (jnp.minimum(i, npt - 1), col(j))),
            pl.BlockSpec((bm, bn), lambda j, i: (jnp.clip(i - npt, 0, nst - 1), col(j)))]


def _on_rows(i, npt, split, fn, *ref_pairs):
    if not split:
        fn(*[p[0] for p in ref_pairs])
        return
    pl.when(i < npt)(lambda: fn(*[p[0] for p in ref_pairs]))
    pl.when(i >= npt)(lambda: fn(*[p[-1] for p in ref_pairs]))


def _norm_kernel(*refs, npt):
    *x_refs, g_ref, o_ref = refs

    def run(x_ref):
        x = x_ref[...]
        y = x * lax.rsqrt(jnp.mean(x * x, axis=-1, keepdims=True) + EPS)
        o_ref[...] = (y * g_ref[...]).astype(o_ref.dtype)

    _on_rows(pl.program_id(0), npt, len(x_refs) == 2, run, x_refs)


def _rms_norm_bf16(x, gains, layer):
    xs = x if isinstance(x, tuple) else (x,)
    d = xs[0].shape[1]
    m = sum(a.shape[0] for a in xs)
    bm = _tile(math.gcd(*[a.shape[0] for a in xs]), 512)
    npt = xs[0].shape[0] // bm
    if len(xs) == 2:
        nst = xs[1].shape[0] // bm
        x_specs = [pl.BlockSpec((bm, d), lambda i: (jnp.minimum(i, npt - 1), 0)),
                   pl.BlockSpec((bm, d), lambda i: (jnp.clip(i - npt, 0, nst - 1), 0))]
    else:
        x_specs = [pl.BlockSpec((bm, d), lambda i: (i, 0))]
    return pl.pallas_call(
        functools.partial(_norm_kernel, npt=npt),
        grid=(m // bm,),
        in_specs=x_specs + [pl.BlockSpec((None, 1, d), lambda i: (layer, 0, 0))],
        out_specs=pl.BlockSpec((bm, d), lambda i: (i, 0)),
        out_shape=jax.ShapeDtypeStruct((m, d), BF16),
        compiler_params=_params("parallel"),
        name="rms_norm",
    )(*xs, _rows3(gains))


def _mm_kernel(*refs, n_res, split_in, split_out, npt):
    refs = list(refs)
    h_refs = [refs.pop(0) for _ in range(2 if split_in else 1)]
    w_ref = refs.pop(0)
    r_refs = [refs.pop(0) for _ in range(n_res)] or [None]
    o_refs = [refs.pop(0) for _ in range(2 if split_out else 1)]
    wb_ref, = refs
    i = pl.program_id(1)

    @pl.when(i == 0)
    def _():
        wb_ref[...] = w_ref[...].astype(BF16)

    def run(h_ref, r_ref, o_ref):
        acc = jnp.dot(h_ref[...], wb_ref[...], preferred_element_type=F32)
        o_ref[...] = acc if r_ref is None else r_ref[...] + acc

    _on_rows(i, npt, split_in or split_out or n_res == 2, run, h_refs, r_refs, o_refs)


def _matmul(h, w, layer, n_p, *, res=None, split_out=False, bm=1024, bn=512, w_buffers=2, name="matmul"):
    split_in = isinstance(h, tuple)
    k, ncols = w.shape[1], w.shape[2]
    m = sum(a.shape[0] for a in h) if split_in else h.shape[0]
    n_s = m - n_p
    bm, bn = _tile(math.gcd(n_p, n_s), bm), _tile(ncols, bn)
    npt = n_p // bm
    in_specs = (_split_specs(n_p, n_s, bm, k, lambda j: 0) if split_in
                else [pl.BlockSpec((bm, k), lambda j, i: (i, 0))])
    in_specs.append(pl.BlockSpec((None, k, bn), lambda j, i: (layer, 0, j),
                                 pipeline_mode=pl.Buffered(w_buffers)))
    args = list(h) if split_in else [h]
    args.append(w)
    res = () if res is None else res if isinstance(res, tuple) else (res,)
    in_specs += (_split_specs(n_p, n_s, bm, bn, lambda j: j) if len(res) == 2
                 else [pl.BlockSpec((bm, bn), lambda j, i: (i, j))] * len(res))
    args += res
    if split_out:
        out_specs = _split_specs(n_p, n_s, bm, bn, lambda j: j)
        out_shape = [jax.ShapeDtypeStruct((n_p, ncols), F32), jax.ShapeDtypeStruct((n_s, ncols), F32)]
    else:
        out_specs = pl.BlockSpec((bm, bn), lambda j, i: (i, j))
        out_shape = jax.ShapeDtypeStruct((m, ncols), F32)
    return pl.pallas_call(
        functools.partial(_mm_kernel, n_res=len(res), split_in=split_in, split_out=split_out, npt=npt),
        grid=(ncols // bn, m // bm),
        in_specs=in_specs,
        out_specs=out_specs,
        out_shape=out_shape,
        scratch_shapes=[pltpu.VMEM((k, bn), BF16)],
        compiler_params=_params("parallel", "arbitrary"),
        name=name,
    )(*args)


def _headnorm(acc, g, scale):
    outs = []
    for c in range(acc.shape[1] // LANES):
        blk = acc[:, c * LANES:(c + 1) * LANES]
        y = blk * lax.rsqrt(jnp.mean(blk * blk, axis=-1, keepdims=True) + EPS)
        outs.append(y * g * scale if scale != 1.0 else y * g)
    return jnp.concatenate(outs, axis=1) if len(outs) > 1 else outs[0]


def _qkv_kernel(h_ref, wq_ref, wk_ref, wv_ref, gq_ref, gk_ref,
                qb_ref, kfp_ref, kfs_ref, kb_ref, vfp_ref, vfs_ref, vb_ref,
                wqb_ref, wkb_ref, wvb_ref, *, scale, npt):
    i = pl.program_id(1)

    @pl.when(i == 0)
    def _():
        wqb_ref[...] = wq_ref[...].astype(BF16)
        wkb_ref[...] = wk_ref[...].astype(BF16)
        wvb_ref[...] = wv_ref[...].astype(BF16)

    h = h_ref[...]
    q = jnp.dot(h, wqb_ref[...], preferred_element_type=F32)
    k = jnp.dot(h, wkb_ref[...], preferred_element_type=F32)
    v = jnp.dot(h, wvb_ref[...], preferred_element_type=F32)
    qb_ref[...] = _headnorm(q, gq_ref[...], scale).astype(BF16)
    kn = _headnorm(k, gk_ref[...], 1.0)
    kb_ref[...] = kn.astype(BF16)
    vb_ref[...] = v.astype(BF16)

    def put(kf_ref, vf_ref):
        for g in range(kf_ref.shape[1]):
            kf_ref[:, g, :] = kn[:, g * LANES:(g + 1) * LANES]
        vf_ref[...] = v

    _on_rows(i, npt, True, put, (kfp_ref, kfs_ref), (vfp_ref, vfs_ref))


def _diff_project(h, w, layer, q_g, k_g, dh, n_p):
    m, d = h.shape
    n_s = m - n_p
    bm, bn = _tile(math.gcd(n_p, n_s), 1024), _tile(d, 512)
    nj = d // bn
    f_specs = _split_specs(n_p, n_s, bm, bn, lambda j: j)
    f_shapes = [jax.ShapeDtypeStruct((n_p, d), F32), jax.ShapeDtypeStruct((n_s, d), F32)]
    npt, nst, ng = n_p // bm, n_s // bm, bn // dh
    k_specs = [pl.BlockSpec((bm, None, ng, dh), lambda j, i: (jnp.minimum(i, npt - 1), j, 0, 0)),
               pl.BlockSpec((bm, None, ng, dh), lambda j, i: (jnp.clip(i - npt, 0, nst - 1), j, 0, 0))]
    k_shapes = [jax.ShapeDtypeStruct((n_p, nj, ng, dh), F32), jax.ShapeDtypeStruct((n_s, nj, ng, dh), F32)]
    o_spec = pl.BlockSpec((bm, bn), lambda j, i: (i, j))
    b_shape = jax.ShapeDtypeStruct((m, d), BF16)
    g_spec = pl.BlockSpec((None, 1, dh), lambda j, i: (layer, 0, 0))

    def w_spec(seg):
        return pl.BlockSpec((None, d, bn), lambda j, i: (layer, 0, seg * nj + j),
                            pipeline_mode=pl.Buffered(1))

    qb, kf_p, kf_s, kb, vf_p, vf_s, vb = pl.pallas_call(
        functools.partial(_qkv_kernel, scale=dh ** -0.5 * LOG2E, npt=n_p // bm),
        grid=(nj, m // bm),
        in_specs=[pl.BlockSpec((bm, d), lambda j, i: (i, 0)), w_spec(0), w_spec(1), w_spec(2),
                  g_spec, g_spec],
        out_specs=[o_spec] + k_specs + [o_spec] + f_specs + [o_spec],
        out_shape=[b_shape] + k_shapes + [b_shape] + f_shapes + [b_shape],
        scratch_shapes=[pltpu.VMEM((d, bn), BF16)] * 3,
        compiler_params=_params("parallel", "arbitrary"),
        name="diff_qkv",
    )(h, w, w, w, _rows3(q_g), _rows3(k_g))
    return qb, (kf_p, kf_s), kb, (vf_p, vf_s), vb


def _ffn_up_kernel(h_ref, wg_ref, wu_ref, cw_ref, cb_ref, st_ref, a_ref, tail_ref,
                   wgb_ref, wub_ref, g_ref, *, n_prompt_tiles, seq):
    i = pl.program_id(1)
    bm, bn = a_ref.shape
    nseg = bm // seq

    @pl.when(i == 0)
    def _():
        wgb_ref[...] = wg_ref[...].astype(BF16)
        wub_ref[...] = wu_ref[...].astype(BF16)
        g_ref[0:SUBLANES, :] = jnp.zeros((SUBLANES, bn), F32)

    h = h_ref[...]
    g = jnp.dot(h, wgb_ref[...], preferred_element_type=F32)
    u = jnp.dot(h, wub_ref[...], preferred_element_type=F32)
    g_ref[SUBLANES:SUBLANES + bm, :] = g
    g1 = g_ref[SUBLANES - 1:SUBLANES - 1 + bm, :]
    g2 = g_ref[SUBLANES - 2:SUBLANES - 2 + bm, :]
    st = st_ref[...]
    p2 = jnp.broadcast_to(st[:, 0:1, :], (nseg, seq, bn)).reshape(bm, bn)
    p1 = jnp.broadcast_to(st[:, 1:2, :], (nseg, seq, bn)).reshape(bm, bn)
    pos = (lax.broadcasted_iota(jnp.int32, (bm, bn), 0) % seq
           + jnp.where(i >= n_prompt_tiles, 0, seq))
    g1 = jnp.where(pos == 0, p1, g1)
    g2 = jnp.where(pos == 0, p2, jnp.where(pos == 1, p1, g2))
    cw = cw_ref[...]
    conv = cb_ref[...] + cw[2:3, :] * g + cw[1:2, :] * g1 + cw[0:1, :] * g2
    a_ref[...] = (conv * jax.nn.sigmoid(conv) * u).astype(BF16)

    for n in range(nseg):
        end = SUBLANES + (n + 1) * seq
        tail_ref[n] = g_ref[end - (CONV_W - 1):end, :]
    g_ref[0:SUBLANES, :] = g_ref[bm:bm + SUBLANES, :]


def _ffn_up(h, w_up, conv_w, conv_b, conv_state, layer, n_prompt_rows, seq):
    m, d = h.shape
    dff = conv_w.shape[-1]
    nb = conv_state.shape[1]
    bm = _tile(math.gcd(n_prompt_rows, nb * seq), 1024)
    bn = _tile(dff, 512)
    assert bm % seq == 0 and (m - n_prompt_rows) == nb * seq
    nseg = bm // seq
    npt = n_prompt_rows // bm
    nj = dff // bn
    return pl.pallas_call(
        functools.partial(_ffn_up_kernel, n_prompt_tiles=npt, seq=seq),
        grid=(nj, m // bm),
        in_specs=[pl.BlockSpec((bm, d), lambda j, i: (i, 0)),
                  pl.BlockSpec((None, d, bn), lambda j, i: (layer, 0, j)),
                  pl.BlockSpec((None, d, bn), lambda j, i: (layer, 0, nj + j)),
                  pl.BlockSpec((None, CONV_W, bn), lambda j, i: (layer, 0, j)),
                  pl.BlockSpec((None, 1, bn), lambda j, i: (layer, 0, j)),
                  pl.BlockSpec((None, nseg, CONV_W - 1, bn),
                               lambda j, i: (layer, jnp.maximum(i - npt, 0), 0, j))],
        out_specs=[pl.BlockSpec((bm, bn), lambda j, i: (i, j)),
                   pl.BlockSpec((nseg, CONV_W - 1, bn), lambda j, i: (i, 0, j))],
        out_shape=[jax.ShapeDtypeStruct((m, dff), BF16),
                   jax.ShapeDtypeStruct((m // seq, CONV_W - 1, dff), F32)],
        scratch_shapes=[pltpu.VMEM((d, bn), BF16), pltpu.VMEM((d, bn), BF16),
                        pltpu.VMEM((bm + SUBLANES, bn), F32)],
        compiler_params=_params("parallel", "arbitrary"),
        name="ffn_up",
    )(h, w_up, w_up, conv_w, _rows3(conv_b), conv_state)


def _hgrn_in_kernel(h_ref, wq_ref, wf_ref, wv_ref, wg_ref, lbp_ref,
                    q_ref, b2_ref, k_ref, qe_ref, kd_ref, v_ref, gate_ref,
                    wqb_ref, wfb_ref, wvb_ref, wgb_ref, *, layer):
    @pl.when(pl.program_id(1) == 0)
    def _():
        wqb_ref[...] = wq_ref[...].astype(BF16)
        wfb_ref[...] = wf_ref[...].astype(BF16)
        wvb_ref[...] = wv_ref[...].astype(BF16)
        wgb_ref[...] = wg_ref[...].astype(BF16)

    h = h_ref[...]
    rows, width = q_ref.shape
    q = jnp.dot(h, wqb_ref[...], preferred_element_type=F32)
    fz = jnp.dot(h, wfb_ref[...], preferred_element_type=F32)
    v_ref[...] = jnp.dot(h, wvb_ref[...], preferred_element_type=F32).astype(BF16)
    gate = jnp.dot(h, wgb_ref[...], preferred_element_type=F32)
    gate_ref[...] = (gate * jax.nn.sigmoid(gate)).astype(BF16)

    lbp = lbp_ref[...]
    e = jnp.exp(lbp - jnp.max(lbp, axis=0, keepdims=True))
    lb = jnp.sum(e[0:layer + 1], axis=0, keepdims=True) / jnp.sum(e, axis=0, keepdims=True)

    f = lb + (1.0 - lb) * jax.nn.sigmoid(fz)
    kk = 1.0 - f
    pos = lax.broadcasted_iota(jnp.int32, (rows, width), 0) % HGRN_BLOCK
    b = jnp.log(f)
    sh = 1
    while sh < HGRN_BLOCK:
        b = b + jnp.where(pos >= sh, pltpu.roll(b, sh, 0), 0.0)
        sh *= 2
    b3 = b.reshape(rows // HGRN_BLOCK, HGRN_BLOCK, width)
    b_last = jnp.broadcast_to(b3[:, HGRN_BLOCK - 1:, :], b3.shape).reshape(rows, width)
    q_ref[...] = q
    b2_ref[...] = b * LOG2E
    k_ref[...] = kk
    qe_ref[...] = (q * jnp.exp(b)).astype(BF16)
    kd_ref[...] = (kk * jnp.exp(b_last - b)).astype(BF16)


def _hgrn_project(h, w, lower_bounds, layer, jl):
    m, d = h.shape
    dk = w.shape[2] // 4
    bm, bn = _tile(m, 1024), _tile(dk, 256)
    nj = dk // bn

    def w_spec(seg):
        return pl.BlockSpec((None, d, bn), lambda j, i: (jl, 0, seg * nj + j))

    o_spec = pl.BlockSpec((bm, bn), lambda j, i: (i, j))
    f32_out, bf16_out = jax.ShapeDtypeStruct((m, dk), F32), jax.ShapeDtypeStruct((m, dk), BF16)
    return pl.pallas_call(
        functools.partial(_hgrn_in_kernel, layer=layer),
        grid=(nj, m // bm),
        in_specs=[pl.BlockSpec((bm, d), lambda j, i: (i, 0)), w_spec(0), w_spec(1), w_spec(2), w_spec(3),
                  pl.BlockSpec((lower_bounds.shape[0], bn), lambda j, i: (0, j))],
        out_specs=[o_spec] * 7,
        out_shape=[f32_out, f32_out, f32_out, bf16_out, bf16_out, bf16_out, bf16_out],
        scratch_shapes=[pltpu.VMEM((d, bn), BF16)] * 4,
        compiler_params=_params("parallel", "arbitrary"),
        name="hgrn_in",
    )(h, w, w, w, w, lower_bounds)


def _hgrn_kernel(q_ref, b_ref, k_ref, qe_ref, kd_ref, v_ref, gate_ref, gout_ref, s0_ref, o_ref, sout_ref,
                 st_ref, oacc_ref, *, seq_blocks):
    c = pl.program_id(1)
    t_rows = q_ref.shape[0]
    nblk = t_rows // HGRN_BLOCK
    carry = seq_blocks is None

    if carry:
        @pl.when(c == 0)
        def _():
            st_ref[...] = jnp.zeros_like(st_ref)

    half = HGRN_BLOCK // 2
    row = lax.broadcasted_iota(jnp.int32, (half, LANES), 0)
    lane = lax.broadcasted_iota(jnp.int32, (half, LANES), 1)

    def scores(r0):
        lo, hi = pl.ds(r0, half), pl.ds(r0 + half, half)
        b_lo, b_hi, q_lo, q_hi = b_ref[lo, :], b_ref[hi, :], q_ref[lo, :], q_ref[hi, :]
        b_mid = b_ref[pl.ds(r0 + half - 1, 1), :]
        q_in = (q_hi * jnp.exp2(b_hi - b_mid)).astype(BF16)
        k_out = (k_ref[lo, :] * jnp.exp2(b_mid - b_lo)).astype(BF16)
        k_out = jnp.concatenate([k_out, jnp.zeros((LANES - half, LANES), BF16)], axis=0)
        sc_hi = lax.dot_general(q_in, k_out, (((1,), (1,)), ((), ())), preferred_element_type=F32)
        sc_lo = jnp.zeros((half, LANES), F32)
        for s in range(HGRN_BLOCK):
            bs, ks = b_ref[pl.ds(r0 + s, 1), :], k_ref[pl.ds(r0 + s, 1), :]
            if s < half:
                col_lo = jnp.sum(jnp.exp2(b_lo - bs) * (q_lo * ks), axis=-1, keepdims=True)
                sc_lo = jnp.where(lane == s, col_lo, sc_lo)
            else:
                col_hi = jnp.sum(jnp.exp2(b_hi - bs) * (q_hi * ks), axis=-1, keepdims=True)
                sc_hi = jnp.where(lane == s, col_hi, sc_hi)
        sc = jnp.concatenate([jnp.where(row >= lane, sc_lo, 0.0),
                              jnp.where(row + half >= lane, sc_hi, 0.0)], axis=0)
        return sc[:, 0:HGRN_BLOCK].astype(BF16)

    group = min(nblk, HGRN_GROUP) if carry else nblk

    def blocks(jg, st):
        rows, sc, vb, decay, upd = [], [], [], [], []
        for g in range(group):
            r0 = pl.multiple_of((jg * group + g) * HGRN_BLOCK, HGRN_BLOCK)
            rows.append(pl.ds(r0, HGRN_BLOCK))
            vb.append(v_ref[rows[g], :])
            sc.append(scores(r0))
            decay.append(jnp.exp2(b_ref[pl.ds(r0 + HGRN_BLOCK - 1, 1), :]))
            upd.append(lax.dot_general(vb[g], kd_ref[rows[g], :], (((0,), (0,)), ((), ())),
                                       preferred_element_type=F32))
        states = []
        for g in range(group):
            if not carry and g % seq_blocks == 0:
                st = s0_ref[g // seq_blocks].T
            states.append(st.astype(BF16))
            st = st * decay[g] + upd[g]
            if not carry and (g + 1) % seq_blocks == 0:
                sout_ref[g // seq_blocks] = st.T
        for g in range(group):
            o = lax.dot_general(qe_ref[rows[g], :], states[g], (((1,), (1,)), ((), ())),
                                preferred_element_type=F32)
            oacc_ref[rows[g], :] = o + jnp.dot(sc[g], vb[g], preferred_element_type=F32)
        return st

    if carry:
        st_ref[...] = lax.fori_loop(0, nblk // group, blocks, st_ref[...])
    else:
        blocks(0, None)

    o = oacc_ref[...]
    y = o * lax.rsqrt(jnp.mean(o * o, axis=-1, keepdims=True) + EPS) * gout_ref[...]
    o_ref[...] = (y * gate_ref[...]).astype(BF16)

    if carry:
        @pl.when(c == pl.num_programs(1) - 1)
        def _():
            sout_ref[...] = st_ref[...].T


def _hgrn_scan(ops, out_norm, state, jl, n_prompt_rows, seq):
    nb, nh, dk, dv = state.shape[1:]
    assert dk == LANES and dv == LANES
    scratch = lambda t: [pltpu.VMEM((dv, dk), F32), pltpu.VMEM((t, dv), F32)]

    def specs(t, row_of):
        return ([pl.BlockSpec((t, LANES), lambda h, c: (row_of(c), h))] * len(ops)
                + [pl.BlockSpec((None, 1, dv), lambda h, c: (jl, 0, 0))])

    out_norm = _rows3(out_norm)
    tp = _tile(n_prompt_rows, HGRN_TILE)
    o_p, s_p = pl.pallas_call(
        functools.partial(_hgrn_kernel, seq_blocks=None),
        grid=(nh, n_prompt_rows // tp),
        in_specs=specs(tp, lambda c: c) + [pl.BlockSpec((None, None, None, dk, dv),
                                                        lambda h, c: (jl, 0, h, 0, 0))],
        out_specs=[pl.BlockSpec((tp, dv), lambda h, c: (c, h)),
                   pl.BlockSpec((None, dk, dv), lambda h, c: (h, 0, 0))],
        out_shape=[jax.ShapeDtypeStruct((n_prompt_rows, nh * dv), BF16),
                   jax.ShapeDtypeStruct((nh, dk, dv), F32)],
        scratch_shapes=scratch(tp),
        compiler_params=_params("parallel", "arbitrary"),
        name="hgrn_prompt",
    )(*ops, out_norm, state)
    ns = _tile(nb, max(1, HGRN_GROUP * HGRN_BLOCK // seq))
    ts = ns * seq
    assert n_prompt_rows % ts == 0 and seq % HGRN_BLOCK == 0
    r0 = n_prompt_rows // ts
    o_s, s_s = pl.pallas_call(
        functools.partial(_hgrn_kernel, seq_blocks=seq // HGRN_BLOCK),
        grid=(nh, nb // ns),
        in_specs=specs(ts, lambda c: r0 + c) + [pl.BlockSpec((None, ns, None, dk, dv),
                                                              lambda h, c: (jl, c, h, 0, 0))],
        out_specs=[pl.BlockSpec((ts, dv), lambda h, c: (c, h)),
                   pl.BlockSpec((ns, None, dk, dv), lambda h, c: (c, h, 0, 0))],
        out_shape=[jax.ShapeDtypeStruct((nb * seq, nh * dv), BF16),
                   jax.ShapeDtypeStruct((nb, nh, dk, dv), F32)],
        scratch_shapes=scratch(ts),
        compiler_params=_params("parallel", "arbitrary"),
        name="hgrn_sample",
    )(*ops, out_norm, state)
    return (o_p, o_s), s_p, s_s


def _lanes(x, n):
    return x[:, :n] if n <= LANES else jnp.concatenate([x] * (n // LANES), axis=1)


def _scores(q, k):
    return lax.dot_general(q, k, (((1,), (1,)), ((), ())), preferred_element_type=F32)


def _softmax_step(c, s, m_ref, l_ref, mask=None, rows=slice(None)):
    if mask is not None:
        s = jnp.where(mask, s, -jnp.inf)
    m_prev = m_ref[c, rows]
    m_new = jnp.maximum(m_prev, jnp.max(s, axis=-1, keepdims=True))
    alpha = jnp.exp2(m_prev - m_new)
    p = jnp.exp2(s - _lanes(m_new, s.shape[1]))
    l_ref[c, rows] = alpha * l_ref[c, rows] + jnp.sum(p, axis=-1, keepdims=True)
    m_ref[c, rows] = m_new
    return alpha, p.astype(BF16)


def _pv_step(c, alpha, p, v, acc_ref, rows=slice(None)):
    acc_ref[c, rows] = (_lanes(alpha, v.shape[1]) * acc_ref[c, rows]
                        + jnp.dot(p, v, preferred_element_type=F32))


def _diff_finish(a0, l0, a1, l1, lam_refs, subln, lam_init):
    lq1, lk1, lq2, lk2 = [r[...] for r in lam_refs]
    lam = (jnp.exp(jnp.sum(lq1 * lk1, axis=-1, keepdims=True))
           - jnp.exp(jnp.sum(lq2 * lk2, axis=-1, keepdims=True)) + lam_init)
    o = a0 * _lanes(1.0 / l0, a0.shape[1]) - lam * (a1 * _lanes(1.0 / l1, a1.shape[1]))
    y = o * lax.rsqrt(jnp.mean(o * o, axis=-1, keepdims=True) + EPS)
    return (y * subln * (1.0 - lam_init)).astype(BF16)


def _attn_prompt_kernel(it_ref, jt_ref, q_ref, k_ref, v_ref, lq1, lk1, lq2, lk2, sub_ref, o_ref,
                        m_ref, l_ref, acc_ref, *, lam_init, heads):
    p = pl.program_id(1)
    i, j = it_ref[p], jt_ref[p]
    bq, bk = q_ref.shape[0], k_ref.shape[0]
    dv = q_ref.shape[1] // heads
    dh = dv // 2

    @pl.when(j == 0)
    def _():
        m_ref[...] = jnp.full(m_ref.shape, -jnp.inf, F32)
        l_ref[...] = jnp.zeros(l_ref.shape, F32)
        acc_ref[...] = jnp.zeros(acc_ref.shape, F32)

    def steps(*parts):
        work = [(c, mask, rows, keys,
                 _scores(q_ref[rows, c * dh:(c + 1) * dh], k_ref[keys, c * dh:(c + 1) * dh]))
                for mask, rows, keys in parts for c in range(2 * heads)]
        for c, mask, rows, keys, s in work:
            v = v_ref[keys, (c // 2) * dv:(c // 2 + 1) * dv]
            alpha, p = _softmax_step(c, s, m_ref, l_ref, mask, rows)
            _pv_step(c, alpha, p, v, acc_ref, rows)

    @pl.when(j < i)
    def _():
        steps((None, slice(None), slice(None)))

    @pl.when(j == i)
    def _():
        half = bq // 2

        def chunk_mask(n_keys, first_row):
            qpos = first_row + lax.broadcasted_iota(jnp.int32, (half, n_keys), 0)
            kpos = lax.broadcasted_iota(jnp.int32, (half, n_keys), 1)
            return kpos < (qpos // CHUNK + 1) * CHUNK

        steps((chunk_mask(half, 0), slice(0, half), slice(0, half)),
              (chunk_mask(bk, half), slice(half, bq), slice(None)))
        for h in range(heads):
            o_ref[:, h * dv:(h + 1) * dv] = _diff_finish(
                acc_ref[2 * h], l_ref[2 * h], acc_ref[2 * h + 1], l_ref[2 * h + 1],
                (lq1, lk1, lq2, lk2), sub_ref[...], lam_init)


def _attn_prompt(qb, kb, vb, lams, subln, jl, lam_init, n_rows, nh):
    dv = qb.shape[1] // nh
    bq = bk = _tile(n_rows, ATTN_TILE)
    assert (bq // 2) % CHUNK == 0
    nq = n_rows // bq
    pairs = [(i, j) for i in range(nq) for j in range(i + 1)]
    it = jnp.asarray([p[0] for p in pairs], jnp.int32)
    jt = jnp.asarray([p[1] for p in pairs], jnp.int32)
    lam_spec = pl.BlockSpec((None, 1, dv // 2), lambda h, p, it, jt: (jl, 0, 0))
    hs = ATTN_HEADS if nh % ATTN_HEADS == 0 else 1
    return pl.pallas_call(
        functools.partial(_attn_prompt_kernel, lam_init=lam_init, heads=hs),
        grid_spec=pltpu.PrefetchScalarGridSpec(
            num_scalar_prefetch=2,
            grid=(nh // hs, len(pairs)),
            in_specs=[pl.BlockSpec((bq, hs * dv), lambda h, p, it, jt: (it[p], h)),
                      pl.BlockSpec((bk, hs * dv), lambda h, p, it, jt: (jt[p], h)),
                      pl.BlockSpec((bk, hs * dv), lambda h, p, it, jt: (jt[p], h)),
                      lam_spec, lam_spec, lam_spec, lam_spec,
                      pl.BlockSpec((None, 1, dv), lambda h, p, it, jt: (jl, 0, 0))],
            out_specs=pl.BlockSpec((bq, hs * dv), lambda h, p, it, jt: (it[p], h)),
            scratch_shapes=[pltpu.VMEM((2 * hs, bq, LANES), F32), pltpu.VMEM((2 * hs, bq, LANES), F32),
                            pltpu.VMEM((2 * hs, bq, dv), F32)]),
        out_shape=jax.ShapeDtypeStruct((n_rows, nh * dv), BF16),
        compiler_params=_params("parallel", "arbitrary"),
        name="attn_prompt",
    )(it, jt, qb, kb, vb, *[_rows3(a) for a in lams], _rows3(subln))


def _attn_sample_kernel(q_ref, ck_ref, cv_ref, kn_ref, vn_ref, lq1, lk1, lq2, lk2, sub_ref, o_ref,
                        m_ref, l_ref, acc_ref, *, lam_init, nh):
    j = pl.program_id(1)
    last = pl.num_programs(1) - 1
    dv = q_ref.shape[1] // nh
    dh = dv // 2

    @pl.when(j == 0)
    def _():
        m_ref[...] = jnp.full(m_ref.shape, -jnp.inf, F32)
        l_ref[...] = jnp.zeros(l_ref.shape, F32)
        acc_ref[...] = jnp.zeros(acc_ref.shape, F32)

    def q_of(h, c):
        return q_ref[:, h * dv + c * dh:h * dv + (c + 1) * dh]

    def step(k_of, v_of, scores_first):
        ss = [_scores(q_of(h, c), k_of(h, c)) for h in range(nh) for c in range(2)] if scores_first else None
        for h in range(nh):
            v = v_of(h)
            for c in range(2):
                s = ss[2 * h + c] if scores_first else _scores(q_of(h, c), k_of(h, c))
                a, p = _softmax_step(2 * h + c, s, m_ref, l_ref)
                _pv_step(2 * h + c, a, p, v, acc_ref)

    @pl.when(j < last)
    def _():
        ng = 2 * nh
        tok = LANES // ng
        i_out = lax.broadcasted_iota(jnp.int32, (LANES, LANES), 0)
        i_in = lax.broadcasted_iota(jnp.int32, (LANES, LANES), 1)
        perm = jnp.where(i_in == (i_out % tok) * ng + i_out // tok, 1.0, 0.0).astype(BF16)

        def regroup(src_ref):
            n_slab = src_ref.shape[0] // LANES
            wide = jnp.concatenate([src_ref[n * LANES:(n + 1) * LANES, :].astype(BF16)
                                    for n in range(n_slab)], axis=1)
            y = jnp.dot(perm, wide, preferred_element_type=F32)
            return [jnp.concatenate([y[g * tok:(g + 1) * tok, n * LANES:(n + 1) * LANES]
                                     for n in range(n_slab)], axis=0).astype(BF16) for g in range(ng)]

        kg, vg = regroup(ck_ref), regroup(cv_ref)
        step(lambda h, c: kg[2 * h + c],
             lambda h: jnp.concatenate([vg[h], vg[nh + h]], axis=1), False)

    @pl.when(j == last)
    def _():
        step(lambda h, c: kn_ref[:, h * dv + c * dh:h * dv + (c + 1) * dh],
             lambda h: vn_ref[:, h * dv:(h + 1) * dv], True)
        for h in range(nh):
            o_ref[:, h * dv:(h + 1) * dv] = _diff_finish(
                acc_ref[2 * h], l_ref[2 * h], acc_ref[2 * h + 1], l_ref[2 * h + 1],
                (lq1, lk1, lq2, lk2), sub_ref[...], lam_init)


def _attn_sample(qb, kb, vb, cache_k, cache_v, lams, subln, jl, lam_init, n_prompt_rows, seq, nh):
    d = qb.shape[1]
    dv = d // nh
    nb, past = cache_k.shape[1], cache_k.shape[2]
    bk = _tile(past, 512)
    nkc = past // bk
    r0 = n_prompt_rows // seq
    new_spec = pl.BlockSpec((seq, d), lambda b, j: (r0 + b, 0))
    nl = cache_k.shape[0]
    ck = cache_k.reshape(nl, nb, past * nh * 2, dv // 2)
    cv = cache_v.reshape(nl, nb, past, nh, 2, dv // 2).transpose(0, 1, 2, 4, 3, 5).reshape(ck.shape)
    cache_spec = pl.BlockSpec((None, None, bk * nh * 2, dv // 2),
                              lambda b, j: (jl, b, jnp.minimum(j, nkc - 1), 0))
    lam_spec = pl.BlockSpec((None, 1, dv // 2), lambda b, j: (jl, 0, 0))
    return pl.pallas_call(
        functools.partial(_attn_sample_kernel, lam_init=lam_init, nh=nh),
        grid=(nb, nkc + 1),
        in_specs=[new_spec, cache_spec, cache_spec, new_spec, new_spec,
                  lam_spec, lam_spec, lam_spec, lam_spec,
                  pl.BlockSpec((None, 1, dv), lambda b, j: (jl, 0, 0))],
        out_specs=pl.BlockSpec((seq, d), lambda b, j: (b, 0)),
        out_shape=jax.ShapeDtypeStruct((nb * seq, d), BF16),
        scratch_shapes=[pltpu.VMEM((2 * nh, seq, LANES), F32), pltpu.VMEM((2 * nh, seq, LANES), F32),
                        pltpu.VMEM((2 * nh, seq, dv), F32)],
        compiler_params=_params("parallel", "arbitrary"),
        name="attn_sample",
    )(qb, ck, cv, kb, vb, *[_rows3(a) for a in lams], _rows3(subln))


def kernel(x_prompt, x_sample, state_hgrn, cache_k, cache_v, state_ffn_conv, norm_mix, norm_ffn, hgrn_lower_bounds, w_hgrn_in, w_hgrn_out, hgrn_out_norm, w_diff_in, w_diff_out, diff_q_norm, diff_k_norm, diff_lambda_q1, diff_lambda_k1, diff_lambda_q2, diff_lambda_k2, diff_subln, w_ffn_up, ffn_conv_w, ffn_conv_b, w_ffn_down):
    bp, seq_p, d = x_prompt.shape
    nb, seq_s, _ = x_sample.shape
    assert bp == 1 and seq_s == CHUNK
    depth = norm_mix.shape[0]
    n_mixers = 2
    n_p = bp * seq_p
    diff_heads, dh = cache_k.shape[3], cache_k.shape[5]
    x = (x_prompt.reshape(n_p, d), x_sample.reshape(nb * seq_s, d))

    hgrn_p, hgrn_s, kfs, vfs, tails = [], [], [], [], []
    for i in range(depth):
        jl = i // n_mixers
        h = _rms_norm_bf16(x, norm_mix, i)
        if i % n_mixers == 0:
            ops = _hgrn_project(h, w_hgrn_in, hgrn_lower_bounds, i, jl)
            o, s_p, s_s = _hgrn_scan(ops, hgrn_out_norm, state_hgrn, jl, n_p, seq_s)
            hgrn_p.append(s_p[None])
            hgrn_s.append(s_s)
            x = _matmul(o, w_hgrn_out, jl, n_p, res=x, bm=512, bn=1024, w_buffers=1, name="hgrn_out")
        else:
            lam_init = 0.8 - 0.6 * math.exp(-0.3 * i)
            lams = (diff_lambda_q1, diff_lambda_k1, diff_lambda_q2, diff_lambda_k2)
            qb, kf, kb, vf, vb = _diff_project(h, w_diff_in, jl, diff_q_norm, diff_k_norm, dh, n_p)
            o_p = _attn_prompt(qb, kb, vb, lams, diff_subln, jl, lam_init, n_p, diff_heads)
            o_s = _attn_sample(qb, kb, vb, cache_k, cache_v, lams, diff_subln, jl, lam_init,
                               n_p, seq_s, diff_heads)
            kfs.append(kf)
            vfs.append(vf)
            x = _matmul((o_p, o_s), w_diff_out, jl, n_p, res=x, bm=512, bn=1024, w_buffers=1,
                        name="diff_out")
        h = _rms_norm_bf16(x, norm_ffn, i)
        act, tail = _ffn_up(h, w_ffn_up, ffn_conv_w, ffn_conv_b, state_ffn_conv, i, n_p, seq_s)
        tails.append(tail)
        x = _matmul(act, w_ffn_down, i, n_p, res=x, split_out=i == depth - 1, bm=512, bn=1024,
                    w_buffers=1, name="ffn_down")

    x_p, x_s = x
    tail = jnp.stack(tails)
    seg_p = n_p // seq_s
    stack = lambda pairs, k: jnp.stack([p[k] for p in pairs])
    return (x_p.reshape(bp, seq_p, d),
            x_s.reshape(nb, seq_s, d),
            jnp.stack(hgrn_p),
            jnp.stack(hgrn_s),
            stack(kfs, 0).reshape(-1, bp, seq_p, diff_heads, 2, dh),
            stack(vfs, 0).reshape(-1, bp, seq_p, diff_heads, 2 * dh),
            stack(kfs, 1).reshape(-1, nb, seq_s, diff_heads, 2, dh),
            stack(vfs, 1).reshape(-1, nb, seq_s, diff_heads, 2 * dh),
            tail[:, seg_p - 1][:, None],
            tail[:, seg_p:])
```
